```python
import math
import jax, jax.numpy as jnp
from jax import lax
import numpy as np

D_MODEL = 1024
BATCH = 4
SEQ = 8192
DEPTH = 2

GRID_W = 64
CTX_LEN = 256

CONV_DIM = 512
CONV_WIDTH = 3
GLA_HEADS = 4
GLA_DK = 64
GLA_DV = 128
GLA_KEY = GLA_HEADS * GLA_DK
GLA_VAL = GLA_HEADS * GLA_DV
GLA_LOWRANK = 16
GLA_GATE_TEMP = 16.0
GLA_CHUNK = 64
GLA_COL_HEADS = 2
MIX_DIM = CONV_DIM + GLA_VAL

STATE_SIZES = (GLA_KEY, GLA_VAL, GLA_LOWRANK, GLA_LOWRANK)
STATE_COLS = GLA_KEY + GLA_VAL + 2 * GLA_LOWRANK
REST_SIZES = (GLA_KEY, GLA_VAL, 3 * CONV_DIM)
IN_COLS = STATE_COLS + GLA_KEY + GLA_VAL + 3 * CONV_DIM

N_EXPERTS = 256
TOP_K = 8
N_GROUPS = 8
TOPK_GROUPS = 4
EXPERT_HIDDEN = 256
SHARED_HIDDEN = 256
ROUTED_SCALE = 2.5
MOE_BLOCK = 128

DEEPNORM_ALPHA = (2 * DEPTH) ** 0.25
DEEPNORM_BETA = (8 * DEPTH) ** -0.25
LN_EPS = 1e-5
RMS_EPS = 1e-6

kernel_name = "hybrid_conv_gla_moe_dit"


def layer_norm(x, g, b):
    xf = x.astype(jnp.float32)
    mu = jnp.mean(xf, -1, keepdims=True)
    var = jnp.mean(jnp.square(xf - mu), -1, keepdims=True)
    return ((xf - mu) * lax.rsqrt(var + LN_EPS) * g.astype(jnp.float32) + b.astype(jnp.float32)).astype(x.dtype)


def adaln(cond, w_mod, b_mod):
    m = jax.nn.silu(cond) @ w_mod + b_mod
    return jnp.split(m, 6, axis=-1)


def modulate(h, shift, scale):
    return h * (1.0 + scale[:, None, :]) + shift[:, None, :]


def split_sizes(p, sizes):
    cuts = [int(s) for s in np.cumsum(sizes)[:-1]]
    return jnp.split(p, cuts, axis=-1)


def to_heads(a, dh):
    b, l, _ = a.shape
    return a.reshape(b, l, -1, dh).transpose(0, 2, 1, 3)


def flip(a):
    return jnp.flip(a, axis=2)


def grid_transpose(a, rows, cols):
    b, h, _, f = a.shape
    return a.reshape(b, h, rows, cols, f).swapaxes(2, 3).reshape(b, h, rows * cols, f)


def to_scan_order(a, rows):
    nr = GLA_HEADS - GLA_COL_HEADS
    return jnp.concatenate([a[:, :nr], grid_transpose(a[:, nr:], rows, GRID_W)], axis=1)


def from_scan_order(a, rows):
    nr = GLA_HEADS - GLA_COL_HEADS
    return jnp.concatenate([a[:, :nr], grid_transpose(a[:, nr:], GRID_W, rows)], axis=1)


def short_conv_mix(p_conv, w_conv, seg_len):
    bsz, l, _ = p_conv.shape
    b_gate, c_gate, v = jnp.split(p_conv, 3, axis=-1)
    u = (c_gate * v).reshape(bsz * (l // seg_len), seg_len, CONV_DIM)
    y = lax.conv_general_dilated(
        u, w_conv[:, None, :].astype(u.dtype), window_strides=(1,),
        padding=[(CONV_WIDTH // 2, CONV_WIDTH // 2)],
        dimension_numbers=("NWC", "WIO", "NWC"), feature_group_count=CONV_DIM)
    return b_gate * y.reshape(bsz, l, CONV_DIM)


def gla_state_inputs(p_state, w_decay_up, b_decay):
    pk, pv, paf, pab = split_sizes(p_state, STATE_SIZES)

    def log_decay(pa, d):
        z = (pa @ w_decay_up[d] + b_decay[d]).astype(jnp.float32)
        return to_heads(jax.nn.log_sigmoid(z) / GLA_GATE_TEMP, GLA_DK)

    return to_heads(pk, GLA_DK), to_heads(pv, GLA_DV), log_decay(paf, 0), log_decay(pab, 1)


def _chunked(a):
    b, h, t, f = a.shape
    return a.astype(jnp.float32).reshape(b, h, t // GLA_CHUNK, GLA_CHUNK, f)


def gla_states(k, v, log_a, s0):
    kc, vc = _chunked(k), _chunked(v)
    bcum = jnp.cumsum(_chunked(log_a), axis=3)
    b_last = bcum[:, :, :, -1, :]
    k_tail = kc * jnp.exp(b_last[:, :, :, None, :] - bcum)
    chunk_kv = jnp.einsum("bhnjk,bhnjv->bhnkv", k_tail, vc)

    def step(s, inp):
        dec, kv = inp
        return jnp.exp(dec)[..., None] * s + kv, s

    s_final, s_in = lax.scan(step, s0, (jnp.moveaxis(b_last, 2, 0), jnp.moveaxis(chunk_kv, 2, 0)))
    return bcum, jnp.moveaxis(s_in, 0, 2), s_final


def gla_scan(q, k, v, log_a, s0):
    bsz, h, t, _ = q.shape
    bcum, s_in, s_final = gla_states(k, v, log_a, s0)
    qc, kc, vc = _chunked(q), _chunked(k), _chunked(v)
    q_dec = qc * jnp.exp(bcum)
    att = jnp.einsum("bhnik,bhnjk->bhnij", q_dec, kc * jnp.exp(-bcum))
    incl = jnp.tril(jnp.ones((GLA_CHUNK, GLA_CHUNK), dtype=bool))
    att = jnp.where(incl, att, 0.0)
    o = jnp.einsum("bhnij,bhnjv->bhniv", att, vc) + jnp.einsum("bhnik,bhnkv->bhniv", q_dec, s_in)
    return o.reshape(bsz, h, t, GLA_DV), s_final


def gla_bidir(q, k, v, la_f, la_b, s0_f, s0_b):
    o_f, s_f = gla_scan(q, k, v, la_f, s0_f)
    o_b, s_b = gla_scan(flip(q), flip(k), flip(v), flip(la_b), s0_b)
    return o_f + flip(o_b), s_f, s_b


def gla_readout(o, gate, g_norm):
    b, h, l, dv = o.shape
    o = o * lax.rsqrt(jnp.mean(jnp.square(o), -1, keepdims=True) + RMS_EPS) * g_norm.astype(jnp.float32)
    o = o.transpose(0, 2, 1, 3).reshape(b, l, h * dv)
    return (o * jax.nn.silu(gate.astype(jnp.float32))).astype(gate.dtype)


def swiglu(u, w_gate, w_up, w_down):
    return (jax.nn.silu(u @ w_gate) * (u @ w_up)) @ w_down


def route(u, w_router, router_bias):
    n = u.shape[0]
    s = jax.nn.sigmoid((u @ w_router).astype(jnp.float32))
    biased = s + router_bias.astype(jnp.float32)
    grp = biased.reshape(n, N_GROUPS, N_EXPERTS // N_GROUPS)
    grp_score = jnp.sum(lax.top_k(grp, 2)[0], axis=-1)
    _, top_g = lax.top_k(grp_score, TOPK_GROUPS)
    gmask = jnp.any(top_g[..., None] == jnp.arange(N_GROUPS), axis=1)
    emask = jnp.repeat(gmask, N_EXPERTS // N_GROUPS, axis=-1)
    _, idx = lax.top_k(jnp.where(emask, biased, -jnp.inf), TOP_K)
    w = jnp.take_along_axis(s, idx, axis=-1)
    w = w / jnp.sum(w, -1, keepdims=True) * ROUTED_SCALE
    return idx, w


def routed_experts(u, idx, wts, w_gate, w_up, w_down):
    n, _ = u.shape
    nk = n * TOP_K
    flat_e = idx.reshape(nk)
    flat_w = wts.reshape(nk)
    flat_t = jnp.arange(nk, dtype=jnp.int32) // TOP_K
    order = jnp.argsort(flat_e)
    se, st, sw = flat_e[order], flat_t[order], flat_w[order]
    counts = jnp.bincount(flat_e, length=N_EXPERTS)
    start = jnp.cumsum(counts) - counts
    padded = (counts + MOE_BLOCK - 1) // MOE_BLOCK * MOE_BLOCK
    pend = jnp.cumsum(padded)
    pstart = pend - padded
    slot = pstart[se] + (jnp.arange(nk, dtype=jnp.int32) - start[se])
    n_blocks = -(-(nk + N_EXPERTS * (MOE_BLOCK - 1)) // MOE_BLOCK)
    n_slots = n_blocks * MOE_BLOCK
    slot_tok = jnp.zeros((n_slots,), jnp.int32).at[slot].set(st)
    slot_w = jnp.zeros((n_slots,), jnp.float32).at[slot].set(sw)
    block_start = jnp.arange(n_blocks, dtype=jnp.int32) * MOE_BLOCK
    block_e = jnp.minimum(jnp.searchsorted(pend, block_start, side="right"), N_EXPERTS - 1)
    total = pend[-1]

    def body(acc, blk):
        e, toks, ws, b0 = blk

        def add(a):
            xb = u[toks]
            yb = swiglu(xb, w_gate[e], w_up[e], w_down[e]) * ws[:, None].astype(u.dtype)
            return a.at[toks].add(yb.astype(a.dtype))

        return lax.cond(b0 < total, add, lambda a: a, acc), None

    out, _ = lax.scan(body, jnp.zeros_like(u),
                      (block_e, slot_tok.reshape(n_blocks, MOE_BLOCK), slot_w.reshape(n_blocks, MOE_BLOCK), block_start))
    return out


def moe_ffn(u, w_router, router_bias, w_e_gate, w_e_up, w_e_down, w_s_gate, w_s_up, w_s_down):
    idx, wts = route(u, w_router, router_bias)
    return swiglu(u, w_s_gate, w_s_up, w_s_down) + routed_experts(u, idx, wts, w_e_gate, w_e_up, w_e_down)


def setup_inputs(seed: int = 0) -> dict:
    key = jax.random.key(seed)
    ks = iter(jax.random.split(key, 32))
    d = D_MODEL

    def nrm(shape, scale):
        return jax.random.normal(next(ks), shape, jnp.float32) * scale

    return {
        "x": nrm((BATCH, SEQ, d), 1.0),
        "c": nrm((BATCH, d), 1.0),
        "ctx": nrm((BATCH, CTX_LEN, d), 1.0),
        "c_ctx": nrm((d,), 1.0),
        "w_mod": nrm((DEPTH, d, 6 * d), 0.5 * d ** -0.5),
        "b_mod": nrm((DEPTH, 6 * d), 0.02),
        "w_in": nrm((DEPTH, d, IN_COLS), d ** -0.5),
        "w_conv": nrm((DEPTH, CONV_WIDTH, CONV_DIM), CONV_WIDTH ** -0.5),
        "w_decay_up": nrm((DEPTH, 2, GLA_LOWRANK, GLA_KEY), GLA_LOWRANK ** -0.5),
        "b_decay": nrm((DEPTH, 2, GLA_KEY), 0.5),
        "g_gla_norm": 1.0 + nrm((DEPTH, GLA_DV), 0.02),
        "w_out": nrm((DEPTH, MIX_DIM, d), DEEPNORM_BETA * MIX_DIM ** -0.5),
        "ln1_g": 1.0 + nrm((DEPTH, d), 0.02),
        "ln1_b": nrm((DEPTH, d), 0.02),
        "w_router": nrm((DEPTH, d, N_EXPERTS), d ** -0.5),
        "router_bias": nrm((DEPTH, N_EXPERTS), 0.01),
        "w_e_gate": nrm((DEPTH, N_EXPERTS, d, EXPERT_HIDDEN), d ** -0.5),
        "w_e_up": nrm((DEPTH, N_EXPERTS, d, EXPERT_HIDDEN), d ** -0.5),
        "w_e_down": nrm((DEPTH, N_EXPERTS, EXPERT_HIDDEN, d), DEEPNORM_BETA * EXPERT_HIDDEN ** -0.5),
        "w_s_gate": nrm((DEPTH, d, SHARED_HIDDEN), d ** -0.5),
        "w_s_up": nrm((DEPTH, d, SHARED_HIDDEN), d ** -0.5),
        "w_s_down": nrm((DEPTH, SHARED_HIDDEN, d), DEEPNORM_BETA * SHARED_HIDDEN ** -0.5),
        "ln2_g": 1.0 + nrm((DEPTH, d), 0.02),
        "ln2_b": nrm((DEPTH, d), 0.02),
    }


def reference(x, c, ctx, c_ctx, w_mod, b_mod, w_in, w_conv, w_decay_up, b_decay, g_gla_norm, w_out,
              ln1_g, ln1_b, w_router, router_bias, w_e_gate, w_e_up, w_e_down, w_s_gate, w_s_up, w_s_down,
              ln2_g, ln2_b):
    bsz, seq, d = x.shape
    rows = seq // GRID_W
    ctx_len = ctx.shape[1]
    q_scale = GLA_DK ** -0.5
    zero_state = jnp.zeros((bsz, GLA_HEADS, GLA_DK, GLA_DV), jnp.float32)

    for i in range(DEPTH):
        last = i == DEPTH - 1
        sh1, sc1, gt1, sh2, sc2, gt2 = adaln(c, w_mod[i], b_mod[i])
        sh1c, sc1c, gt1c, sh2c, sc2c, gt2c = adaln(c_ctx[None, :], w_mod[i], b_mod[i])

        hc = modulate(ctx, sh1c, sc1c)
        pc = hc @ (w_in[i][:, :STATE_COLS] if last else w_in[i])
        kc, vc, lafc, labc = gla_state_inputs(pc[..., :STATE_COLS], w_decay_up[i], b_decay[i])
        if last:
            _, _, s_ctx_f = gla_states(kc, vc, lafc, zero_state)
            _, _, s_ctx_b = gla_states(flip(kc), flip(vc), flip(labc), zero_state)
        else:
            qc, gc, conv_in_c = split_sizes(pc[..., STATE_COLS:], REST_SIZES)
            qc = to_heads(qc, GLA_DK) * q_scale
            oc, s_ctx_f, s_ctx_b = gla_bidir(qc, kc, vc, lafc, labc, zero_state, zero_state)
            mix_c = jnp.concatenate([short_conv_mix(conv_in_c, w_conv[i], ctx_len),
                                     gla_readout(oc, gc, g_gla_norm[i])], axis=-1)
            ctx_mixed = layer_norm(DEEPNORM_ALPHA * ctx + gt1c[:, None, :] * (mix_c @ w_out[i]), ln1_g[i], ln1_b[i])

        hx = modulate(x, sh1, sc1)
        px = hx @ w_in[i]
        kx, vx, lafx, labx = gla_state_inputs(px[..., :STATE_COLS], w_decay_up[i], b_decay[i])
        qx, gx, conv_in_x = split_sizes(px[..., STATE_COLS:], REST_SIZES)
        qx = to_heads(qx, GLA_DK) * q_scale
        qx, kx, vx, lafx, labx = [to_scan_order(a, rows) for a in (qx, kx, vx, lafx, labx)]
        ox, _, _ = gla_bidir(qx, kx, vx, lafx, labx, s_ctx_f, s_ctx_b)
        ox = from_scan_order(ox, rows)
        mix_x = jnp.concatenate([short_conv_mix(conv_in_x, w_conv[i], GRID_W),
                                 gla_readout(ox, gx, g_gla_norm[i])], axis=-1)
        x = layer_norm(DEEPNORM_ALPHA * x + gt1[:, None, :] * (mix_x @ w_out[i]), ln1_g[i], ln1_b[i])

        hx2 = modulate(x, sh2, sc2).reshape(-1, d)
        moe_w = (w_router[i], router_bias[i], w_e_gate[i], w_e_up[i], w_e_down[i], w_s_gate[i], w_s_up[i], w_s_down[i])
        if last:
            fx = moe_ffn(hx2, *moe_w)
        else:
            ctx = ctx_mixed
            hc2 = modulate(ctx, sh2c, sc2c).reshape(-1, d)
            nc = hc2.shape[0]
            f = moe_ffn(jnp.concatenate([hc2, hx2], axis=0), *moe_w)
            ctx = layer_norm(DEEPNORM_ALPHA * ctx + gt2c[:, None, :] * f[:nc].reshape(ctx.shape), ln2_g[i], ln2_b[i])
            fx = f[nc:]
        x = layer_norm(DEEPNORM_ALPHA * x + gt2[:, None, :] * fx.reshape(x.shape), ln2_g[i], ln2_b[i])

    return x
```

```python
import functools

import jax
import jax.numpy as jnp
from jax import lax
from jax.experimental import pallas as pl
from jax.experimental.pallas import tpu as pltpu

F32 = jnp.float32
BF16 = jnp.bfloat16
I32 = jnp.int32

GRID_W = 64
CONV_DIM = 512
GLA_HEADS = 4
GLA_DK = 64
GLA_DV = 128
GLA_KEY = GLA_HEADS * GLA_DK
GLA_VAL = GLA_HEADS * GLA_DV
GLA_LOWRANK = 16
GLA_GATE_TEMP = 16.0
GLA_CHUNK = 64
TOP_K = 8
N_GROUPS = 8
TOPK_GROUPS = 4
ROUTED_SCALE = 2.5
LN_EPS = 1e-5
RMS_EPS = 1e-6

LANES = 128
SUBLANES = 8
VMEM_LIMIT = 56 * 1024 * 1024

P_V = 0
P_G = 512
P_K = 1024
P_Q = 1280
P_CONV = 1536
P_DEC = 3072
P_COLS = 3200

TOK_TILE = 256
MOE_BLOCK = 128
CMB_TILE = 128
DSP_TILE = 512
ROW_TILES = 8


def _cparams(sem):
    return pltpu.CompilerParams(dimension_semantics=sem, vmem_limit_bytes=VMEM_LIMIT)


def _mod_kernel(c_ref, w_ref, b_ref, o_ref):
    c = c_ref[...]
    a = c * jax.nn.sigmoid(c)
    o_ref[0] = jnp.dot(a, w_ref[0], preferred_element_type=F32, precision=lax.Precision.HIGHEST) + b_ref[0]


def _modulation(cond8, w_mod, b_mod):
    depth, d, n = w_mod.shape
    tn = 1536
    return pl.pallas_call(
        _mod_kernel,
        grid=(depth, n // tn),
        in_specs=[
            pl.BlockSpec((SUBLANES, d), lambda i, j: (0, 0)),
            pl.BlockSpec((1, d, tn), lambda i, j: (i, 0, j)),
            pl.BlockSpec((1, 1, tn), lambda i, j: (i, 0, j)),
        ],
        out_specs=pl.BlockSpec((1, SUBLANES, tn), lambda i, j: (i, 0, j)),
        out_shape=jax.ShapeDtypeStruct((depth, SUBLANES, n), F32),
        compiler_params=_cparams(("arbitrary", "arbitrary")),
        name="adaln_mod",
    )(cond8, w_mod, b_mod.reshape(depth, 1, n))


def _inproj_kernel(x_ref, sh_ref, sc_ref, w_ref, o_ref):
    h = x_ref[...] * (1.0 + sc_ref[0]) + sh_ref[0]
    o_ref[...] = jnp.dot(h.astype(BF16), w_ref[...], preferred_element_type=F32)


def _inproj(h, shift, scale, w, group_of_tile):
    t, d = h.shape
    n = w.shape[1]
    grp = lambda i: (group_of_tile(i), 0, 0)
    return pl.pallas_call(
        _inproj_kernel,
        grid=(t // TOK_TILE,),
        in_specs=[
            pl.BlockSpec((TOK_TILE, d), lambda i: (i, 0)),
            pl.BlockSpec((1, 1, d), grp),
            pl.BlockSpec((1, 1, d), grp),
            pl.BlockSpec((d, n), lambda i: (0, 0)),
        ],
        out_specs=pl.BlockSpec((TOK_TILE, n), lambda i: (i, 0)),
        out_shape=jax.ShapeDtypeStruct((t, n), F32),
        compiler_params=_cparams(("arbitrary",)),
        name="inproj",
    )(h, shift, scale, w)


def _split3(x):
    a = x.astype(BF16)
    r = x - a.astype(F32)
    b = r.astype(BF16)
    c = (r - b.astype(F32)).astype(BF16)
    return a, b, c


def _dot(a, b):
    return jnp.dot(a, b, preferred_element_type=F32)


def _dot_nt(a, b):
    return lax.dot_general(a, b, (((1,), (1,)), ((), ())), preferred_element_type=F32)


def _gla_direction(k2, q2, v2, dec, wup, bdec, s_ref, o_ref, reverse):
    t = TOK_TILE
    c = GLA_CHUNK
    z = jnp.dot(dec, wup, preferred_element_type=F32, precision=lax.Precision.HIGHEST) + bdec
    la = (jnp.minimum(z, 0.0) - jnp.log(1.0 + jnp.exp(-jnp.abs(z)))) * (1.0 / GLA_GATE_TEMP)

    row = lax.broadcasted_iota(I32, (t, t), 0)
    col = lax.broadcasted_iota(I32, (t, t), 1)
    same = (row // c) == (col // c)
    tri = jnp.logical_and(same, (col >= row) if reverse else (col <= row))
    tri_b = jnp.where(tri, 1.0, 0.0).astype(BF16)
    same_b = jnp.where(same, 1.0, 0.0).astype(BF16)
    la3 = _split3(la)
    bcum = _dot(tri_b, la3[0]) + _dot(tri_b, la3[1]) + _dot(tri_b, la3[2])
    btot = _dot(same_b, la3[0]) + _dot(same_b, la3[1]) + _dot(same_b, la3[2])

    q_dec = q2 * jnp.exp(bcum) * (GLA_DK ** -0.5)
    k_inv = (k2 * jnp.exp(-bcum)).astype(BF16)
    k_tail = k2 * jnp.exp(btot - bcum)
    chunk_decay = jnp.exp(btot)

    lane = lax.broadcasted_iota(I32, (t, 2 * GLA_DK), 1)
    order = range(t // c - 1, -1, -1) if reverse else range(t // c)
    for hh in range(2):
        in_head = (lane // GLA_DK) == hh
        qd = jnp.where(in_head, q_dec, 0.0).astype(BF16)
        kt = jnp.where(in_head, k_tail, 0.0).astype(BF16)
        vh = v2[:, hh * GLA_DV:(hh + 1) * GLA_DV]
        vh_b = vh.astype(BF16)
        att = jnp.where(tri, _dot_nt(qd, k_inv), 0.0).astype(BF16)
        o_intra = _dot(att, vh_b)
        s = s_ref[hh]
        for ci in order:
            rows = slice(ci * c, (ci + 1) * c)
            o_ref[0, rows, hh * GLA_DV:(hh + 1) * GLA_DV] = o_intra[rows] + _dot_nt(qd[rows], s.astype(BF16))
            kv_t = _dot(vh[rows].T.astype(BF16), kt[rows])
            s = s * chunk_decay[ci * c:ci * c + 1, :] + kv_t
        s_ref[hh] = s


def _gla_kernel(kf, qf, vf, df, kb, qb, vb, db, wupf, wupb, bf, bb, of_ref, ob_ref, sf_ref, sb_ref):
    @pl.when(pl.program_id(1) == 0)
    def _():
        sf_ref[...] = jnp.zeros_like(sf_ref)
        sb_ref[...] = jnp.zeros_like(sb_ref)

    _gla_direction(kf[0], qf[0], vf[0], df[0], wupf[...], bf[...], sf_ref, of_ref, False)
    _gla_direction(kb[0], qb[0], vb[0], db[0], wupb[...], bb[...], sb_ref, ob_ref, True)


def _gla_pair(src, col_k, col_q, col_v, col_dec, wupf, wupb, bf, bb):
    bsz, l, _ = src.shape
    nb = l // TOK_TILE
    fwd = lambda j: j
    bwd = lambda j: jnp.where(j == 0, 0, nb - j)

    def spec(width, colidx, order):
        return pl.BlockSpec((1, TOK_TILE, width), lambda b, j: (b, order(j), colidx))

    wspec = pl.BlockSpec((LANES, LANES), lambda b, j: (0, 0))
    bspec = pl.BlockSpec((1, LANES), lambda b, j: (0, 0))
    out = jax.ShapeDtypeStruct((bsz, l, 2 * GLA_DV), F32)
    return pl.pallas_call(
        _gla_kernel,
        grid=(bsz, nb),
        in_specs=[spec(LANES, col_k, fwd), spec(LANES, col_q, fwd), spec(2 * GLA_DV, col_v, fwd), spec(LANES, col_dec, fwd),
                  spec(LANES, col_k, bwd), spec(LANES, col_q, bwd), spec(2 * GLA_DV, col_v, bwd), spec(LANES, col_dec, bwd),
                  wspec, wspec, bspec, bspec],
        out_specs=[pl.BlockSpec((1, TOK_TILE, 2 * GLA_DV), lambda b, j: (b, fwd(j), 0)),
                   pl.BlockSpec((1, TOK_TILE, 2 * GLA_DV), lambda b, j: (b, bwd(j), 0))],
        out_shape=[out, out],
        scratch_shapes=[pltpu.VMEM((2, GLA_DV, LANES), F32), pltpu.VMEM((2, GLA_DV, LANES), F32)],
        compiler_params=_cparams(("arbitrary", "arbitrary")),
        name="gla_pair",
    )(src, src, src, src, src, src, src, src, wupf, wupb, bf, bb)


def _layer_norm(v, g, b):
    mu = jnp.mean(v, axis=-1, keepdims=True)
    dv = v - mu
    var = jnp.mean(dv * dv, axis=-1, keepdims=True)
    return dv * lax.rsqrt(var + LN_EPS) * g + b


def _outproj_kernel(conv_ref, gate_ref, ofr, obr, ofc, obc, h_ref, gt_ref, wconv_ref, gn_ref, wout_ref,
                    lng_ref, lnb_ref, o_ref, *, alpha, tiles_per_batch):
    tm = TOK_TILE
    is_ctx = (pl.program_id(0) % tiles_per_batch) == 0
    seg = jnp.where(is_ctx, tm, GRID_W)

    cv = conv_ref[...]
    b_gate = cv[:, :CONV_DIM]
    u = cv[:, CONV_DIM:2 * CONV_DIM] * cv[:, 2 * CONV_DIM:]
    pos = lax.broadcasted_iota(I32, (tm, CONV_DIM), 0) & (seg - 1)
    prev = jnp.where(pos == 0, 0.0, pltpu.roll(u, 1, 0))
    nxt = jnp.where(pos == seg - 1, 0.0, pltpu.roll(u, tm - 1, 0))
    wc = wconv_ref[...]
    mix_conv = b_gate * (wc[0:1, :] * prev + wc[1:2, :] * u + wc[2:3, :] * nxt)

    o_r = ofr[...] + obr[...]
    o_c = ofc[...] + obc[...]
    gate = gate_ref[...]
    gn = gn_ref[...]
    parts = [mix_conv.astype(BF16)]
    for hd in range(GLA_HEADS):
        src = o_r if hd < 2 else o_c
        o = src[:, (hd % 2) * GLA_DV:(hd % 2 + 1) * GLA_DV]
        o = o * lax.rsqrt(jnp.mean(o * o, axis=-1, keepdims=True) + RMS_EPS) * gn
        g = gate[:, hd * GLA_DV:(hd + 1) * GLA_DV]
        parts.append((o * (g * jax.nn.sigmoid(g))).astype(BF16))
    mix = jnp.concatenate(parts, axis=-1)
    y = _dot(mix, wout_ref[...])
    v = alpha * h_ref[...] + gt_ref[0] * y
    o_ref[...] = _layer_norm(v, lng_ref[...], lnb_ref[...])


def _outproj(p, o_fr, o_br, o_fc, o_bc, h, gate1, w_conv, g_norm, w_out, ln_g, ln_b, group_of_tile, alpha, tiles_per_batch):
    t, d = h.shape
    tm = TOK_TILE
    tok = lambda width, colidx: pl.BlockSpec((tm, width), lambda i: (i, colidx))
    full = lambda a: pl.BlockSpec(a.shape, lambda i: (0,) * a.ndim)
    return pl.pallas_call(
        functools.partial(_outproj_kernel, alpha=alpha, tiles_per_batch=tiles_per_batch),
        grid=(t // tm,),
        in_specs=[tok(3 * CONV_DIM, P_CONV // (3 * CONV_DIM)), tok(GLA_VAL, P_G // GLA_VAL),
                  tok(2 * GLA_DV, 0), tok(2 * GLA_DV, 0), tok(2 * GLA_DV, 0), tok(2 * GLA_DV, 0),
                  tok(d, 0), pl.BlockSpec((1, 1, d), lambda i: (group_of_tile(i), 0, 0)),
                  full(w_conv), full(g_norm), full(w_out), full(ln_g), full(ln_b)],
        out_specs=pl.BlockSpec((tm, d), lambda i: (i, 0)),
        out_shape=jax.ShapeDtypeStruct((t, d), F32),
        compiler_params=_cparams(("arbitrary",)),
        name="outproj_ln1",
    )(p, p, o_fr, o_br, o_fc, o_bc, h, gate1, w_conv, g_norm, w_out, ln_g, ln_b)


def _route_kernel(h_ref, sh_ref, sc_ref, wr_hi_ref, wr_lo_ref, rb_ref, wsg_ref, wsu_ref, wsd_ref,
                  x3_ref, idx_ref, wt_ref, rank_ref, cnt_ref, shared_ref, base_ref):
    tm = TOK_TILE
    n_exp = rb_ref.shape[1]
    gsz = n_exp // N_GROUPS

    @pl.when(pl.program_id(0) == 0)
    def _():
        base_ref[...] = jnp.zeros_like(base_ref)

    u = h_ref[...] * (1.0 + sc_ref[0]) + sh_ref[0]
    for j in range(ROW_TILES):
        x3_ref[pl.ds(j, tm, stride=ROW_TILES), :] = u[:, j * LANES:(j + 1) * LANES]

    u_hi = u.astype(BF16)
    u_lo = (u - u_hi.astype(F32)).astype(BF16)
    shared_ref[...] = _dot((jax.nn.silu(_dot(u_hi, wsg_ref[...])) * _dot(u_hi, wsu_ref[...])).astype(BF16), wsd_ref[...])

    wr_hi = wr_hi_ref[...]
    logits = _dot(u_hi, wr_hi) + _dot(u_lo, wr_hi) + _dot(u_hi, wr_lo_ref[...])
    s = jax.nn.sigmoid(logits)
    biased = s + rb_ref[...]

    lane = lax.broadcasted_iota(I32, (tm, n_exp), 1)
    grp = lane // gsz
    neg = -jnp.inf
    gscore = []
    for g in range(N_GROUPS):
        vals = jnp.where(grp == g, biased, neg)
        m1 = jnp.max(vals, axis=-1, keepdims=True)
        i1 = jnp.min(jnp.where(vals == m1, lane, n_exp), axis=-1, keepdims=True)
        m2 = jnp.max(jnp.where(lane == i1, neg, vals), axis=-1, keepdims=True)
        gscore.append(m1 + m2)
    keep = jnp.zeros((tm, n_exp), F32)
    for g in range(N_GROUPS):
        ahead = jnp.zeros((tm, 1), F32)
        for g2 in range(N_GROUPS):
            if g2 == g:
                continue
            wins = (gscore[g2] >= gscore[g]) if g2 < g else (gscore[g2] > gscore[g])
            ahead = ahead + jnp.where(wins, 1.0, 0.0)
        keep = jnp.where(grp == g, jnp.where(ahead < TOPK_GROUPS, 1.0, 0.0), keep)
    masked = jnp.where(keep > 0.0, biased, neg)

    onehot = jnp.zeros((tm, n_exp), F32)
    idxs = []
    for _ in range(TOP_K):
        m = jnp.max(masked, axis=-1, keepdims=True)
        i = jnp.min(jnp.where(masked == m, lane, n_exp), axis=-1, keepdims=True)
        hit = lane == i
        onehot = jnp.where(hit, 1.0, onehot)
        masked = jnp.where(hit, neg, masked)
        idxs.append(i)
    sel = onehot * s
    wnorm = sel / jnp.sum(sel, axis=-1, keepdims=True) * ROUTED_SCALE

    row = lax.broadcasted_iota(I32, (tm, tm), 0)
    col = lax.broadcasted_iota(I32, (tm, tm), 1)
    strict_lower = jnp.where(col < row, 1.0, 0.0).astype(BF16)
    rank_all = _dot(strict_lower, onehot.astype(BF16)) + base_ref[...]
    base_new = base_ref[...] + jnp.sum(onehot, axis=0, keepdims=True)
    base_ref[...] = base_new
    cnt_ref[...] = base_new

    out_lane = lax.broadcasted_iota(I32, (tm, LANES), 1)
    idx_o = jnp.zeros((tm, LANES), I32)
    wt_o = jnp.zeros((tm, LANES), F32)
    rank_o = jnp.zeros((tm, LANES), F32)
    for k in range(TOP_K):
        hit = lane == idxs[k]
        wk = jnp.sum(jnp.where(hit, wnorm, 0.0), axis=-1, keepdims=True)
        rk = jnp.sum(jnp.where(hit, rank_all, 0.0), axis=-1, keepdims=True)
        idx_o = jnp.where(out_lane == k, idxs[k], idx_o)
        wt_o = jnp.where(out_lane == k, wk, wt_o)
        rank_o = jnp.where(out_lane == k, rk, rank_o)
    idx_ref[...] = idx_o
    wt_ref[...] = wt_o
    rank_ref[...] = rank_o.astype(I32)


def _route(h1, shift, scale, wr_hi, wr_lo, rbias, wsg, wsu, wsd, group_of_tile):
    t, d = h1.shape
    tm = TOK_TILE
    n_exp = rbias.shape[1]
    grp = lambda i: (group_of_tile(i), 0, 0)
    full = lambda a: pl.BlockSpec(a.shape, lambda i: (0,) * a.ndim)
    lanes_out = lambda dt: jax.ShapeDtypeStruct((t, LANES), dt)
    return pl.pallas_call(
        _route_kernel,
        grid=(t // tm,),
        in_specs=[pl.BlockSpec((tm, d), lambda i: (i, 0)), pl.BlockSpec((1, 1, d), grp), pl.BlockSpec((1, 1, d), grp),
                  full(wr_hi), full(wr_lo), full(rbias), full(wsg), full(wsu), full(wsd)],
        out_specs=[pl.BlockSpec((tm * ROW_TILES, LANES), lambda i: (i, 0)),
                   pl.BlockSpec((tm, LANES), lambda i: (i, 0)),
                   pl.BlockSpec((tm, LANES), lambda i: (i, 0)),
                   pl.BlockSpec((tm, LANES), lambda i: (i, 0)),
                   pl.BlockSpec((1, n_exp), lambda i: (0, 0)),
                   pl.BlockSpec((tm, d), lambda i: (i, 0))],
        out_shape=[jax.ShapeDtypeStruct((t * ROW_TILES, LANES), F32), lanes_out(I32), lanes_out(F32), lanes_out(I32),
                   jax.ShapeDtypeStruct((1, n_exp), F32), jax.ShapeDtypeStruct((t, d), F32)],
        scratch_shapes=[pltpu.VMEM((1, n_exp), F32)],
        compiler_params=_cparams(("arbitrary",)),
        name="route_shared",
    )(h1, shift, scale, wr_hi, wr_lo, rbias, wsg, wsu, wsd)


def _dispatch_kernel(slot_ref, x_hbm, xs_hbm, sem):
    base = pl.program_id(0) * DSP_TILE

    def row_copy(tok, slot):
        return pltpu.make_async_copy(x_hbm.at[tok], xs_hbm.at[slot], sem)

    def body(i, carry):
        for k in range(TOP_K):
            row_copy(base + i, slot_ref[i * TOP_K + k]).start()
        return carry

    lax.fori_loop(0, DSP_TILE, body, 0)

    def drain(i, carry):
        for k in range(TOP_K):
            row_copy(base + i, slot_ref[i * TOP_K + k]).wait()
        return carry

    lax.fori_loop(0, DSP_TILE, drain, 0)


def _dispatch(x3, slots, n_slots):
    t = x3.shape[0]
    return pl.pallas_call(
        _dispatch_kernel,
        grid=(t // DSP_TILE,),
        in_specs=[pl.BlockSpec((DSP_TILE * TOP_K,), lambda i: (i,), memory_space=pltpu.SMEM),
                  pl.BlockSpec(memory_space=pl.ANY)],
        out_specs=pl.BlockSpec(memory_space=pl.ANY),
        out_shape=jax.ShapeDtypeStruct((n_slots, ROW_TILES, LANES), F32),
        scratch_shapes=[pltpu.SemaphoreType.DMA(())],
        compiler_params=_cparams(("arbitrary",)),
        name="dispatch",
    )(slots, x3)


def _expert_kernel(be_ref, nb_ref, xs_ref, wg_ref, wu_ref, wd_ref, ys_ref):
    @pl.when(pl.program_id(0) < nb_ref[0])
    def _():
        x = jnp.concatenate([xs_ref[pl.ds(j, MOE_BLOCK, stride=ROW_TILES), :] for j in range(ROW_TILES)], axis=-1)
        xb = x.astype(BF16)
        hid = jax.nn.silu(_dot(xb, wg_ref[0].astype(BF16))) * _dot(xb, wu_ref[0].astype(BF16))
        y = _dot(hid.astype(BF16), wd_ref[0].astype(BF16))
        for j in range(ROW_TILES):
            ys_ref[pl.ds(j, MOE_BLOCK, stride=ROW_TILES), :] = y[:, j * LANES:(j + 1) * LANES]


def _experts(xs2, block_e, n_used, w_gate, w_up, w_down):
    n_rows = xs2.shape[0]
    n_blocks = n_rows // (MOE_BLOCK * ROW_TILES)
    _, d, hdim = w_gate.shape
    rows = pl.BlockSpec((MOE_BLOCK * ROW_TILES, LANES), lambda i, be, nb: (i, 0))
    return pl.pallas_call(
        _expert_kernel,
        grid_spec=pltpu.PrefetchScalarGridSpec(
            num_scalar_prefetch=2,
            grid=(n_blocks,),
            in_specs=[rows,
                      pl.BlockSpec((1, d, hdim), lambda i, be, nb: (be[i], 0, 0)),
                      pl.BlockSpec((1, d, hdim), lambda i, be, nb: (be[i], 0, 0)),
                      pl.BlockSpec((1, hdim, d), lambda i, be, nb: (be[i], 0, 0))],
            out_specs=rows,
        ),
        out_shape=jax.ShapeDtypeStruct((n_rows, LANES), F32),
        compiler_params=_cparams(("arbitrary",)),
        name="experts",
    )(block_e, n_used, xs2, w_gate, w_up, w_down)


def _combine_kernel(slot_cur, slot_nxt, ys_hbm, wt_ref, shared_ref, h_ref, gt_ref, lng_ref, lnb_ref, o_ref, buf, sem,
                    *, alpha):
    i = pl.program_id(0)
    n = pl.num_programs(0)
    tt = CMB_TILE
    rows_per_tile = tt * TOP_K

    def row_copy(slot_ref, r, b):
        return pltpu.make_async_copy(ys_hbm.at[slot_ref[r]], buf.at[b, pl.ds(r * ROW_TILES, ROW_TILES)], sem.at[b])

    def issue(slot_ref, b):
        def body(r, carry):
            row_copy(slot_ref, r, b).start()
            return carry
        lax.fori_loop(0, rows_per_tile, body, 0)

    @pl.when(i == 0)
    def _():
        issue(slot_cur, 0)

    @pl.when(i + 1 < n)
    def _():
        issue(slot_nxt, (i + 1) % 2)

    cur = i % 2

    def drain(r, carry):
        row_copy(slot_cur, r, cur).wait()
        return carry
    lax.fori_loop(0, rows_per_tile, drain, 0)

    wt = wt_ref[...]
    tile = buf.at[cur]
    cols = []
    for j in range(ROW_TILES):
        acc = jnp.zeros((tt, LANES), F32)
        for k in range(TOP_K):
            acc = acc + wt[:, k:k + 1] * tile[pl.ds(k * ROW_TILES + j, tt, stride=TOP_K * ROW_TILES), :]
        cols.append(acc)
    f = shared_ref[...] + jnp.concatenate(cols, axis=-1)
    v = alpha * h_ref[...] + gt_ref[0] * f
    o_ref[...] = _layer_norm(v, lng_ref[...], lnb_ref[...])


def _combine(ys3, slots, wts, shared, h1, gate2, ln_g, ln_b, group_of_tile, alpha):
    t, d = h1.shape
    tt = CMB_TILE
    n = t // tt
    scale = TOK_TILE // tt
    full = lambda a: pl.BlockSpec(a.shape, lambda i: (0,) * a.ndim)
    return pl.pallas_call(
        functools.partial(_combine_kernel, alpha=alpha),
        grid=(n,),
        in_specs=[pl.BlockSpec((tt * TOP_K,), lambda i: (i,), memory_space=pltpu.SMEM),
                  pl.BlockSpec((tt * TOP_K,), lambda i: (jnp.minimum(i + 1, n - 1),), memory_space=pltpu.SMEM),
                  pl.BlockSpec(memory_space=pl.ANY),
                  pl.BlockSpec((tt, LANES), lambda i: (i, 0)),
                  pl.BlockSpec((tt, d), lambda i: (i, 0)),
                  pl.BlockSpec((tt, d), lambda i: (i, 0)),
                  pl.BlockSpec((1, 1, d), lambda i: (group_of_tile(i // scale), 0, 0)),
                  full(ln_g), full(ln_b)],
        out_specs=pl.BlockSpec((tt, d), lambda i: (i, 0)),
        out_shape=jax.ShapeDtypeStruct((t, d), F32),
        scratch_shapes=[pltpu.VMEM((2, tt * TOP_K * ROW_TILES, LANES), F32), pltpu.SemaphoreType.DMA((2,))],
        compiler_params=_cparams(("arbitrary",)),
        name="combine_ln2",
    )(slots, slots, ys3, wts, shared, h1, gate2, ln_g, ln_b)


def _reorder_w_in(w):
    d = w.shape[0]
    o = 0
    k = w[:, o:o + GLA_KEY]; o += GLA_KEY
    v = w[:, o:o + GLA_VAL]; o += GLA_VAL
    dec = w[:, o:o + 2 * GLA_LOWRANK]; o += 2 * GLA_LOWRANK
    q = w[:, o:o + GLA_KEY]; o += GLA_KEY
    g = w[:, o:o + GLA_VAL]; o += GLA_VAL
    conv = w[:, o:o + 3 * CONV_DIM]
    pad = jnp.zeros((d, LANES - 2 * GLA_LOWRANK), w.dtype)
    return jnp.concatenate([v, g, k, q, conv, dec, pad], axis=1).astype(BF16)


def _decay_weights(w_up, b_dec, pair):
    lo = pair * 2 * GLA_DK
    outs = []
    for d in range(2):
        w = jnp.zeros((LANES, LANES), F32).at[d * GLA_LOWRANK:(d + 1) * GLA_LOWRANK, :].set(w_up[d][:, lo:lo + LANES])
        outs.append((w, b_dec[d][lo:lo + LANES].reshape(1, LANES)))
    return outs[0][0], outs[1][0], outs[0][1], outs[1][1]


def _to_scan_order(a, ctx_len, rows):
    b, _, f = a.shape
    lat = a[:, ctx_len:].reshape(b, rows, GRID_W, f).swapaxes(1, 2).reshape(b, rows * GRID_W, f)
    return jnp.concatenate([a[:, :ctx_len], lat], axis=1)


def _from_scan_order(a, ctx_len, rows):
    b, _, f = a.shape
    lat = a[:, ctx_len:].reshape(b, GRID_W, rows, f).swapaxes(1, 2).reshape(b, rows * GRID_W, f)
    return jnp.concatenate([a[:, :ctx_len], lat], axis=1)


def kernel(x, c, ctx, c_ctx, w_mod, b_mod, w_in, w_conv, w_decay_up, b_decay, g_gla_norm, w_out, ln1_g, ln1_b, w_router, router_bias, w_e_gate, w_e_up, w_e_down, w_s_gate, w_s_up, w_s_down, ln2_g, ln2_b):
    bsz, seq, d = x.shape
    ctx_len = ctx.shape[1]
    depth = w_mod.shape[0]
    n_exp = w_router.shape[2]
    rows = seq // GRID_W
    l = ctx_len + seq
    t = bsz * l
    assert ctx_len == TOK_TILE and seq % TOK_TILE == 0 and d == ROW_TILES * LANES
    assert t % DSP_TILE == 0 and bsz + 1 <= SUBLANES
    tiles_per_batch = l // TOK_TILE
    alpha = float((2 * depth) ** 0.25)

    def group_of_tile(i):
        return jnp.where(i % tiles_per_batch == 0, bsz, i // tiles_per_batch)

    cond8 = jnp.zeros((SUBLANES, d), F32).at[:bsz].set(c).at[bsz].set(c_ctx)
    mod = _modulation(cond8, w_mod, b_mod).reshape(depth, SUBLANES, 6, 1, d)

    nk = t * TOP_K
    n_blocks = -(-(nk + n_exp * (MOE_BLOCK - 1)) // MOE_BLOCK)
    n_slots = n_blocks * MOE_BLOCK

    h = jnp.concatenate([ctx, x], axis=1).reshape(t, d)
    for i in range(depth):
        m = [mod[i, :, j] for j in range(6)]
        p = _inproj(h, m[0], m[1], _reorder_w_in(w_in[i]), group_of_tile)
        p3 = p.reshape(bsz, l, P_COLS)

        wf, wb, bf, bb = _decay_weights(w_decay_up[i], b_decay[i], 0)
        o_fr, o_br = _gla_pair(p3, P_K // LANES, P_Q // LANES, P_V // (2 * GLA_DV), P_DEC // LANES, wf, wb, bf, bb)
        col_in = jnp.concatenate([p3[..., P_V + 2 * GLA_DV:P_V + 4 * GLA_DV], p3[..., P_K + LANES:P_K + 2 * LANES],
                                  p3[..., P_Q + LANES:P_Q + 2 * LANES], p3[..., P_DEC:P_DEC + LANES]], axis=-1)
        col_in = _to_scan_order(col_in, ctx_len, rows)
        wf, wb, bf, bb = _decay_weights(w_decay_up[i], b_decay[i], 1)
        o_fc, o_bc = _gla_pair(col_in, 2, 3, 0, 4, wf, wb, bf, bb)
        o_fc = _from_scan_order(o_fc, ctx_len, rows)
        o_bc = _from_scan_order(o_bc, ctx_len, rows)

        flat = lambda a: a.reshape(t, a.shape[-1])
        h1 = _outproj(p, flat(o_fr), flat(o_br), flat(o_fc), flat(o_bc), h, m[2], w_conv[i],
                      g_gla_norm[i].reshape(1, GLA_DV), w_out[i].astype(BF16), ln1_g[i].reshape(1, d), ln1_b[i].reshape(1, d),
                      group_of_tile, alpha, tiles_per_batch)

        wr = w_router[i]
        wr_hi = wr.astype(BF16)
        wr_lo = (wr - wr_hi.astype(F32)).astype(BF16)
        x3, idx, wts, rank, counts, shared = _route(
            h1, m[3], m[4], wr_hi, wr_lo, router_bias[i].reshape(1, n_exp),
            w_s_gate[i].astype(BF16), w_s_up[i].astype(BF16), w_s_down[i].astype(BF16), group_of_tile)

        cnt = counts[0].astype(I32)
        padded = (cnt + MOE_BLOCK - 1) // MOE_BLOCK * MOE_BLOCK
        pend = jnp.cumsum(padded)
        pstart = pend - padded
        slots = (pstart[idx[:, :TOP_K]] + rank[:, :TOP_K]).reshape(nk)
        block_start = jnp.arange(n_blocks, dtype=I32) * MOE_BLOCK
        block_e = jnp.minimum(jnp.searchsorted(pend, block_start, side="right"), n_exp - 1).astype(I32)
        n_used = (pend[-1] // MOE_BLOCK).astype(I32).reshape(1)

        xs3 = _dispatch(x3.reshape(t, ROW_TILES, LANES), slots, n_slots)
        ys2 = _experts(xs3.reshape(n_slots * ROW_TILES, LANES), block_e, n_used, w_e_gate[i], w_e_up[i], w_e_down[i])
        h = _combine(ys2.reshape(n_slots, ROW_TILES, LANES), slots, wts, shared, h1, m[5],
                     ln2_g[i].reshape(1, d), ln2_b[i].reshape(1, d), group_of_tile, alpha)

    return h.reshape(bsz, l, d)[:, ctx_len:]
```

```python
import functools

import jax
import jax.numpy as jnp
from jax import lax
from jax.experimental import pallas as pl
from jax.experimental.pallas import tpu as pltpu

F32 = jnp.float32
BF16 = jnp.bfloat16
I32 = jnp.int32

GRID_W = 64
CONV_DIM = 512
GLA_HEADS = 4
GLA_DK = 64
GLA_DV = 128
GLA_KEY = GLA_HEADS * GLA_DK
GLA_VAL = GLA_HEADS * GLA_DV
GLA_LOWRANK = 16
GLA_GATE_TEMP = 16.0
GLA_CHUNK = 64
TOP_K = 8
N_GROUPS = 8
TOPK_GROUPS = 4
ROUTED_SCALE = 2.5
LN_EPS = 1e-5
RMS_EPS = 1e-6

LANES = 128
SUBLANES = 8
VMEM_LIMIT = 56 * 1024 * 1024

P_V = 0
P_G = 512
P_K = 1024
P_Q = 1280
P_CONV = 1536
P_DEC = 3072
P_COLS = 3200

TOK_TILE = 256
MOE_BLOCK = 128
CMB_TILE = 128
DSP_TILE = 512
ROW_TILES = 8


def _cparams(sem):
    return pltpu.CompilerParams(dimension_semantics=sem, vmem_limit_bytes=VMEM_LIMIT)


def _mod_kernel(c_ref, w_ref, b_ref, o_ref):
    c = c_ref[...]
    a = c * jax.nn.sigmoid(c)
    o_ref[0] = jnp.dot(a, w_ref[0], preferred_element_type=F32, precision=lax.Precision.HIGHEST) + b_ref[0]


def _modulation(cond8, w_mod, b_mod):
    depth, d, n = w_mod.shape
    tn = 1536
    return pl.pallas_call(
        _mod_kernel,
        grid=(depth, n // tn),
        in_specs=[
            pl.BlockSpec((SUBLANES, d), lambda i, j: (0, 0)),
            pl.BlockSpec((1, d, tn), lambda i, j: (i, 0, j)),
            pl.BlockSpec((1, 1, tn), lambda i, j: (i, 0, j)),
        ],
        out_specs=pl.BlockSpec((1, SUBLANES, tn), lambda i, j: (i, 0, j)),
        out_shape=jax.ShapeDtypeStruct((depth, SUBLANES, n), F32),
        compiler_params=_cparams(("arbitrary", "arbitrary")),
        name="adaln_mod",
    )(cond8, w_mod, b_mod.reshape(depth, 1, n))


def _inproj_kernel(x_ref, sh_ref, sc_ref, w_ref, o_ref):
    h = x_ref[...] * (1.0 + sc_ref[0]) + sh_ref[0]
    o_ref[...] = jnp.dot(h.astype(BF16), w_ref[...], preferred_element_type=F32)


def _inproj(h, shift, scale, w, group_of_tile):
    t, d = h.shape
    n = w.shape[1]
    grp = lambda i: (group_of_tile(i), 0, 0)
    return pl.pallas_call(
        _inproj_kernel,
        grid=(t // TOK_TILE,),
        in_specs=[
            pl.BlockSpec((TOK_TILE, d), lambda i: (i, 0)),
            pl.BlockSpec((1, 1, d), grp),
            pl.BlockSpec((1, 1, d), grp),
            pl.BlockSpec((d, n), lambda i: (0, 0)),
        ],
        out_specs=pl.BlockSpec((TOK_TILE, n), lambda i: (i, 0)),
        out_shape=jax.ShapeDtypeStruct((t, n), F32),
        compiler_params=_cparams(("arbitrary",)),
        name="inproj",
    )(h, shift, scale, w)


def _split3(x):
    a = x.astype(BF16)
    r = x - a.astype(F32)
    b = r.astype(BF16)
    c = (r - b.astype(F32)).astype(BF16)
    return a, b, c


def _dot(a, b):
    return jnp.dot(a, b, preferred_element_type=F32)


def _dot_nt(a, b):
    return lax.dot_general(a, b, (((1,), (1,)), ((), ())), preferred_element_type=F32)


def _gla_direction(k2, q2, v2, dec, wup, bdec, s_ref, o_ref, reverse):
    t = TOK_TILE
    c = GLA_CHUNK
    z = jnp.dot(dec, wup, preferred_element_type=F32, precision=lax.Precision.HIGHEST) + bdec
    la = (jnp.minimum(z, 0.0) - jnp.log(1.0 + jnp.exp(-jnp.abs(z)))) * (1.0 / GLA_GATE_TEMP)

    row = lax.broadcasted_iota(I32, (t, t), 0)
    col = lax.broadcasted_iota(I32, (t, t), 1)
    same = (row // c) == (col // c)
    tri = jnp.logical_and(same, (col >= row) if reverse else (col <= row))
    tri_b = jnp.where(tri, 1.0, 0.0).astype(BF16)
    same_b = jnp.where(same, 1.0, 0.0).astype(BF16)
    la3 = _split3(la)
    bcum = _dot(tri_b, la3[0]) + _dot(tri_b, la3[1]) + _dot(tri_b, la3[2])
    btot = _dot(same_b, la3[0]) + _dot(same_b, la3[1]) + _dot(same_b, la3[2])

    q_dec = q2 * jnp.exp(bcum) * (GLA_DK ** -0.5)
    k_inv = (k2 * jnp.exp(-bcum)).astype(BF16)
    k_tail = k2 * jnp.exp(btot - bcum)
    chunk_decay = jnp.exp(btot)

    lane = lax.broadcasted_iota(I32, (t, 2 * GLA_DK), 1)
    order = range(t // c - 1, -1, -1) if reverse else range(t // c)
    for hh in range(2):
        in_head = (lane // GLA_DK) == hh
        qd = jnp.where(in_head, q_dec, 0.0).astype(BF16)
        kt = jnp.where(in_head, k_tail, 0.0).astype(BF16)
        vh = v2[:, hh * GLA_DV:(hh + 1) * GLA_DV]
        vh_b = vh.astype(BF16)
        att = jnp.where(tri, _dot_nt(qd, k_inv), 0.0).astype(BF16)
        o_intra = _dot(att, vh_b)
        s = s_ref[hh]
        for ci in order:
            rows = slice(ci * c, (ci + 1) * c)
            o_ref[0, rows, hh * GLA_DV:(hh + 1) * GLA_DV] = o_intra[rows] + _dot_nt(qd[rows], s.astype(BF16))
            kv_t = _dot(vh[rows].T.astype(BF16), kt[rows])
            s = s * chunk_decay[ci * c:ci * c + 1, :] + kv_t
        s_ref[hh] = s


def _gla_kernel(kf, qf, vf, df, kb, qb, vb, db, wupf, wupb, bf, bb, of_ref, ob_ref, sf_ref, sb_ref):
    @pl.when(pl.program_id(1) == 0)
    def _():
        sf_ref[...] = jnp.zeros_like(sf_ref)
        sb_ref[...] = jnp.zeros_like(sb_ref)

    _gla_direction(kf[0], qf[0], vf[0], df[0], wupf[...], bf[...], sf_ref, of_ref, False)
    _gla_direction(kb[0], qb[0], vb[0], db[0], wupb[...], bb[...], sb_ref, ob_ref, True)


def _gla_pair(src, col_k, col_q, col_v, col_dec, wupf, wupb, bf, bb):
    bsz, l, _ = src.shape
    nb = l // TOK_TILE
    fwd = lambda j: j
    bwd = lambda j: jnp.where(j == 0, 0, nb - j)

    def spec(width, colidx, order):
        return pl.BlockSpec((1, TOK_TILE, width), lambda b, j: (b, order(j), colidx))

    wspec = pl.BlockSpec((LANES, LANES), lambda b, j: (0, 0))
    bspec = pl.BlockSpec((1, LANES), lambda b, j: (0, 0))
    out = jax.ShapeDtypeStruct((bsz, l, 2 * GLA_DV), F32)
    return pl.pallas_call(
        _gla_kernel,
        grid=(bsz, nb),
        in_specs=[spec(LANES, col_k, fwd), spec(LANES, col_q, fwd), spec(2 * GLA_DV, col_v, fwd), spec(LANES, col_dec, fwd),
                  spec(LANES, col_k, bwd), spec(LANES, col_q, bwd), spec(2 * GLA_DV, col_v, bwd), spec(LANES, col_dec, bwd),
                  wspec, wspec, bspec, bspec],
        out_specs=[pl.BlockSpec((1, TOK_TILE, 2 * GLA_DV), lambda b, j: (b, fwd(j), 0)),
                   pl.BlockSpec((1, TOK_TILE, 2 * GLA_DV), lambda b, j: (b, bwd(j), 0))],
        out_shape=[out, out],
        scratch_shapes=[pltpu.VMEM((2, GLA_DV, LANES), F32), pltpu.VMEM((2, GLA_DV, LANES), F32)],
        compiler_params=_cparams(("arbitrary", "arbitrary")),
        name="gla_pair",
    )(src, src, src, src, src, src, src, src, wupf, wupb, bf, bb)


def _layer_norm(v, g, b):
    mu = jnp.mean(v, axis=-1, keepdims=True)
    dv = v - mu
    var = jnp.mean(dv * dv, axis=-1, keepdims=True)
    return dv * lax.rsqrt(var + LN_EPS) * g + b


def _outproj_kernel(conv_ref, gate_ref, ofr, obr, ofc, obc, h_ref, gt_ref, wconv_ref, gn_ref, wout_ref,
                    lng_ref, lnb_ref, o_ref, *, alpha, tiles_per_batch):
    tm = TOK_TILE
    is_ctx = (pl.program_id(0) % tiles_per_batch) == 0
    seg = jnp.where(is_ctx, tm, GRID_W)

    cv = conv_ref[...]
    b_gate = cv[:, :CONV_DIM]
    u = cv[:, CONV_DIM:2 * CONV_DIM] * cv[:, 2 * CONV_DIM:]
    pos = lax.broadcasted_iota(I32, (tm, CONV_DIM), 0) & (seg - 1)
    prev = jnp.where(pos == 0, 0.0, pltpu.roll(u, 1, 0))
    nxt = jnp.where(pos == seg - 1, 0.0, pltpu.roll(u, tm - 1, 0))
    wc = wconv_ref[...]
    mix_conv = b_gate * (wc[0:1, :] * prev + wc[1:2, :] * u + wc[2:3, :] * nxt)

    o_r = ofr[...] + obr[...]
    o_c = ofc[...] + obc[...]
    gate = gate_ref[...]
    gn = gn_ref[...]
    parts = [mix_conv.astype(BF16)]
    for hd in range(GLA_HEADS):
        src = o_r if hd < 2 else o_c
        o = src[:, (hd % 2) * GLA_DV:(hd % 2 + 1) * GLA_DV]
        o = o * lax.rsqrt(jnp.mean(o * o, axis=-1, keepdims=True) + RMS_EPS) * gn
        g = gate[:, hd * GLA_DV:(hd + 1) * GLA_DV]
        parts.append((o * (g * jax.nn.sigmoid(g))).astype(BF16))
    mix = jnp.concatenate(parts, axis=-1)
    y = _dot(mix, wout_ref[...])
    v = alpha * h_ref[...] + gt_ref[0] * y
    o_ref[...] = _layer_norm(v, lng_ref[...], lnb_ref[...])


def _outproj(p, o_fr, o_br, o_fc, o_bc, h, gate1, w_conv, g_norm, w_out, ln_g, ln_b, group_of_tile, alpha, tiles_per_batch):
    t, d = h.shape
    tm = TOK_TILE
    tok = lambda width, colidx: pl.BlockSpec((tm, width), lambda i: (i, colidx))
    full = lambda a: pl.BlockSpec(a.shape, lambda i: (0,) * a.ndim)
    return pl.pallas_call(
        functools.partial(_outproj_kernel, alpha=alpha, tiles_per_batch=tiles_per_batch),
        grid=(t // tm,),
        in_specs=[tok(3 * CONV_DIM, P_CONV // (3 * CONV_DIM)), tok(GLA_VAL, P_G // GLA_VAL),
                  tok(2 * GLA_DV, 0), tok(2 * GLA_DV, 0), tok(2 * GLA_DV, 0), tok(2 * GLA_DV, 0),
                  tok(d, 0), pl.BlockSpec((1, 1, d), lambda i: (group_of_tile(i), 0, 0)),
                  full(w_conv), full(g_norm), full(w_out), full(ln_g), full(ln_b)],
        out_specs=pl.BlockSpec((tm, d), lambda i: (i, 0)),
        out_shape=jax.ShapeDtypeStruct((t, d), F32),
        compiler_params=_cparams(("arbitrary",)),
        name="outproj_ln1",
    )(p, p, o_fr, o_br, o_fc, o_bc, h, gate1, w_conv, g_norm, w_out, ln_g, ln_b)


def _route_kernel(h_ref, sh_ref, sc_ref, wr_hi_ref, wr_lo_ref, rb_ref, wsg_ref, wsu_ref, wsd_ref,
                  x3_ref, idx_ref, wt_ref, rank_ref, cnt_ref, shared_ref, base_ref):
    tm = TOK_TILE
    n_exp = rb_ref.shape[1]
    gsz = n_exp // N_GROUPS

    @pl.when(pl.program_id(0) == 0)
    def _():
        base_ref[...] = jnp.zeros_like(base_ref)

    u = h_ref[...] * (1.0 + sc_ref[0]) + sh_ref[0]
    for j in range(ROW_TILES):
        x3_ref[pl.ds(j, tm, stride=ROW_TILES), :] = u[:, j * LANES:(j + 1) * LANES]

    u_hi = u.astype(BF16)
    u_lo = (u - u_hi.astype(F32)).astype(BF16)
    shared_ref[...] = _dot((jax.nn.silu(_dot(u_hi, wsg_ref[...])) * _dot(u_hi, wsu_ref[...])).astype(BF16), wsd_ref[...])

    wr_hi = wr_hi_ref[...]
    logits = _dot(u_hi, wr_hi) + _dot(u_lo, wr_hi) + _dot(u_hi, wr_lo_ref[...])
    s = jax.nn.sigmoid(logits)
    biased = s + rb_ref[...]

    lane = lax.broadcasted_iota(I32, (tm, n_exp), 1)
    grp = lane // gsz
    neg = -jnp.inf
    gscore = []
    for g in range(N_GROUPS):
        vals = jnp.where(grp == g, biased, neg)
        m1 = jnp.max(vals, axis=-1, keepdims=True)
        i1 = jnp.min(jnp.where(vals == m1, lane, n_exp), axis=-1, keepdims=True)
        m2 = jnp.max(jnp.where(lane == i1, neg, vals), axis=-1, keepdims=True)
        gscore.append(m1 + m2)
    keep = jnp.zeros((tm, n_exp), F32)
    for g in range(N_GROUPS):
        ahead = jnp.zeros((tm, 1), F32)
        for g2 in range(N_GROUPS):
            if g2 == g:
                continue
            wins = (gscore[g2] >= gscore[g]) if g2 < g else (gscore[g2] > gscore[g])
            ahead = ahead + jnp.where(wins, 1.0, 0.0)
        keep = jnp.where(grp == g, jnp.where(ahead < TOPK_GROUPS, 1.0, 0.0), keep)
    masked = jnp.where(keep > 0.0, biased, neg)

    onehot = jnp.zeros((tm, n_exp), F32)
    idxs = []
    for _ in range(TOP_K):
        m = jnp.max(masked, axis=-1, keepdims=True)
        i = jnp.min(jnp.where(masked == m, lane, n_exp), axis=-1, keepdims=True)
        hit = lane == i
        onehot = jnp.where(hit, 1.0, onehot)
        masked = jnp.where(hit, neg, masked)
        idxs.append(i)
    sel = onehot * s
    wnorm = sel / jnp.sum(sel, axis=-1, keepdims=True) * ROUTED_SCALE

    row = lax.broadcasted_iota(I32, (tm, tm), 0)
    col = lax.broadcasted_iota(I32, (tm, tm), 1)
    strict_lower = jnp.where(col < row, 1.0, 0.0).astype(BF16)
    rank_all = _dot(strict_lower, onehot.astype(BF16)) + base_ref[...]
    base_new = base_ref[...] + jnp.sum(onehot, axis=0, keepdims=True)
    base_ref[...] = base_new
    cnt_ref[...] = base_new

    out_lane = lax.broadcasted_iota(I32, (tm, LANES), 1)
    idx_o = jnp.zeros((tm, LANES), I32)
    wt_o = jnp.zeros((tm, LANES), F32)
    rank_o = jnp.zeros((tm, LANES), F32)
    for k in range(TOP_K):
        hit = lane == idxs[k]
        wk = jnp.sum(jnp.where(hit, wnorm, 0.0), axis=-1, keepdims=True)
        rk = jnp.sum(jnp.where(hit, rank_all, 0.0), axis=-1, keepdims=True)
        idx_o = jnp.where(out_lane == k, idxs[k], idx_o)
        wt_o = jnp.where(out_lane == k, wk, wt_o)
        rank_o = jnp.where(out_lane == k, rk, rank_o)
    idx_ref[...] = idx_o
    wt_ref[...] = wt_o
    rank_ref[...] = rank_o.astype(I32)


def _route(h1, shift, scale, wr_hi, wr_lo, rbias, wsg, wsu, wsd, group_of_tile):
    t, d = h1.shape
    tm = TOK_TILE
    n_exp = rbias.shape[1]
    grp = lambda i: (group_of_tile(i), 0, 0)
    full = lambda a: pl.BlockSpec(a.shape, lambda i: (0,) * a.ndim)
    lanes_out = lambda dt: jax.ShapeDtypeStruct((t, LANES), dt)
    return pl.pallas_call(
        _route_kernel,
        grid=(t // tm,),
        in_specs=[pl.BlockSpec((tm, d), lambda i: (i, 0)), pl.BlockSpec((1, 1, d), grp), pl.BlockSpec((1, 1, d), grp),
                  full(wr_hi), full(wr_lo), full(rbias), full(wsg), full(wsu), full(wsd)],
        out_specs=[pl.BlockSpec((tm * ROW_TILES, LANES), lambda i: (i, 0)),
                   pl.BlockSpec((tm, LANES), lambda i: (i, 0)),
                   pl.BlockSpec((tm, LANES), lambda i: (i, 0)),
                   pl.BlockSpec((tm, LANES), lambda i: (i, 0)),
                   pl.BlockSpec((1, n_exp), lambda i: (0, 0)),
                   pl.BlockSpec((tm, d), lambda i: (i, 0))],
        out_shape=[jax.ShapeDtypeStruct((t * ROW_TILES, LANES), F32), lanes_out(I32), lanes_out(F32), lanes_out(I32),
                   jax.ShapeDtypeStruct((1, n_exp), F32), jax.ShapeDtypeStruct((t, d), F32)],
        scratch_shapes=[pltpu.VMEM((1, n_exp), F32)],
        compiler_params=_cparams(("arbitrary",)),
        name="route_shared",
    )(h1, shift, scale, wr_hi, wr_lo, rbias, wsg, wsu, wsd)


def _slots_kernel(idx_ref, rank_ref, pstart_ref, o_ref):
    tm = TOK_TILE
    n_exp = pstart_ref.shape[1]
    idx = idx_ref[...]
    lane = lax.broadcasted_iota(I32, (tm, n_exp), 1)
    out_lane = lax.broadcasted_iota(I32, (tm, LANES), 1)
    start = jnp.zeros((tm, LANES), F32)
    for k in range(TOP_K):
        hit = lane == idx[:, k:k + 1]
        sk = jnp.sum(jnp.where(hit, pstart_ref[...], 0.0), axis=-1, keepdims=True)
        start = jnp.where(out_lane == k, sk, start)
    o_ref[...] = start.astype(I32) + rank_ref[...]


def _slots(idx, rank, pstart_f):
    t = idx.shape[0]
    tok = pl.BlockSpec((TOK_TILE, LANES), lambda i: (i, 0))
    return pl.pallas_call(
        _slots_kernel,
        grid=(t // TOK_TILE,),
        in_specs=[tok, tok, pl.BlockSpec(pstart_f.shape, lambda i: (0, 0))],
        out_specs=tok,
        out_shape=jax.ShapeDtypeStruct((t, LANES), I32),
        compiler_params=_cparams(("arbitrary",)),
        name="slots",
    )(idx, rank, pstart_f)


def _dispatch_kernel(slot_ref, x_ref, xs_hbm, sem):
    def body(i, carry):
        src = x_ref.at[pl.ds(pl.multiple_of(i * ROW_TILES, ROW_TILES), ROW_TILES)]
        for k in range(TOP_K):
            pltpu.make_async_copy(src, xs_hbm.at[slot_ref[i * TOP_K + k]], sem).start()
        return carry

    lax.fori_loop(0, DSP_TILE, body, 0)
    done = xs_hbm.at[pl.ds(0, DSP_TILE * TOP_K)]
    pltpu.make_async_copy(done, done, sem).wait()


def _dispatch(x2, slots, n_slots):
    t = x2.shape[0] // ROW_TILES
    return pl.pallas_call(
        _dispatch_kernel,
        grid=(t // DSP_TILE,),
        in_specs=[pl.BlockSpec((DSP_TILE * TOP_K,), lambda i: (i,), memory_space=pltpu.SMEM),
                  pl.BlockSpec((DSP_TILE * ROW_TILES, LANES), lambda i: (i, 0))],
        out_specs=pl.BlockSpec(memory_space=pl.ANY),
        out_shape=jax.ShapeDtypeStruct((n_slots, ROW_TILES, LANES), F32),
        scratch_shapes=[pltpu.SemaphoreType.DMA(())],
        compiler_params=_cparams(("arbitrary",)),
        name="dispatch",
    )(slots, x2)


def _expert_kernel(pstart_ref, nblk_ref, xs_hbm, wg_ref, wu_ref, wd_ref, ys_hbm,
                   xbuf, ybuf, wg_b, wu_b, wd_b, sem_in, sem_out):
    e = pl.program_id(0)
    nb = nblk_ref[e]
    base = pstart_ref[e]
    blk_rows = MOE_BLOCK * ROW_TILES

    def rows_of(b):
        return pl.ds(pl.multiple_of((base + b * MOE_BLOCK) * ROW_TILES, blk_rows), blk_rows)

    def in_copy(b, s):
        return pltpu.make_async_copy(xs_hbm.at[rows_of(b)], xbuf.at[s], sem_in.at[s])

    def out_copy(b, s):
        return pltpu.make_async_copy(ybuf.at[s], ys_hbm.at[rows_of(b)], sem_out.at[s])

    @pl.when(nb > 0)
    def _():
        in_copy(0, 0).start()
        wg_b[...] = wg_ref[0].astype(BF16)
        wu_b[...] = wu_ref[0].astype(BF16)
        wd_b[...] = wd_ref[0].astype(BF16)

    def body(b, carry):
        s = b % 2

        @pl.when(b + 1 < nb)
        def _():
            in_copy(b + 1, 1 - s).start()

        in_copy(b, s).wait()

        @pl.when(b >= 2)
        def _():
            out_copy(b - 2, s).wait()

        x = jnp.concatenate([xbuf[s, pl.ds(j, MOE_BLOCK, stride=ROW_TILES), :] for j in range(ROW_TILES)], axis=-1)
        xb = x.astype(BF16)
        hid = jax.nn.silu(_dot(xb, wg_b[...])) * _dot(xb, wu_b[...])
        y = _dot(hid.astype(BF16), wd_b[...])
        for j in range(ROW_TILES):
            ybuf[s, pl.ds(j, MOE_BLOCK, stride=ROW_TILES), :] = y[:, j * LANES:(j + 1) * LANES]
        out_copy(b, s).start()
        return carry

    lax.fori_loop(0, nb, body, 0)

    @pl.when(nb >= 2)
    def _():
        out_copy(nb - 2, nb % 2).wait()

    @pl.when(nb >= 1)
    def _():
        out_copy(nb - 1, (nb - 1) % 2).wait()


def _experts(xs2, pstart, nblk, w_gate, w_up, w_down):
    n_rows = xs2.shape[0]
    n_exp, d, hdim = w_gate.shape
    blk_rows = MOE_BLOCK * ROW_TILES
    return pl.pallas_call(
        _expert_kernel,
        grid_spec=pltpu.PrefetchScalarGridSpec(
            num_scalar_prefetch=2,
            grid=(n_exp,),
            in_specs=[pl.BlockSpec(memory_space=pl.ANY),
                      pl.BlockSpec((1, d, hdim), lambda e, ps, nb: (e, 0, 0)),
                      pl.BlockSpec((1, d, hdim), lambda e, ps, nb: (e, 0, 0)),
                      pl.BlockSpec((1, hdim, d), lambda e, ps, nb: (e, 0, 0))],
            out_specs=pl.BlockSpec(memory_space=pl.ANY),
            scratch_shapes=[pltpu.VMEM((2, blk_rows, LANES), F32), pltpu.VMEM((2, blk_rows, LANES), F32),
                            pltpu.VMEM((d, hdim), BF16), pltpu.VMEM((d, hdim), BF16), pltpu.VMEM((hdim, d), BF16),
                            pltpu.SemaphoreType.DMA((2,)), pltpu.SemaphoreType.DMA((2,))],
        ),
        out_shape=jax.ShapeDtypeStruct((n_rows, LANES), F32),
        compiler_params=_cparams(("arbitrary",)),
        name="experts",
    )(pstart, nblk, xs2, w_gate, w_up, w_down)


def _combine_kernel(slot_cur, slot_nxt, ys_hbm, wt_ref, shared_ref, h_ref, gt_ref, lng_ref, lnb_ref, o_ref, buf, sem,
                    *, alpha):
    i = pl.program_id(0)
    n = pl.num_programs(0)
    tt = CMB_TILE

    def issue(slot_ref, b):
        def body(tok, carry):
            for k in range(TOP_K):
                r = tok * TOP_K + k
                dst = buf.at[b, pl.ds(pl.multiple_of(r * ROW_TILES, ROW_TILES), ROW_TILES)]
                pltpu.make_async_copy(ys_hbm.at[slot_ref[r]], dst, sem.at[b]).start()
            return carry
        lax.fori_loop(0, tt, body, 0)

    @pl.when(i == 0)
    def _():
        issue(slot_cur, 0)

    @pl.when(i + 1 < n)
    def _():
        issue(slot_nxt, (i + 1) % 2)

    cur = i % 2
    pltpu.make_async_copy(buf.at[cur], buf.at[cur], sem.at[cur]).wait()

    wt = wt_ref[...]
    cols = []
    for j in range(ROW_TILES):
        acc = jnp.zeros((tt, LANES), F32)
        for k in range(TOP_K):
            acc = acc + wt[:, k:k + 1] * buf[cur, pl.ds(k * ROW_TILES + j, tt, stride=TOP_K * ROW_TILES), :]
        cols.append(acc)
    f = shared_ref[...] + jnp.concatenate(cols, axis=-1)
    v = alpha * h_ref[...] + gt_ref[0] * f
    o_ref[...] = _layer_norm(v, lng_ref[...], lnb_ref[...])


def _combine(ys3, slots, wts, shared, h1, gate2, ln_g, ln_b, group_of_tile, alpha):
    t, d = h1.shape
    tt = CMB_TILE
    n = t // tt
    scale = TOK_TILE // tt
    full = lambda a: pl.BlockSpec(a.shape, lambda i: (0,) * a.ndim)
    return pl.pallas_call(
        functools.partial(_combine_kernel, alpha=alpha),
        grid=(n,),
        in_specs=[pl.BlockSpec((tt * TOP_K,), lambda i: (i,), memory_space=pltpu.SMEM),
                  pl.BlockSpec((tt * TOP_K,), lambda i: (jnp.minimum(i + 1, n - 1),), memory_space=pltpu.SMEM),
                  pl.BlockSpec(memory_space=pl.ANY),
                  pl.BlockSpec((tt, LANES), lambda i: (i, 0)),
                  pl.BlockSpec((tt, d), lambda i: (i, 0)),
                  pl.BlockSpec((tt, d), lambda i: (i, 0)),
                  pl.BlockSpec((1, 1, d), lambda i: (group_of_tile(i // scale), 0, 0)),
                  full(ln_g), full(ln_b)],
        out_specs=pl.BlockSpec((tt, d), lambda i: (i, 0)),
        out_shape=jax.ShapeDtypeStruct((t, d), F32),
        scratch_shapes=[pltpu.VMEM((2, tt * TOP_K * ROW_TILES, LANES), F32), pltpu.SemaphoreType.DMA((2,))],
        compiler_params=_cparams(("arbitrary",)),
        name="combine_ln2",
    )(slots, slots, ys3, wts, shared, h1, gate2, ln_g, ln_b)


def _reorder_w_in(w):
    d = w.shape[0]
    o = 0
    k = w[:, o:o + GLA_KEY]; o += GLA_KEY
    v = w[:, o:o + GLA_VAL]; o += GLA_VAL
    dec = w[:, o:o + 2 * GLA_LOWRANK]; o += 2 * GLA_LOWRANK
    q = w[:, o:o + GLA_KEY]; o += GLA_KEY
    g = w[:, o:o + GLA_VAL]; o += GLA_VAL
    conv = w[:, o:o + 3 * CONV_DIM]
    pad = jnp.zeros((d, LANES - 2 * GLA_LOWRANK), w.dtype)
    return jnp.concatenate([v, g, k, q, conv, dec, pad], axis=1).astype(BF16)


def _decay_weights(w_up, b_dec, pair):
    lo = pair * 2 * GLA_DK
    outs = []
    for d in range(2):
        w = jnp.zeros((LANES, LANES), F32).at[d * GLA_LOWRANK:(d + 1) * GLA_LOWRANK, :].set(w_up[d][:, lo:lo + LANES])
        outs.append((w, b_dec[d][lo:lo + LANES].reshape(1, LANES)))
    return outs[0][0], outs[1][0], outs[0][1], outs[1][1]


def _to_scan_order(a, ctx_len, rows):
    b, _, f = a.shape
    lat = a[:, ctx_len:].reshape(b, rows, GRID_W, f).swapaxes(1, 2).reshape(b, rows * GRID_W, f)
    return jnp.concatenate([a[:, :ctx_len], lat], axis=1)


def _from_scan_order(a, ctx_len, rows):
    b, _, f = a.shape
    lat = a[:, ctx_len:].reshape(b, GRID_W, rows, f).swapaxes(1, 2).reshape(b, rows * GRID_W, f)
    return jnp.concatenate([a[:, :ctx_len], lat], axis=1)


def kernel(x, c, ctx, c_ctx, w_mod, b_mod, w_in, w_conv, w_decay_up, b_decay, g_gla_norm, w_out, ln1_g, ln1_b, w_router, router_bias, w_e_gate, w_e_up, w_e_down, w_s_gate, w_s_up, w_s_down, ln2_g, ln2_b):
    bsz, seq, d = x.shape
    ctx_len = ctx.shape[1]
    depth = w_mod.shape[0]
    n_exp = w_router.shape[2]
    rows = seq // GRID_W
    l = ctx_len + seq
    t = bsz * l
    assert ctx_len == TOK_TILE and seq % TOK_TILE == 0 and d == ROW_TILES * LANES
    assert t % DSP_TILE == 0 and bsz + 1 <= SUBLANES
    tiles_per_batch = l // TOK_TILE
    alpha = float((2 * depth) ** 0.25)

    def group_of_tile(i):
        return jnp.where(i % tiles_per_batch == 0, bsz, i // tiles_per_batch)

    cond8 = jnp.zeros((SUBLANES, d), F32).at[:bsz].set(c).at[bsz].set(c_ctx)
    mod = _modulation(cond8, w_mod, b_mod).reshape(depth, SUBLANES, 6, 1, d)

    nk = t * TOP_K
    n_blocks = -(-(nk + n_exp * (MOE_BLOCK - 1)) // MOE_BLOCK)
    n_slots = n_blocks * MOE_BLOCK

    h = jnp.concatenate([ctx, x], axis=1).reshape(t, d)
    for i in range(depth):
        m = [mod[i, :, j] for j in range(6)]
        p = _inproj(h, m[0], m[1], _reorder_w_in(w_in[i]), group_of_tile)
        p3 = p.reshape(bsz, l, P_COLS)

        wf, wb, bf, bb = _decay_weights(w_decay_up[i], b_decay[i], 0)
        o_fr, o_br = _gla_pair(p3, P_K // LANES, P_Q // LANES, P_V // (2 * GLA_DV), P_DEC // LANES, wf, wb, bf, bb)
        col_in = jnp.concatenate([p3[..., P_V + 2 * GLA_DV:P_V + 4 * GLA_DV], p3[..., P_K + LANES:P_K + 2 * LANES],
                                  p3[..., P_Q + LANES:P_Q + 2 * LANES], p3[..., P_DEC:P_DEC + LANES]], axis=-1)
        col_in = _to_scan_order(col_in, ctx_len, rows)
        wf, wb, bf, bb = _decay_weights(w_decay_up[i], b_decay[i], 1)
        o_fc, o_bc = _gla_pair(col_in, 2, 3, 0, 4, wf, wb, bf, bb)
        o_fc = _from_scan_order(o_fc, ctx_len, rows)
        o_bc = _from_scan_order(o_bc, ctx_len, rows)

        flat = lambda a: a.reshape(t, a.shape[-1])
        h1 = _outproj(p, flat(o_fr), flat(o_br), flat(o_fc), flat(o_bc), h, m[2], w_conv[i],
                      g_gla_norm[i].reshape(1, GLA_DV), w_out[i].astype(BF16), ln1_g[i].reshape(1, d), ln1_b[i].reshape(1, d),
                      group_of_tile, alpha, tiles_per_batch)

        wr = w_router[i]
        wr_hi = wr.astype(BF16)
        wr_lo = (wr - wr_hi.astype(F32)).astype(BF16)
        x3, idx, wts, rank, counts, shared = _route(
            h1, m[3], m[4], wr_hi, wr_lo, router_bias[i].reshape(1, n_exp),
            w_s_gate[i].astype(BF16), w_s_up[i].astype(BF16), w_s_down[i].astype(BF16), group_of_tile)

        cnt = counts[0].astype(I32)
        padded = (cnt + MOE_BLOCK - 1) // MOE_BLOCK * MOE_BLOCK
        pend = jnp.cumsum(padded)
        pstart = pend - padded
        slots = _slots(idx, rank, pstart.astype(F32).reshape(1, n_exp))[:, :TOP_K].reshape(nk)

        xs3 = _dispatch(x3, slots, n_slots)
        ys2 = _experts(xs3.reshape(n_slots * ROW_TILES, LANES), pstart, padded // MOE_BLOCK,
                       w_e_gate[i], w_e_up[i], w_e_down[i])
        h = _combine(ys2.reshape(n_slots, ROW_TILES, LANES), slots, wts, shared, h1, m[5],
                     ln2_g[i].reshape(1, d), ln2_b[i].reshape(1, d), group_of_tile, alpha)

    return h.reshape(bsz, l, d)[:, ctx_len:]
```

```python
import functools

import jax
import jax.numpy as jnp
from jax import lax
from jax.experimental import pallas as pl
from jax.experimental.pallas import tpu as pltpu

F32 = jnp.float32
BF16 = jnp.bfloat16
I32 = jnp.int32

GRID_W = 64
CONV_DIM = 512
GLA_HEADS = 4
GLA_DK = 64
GLA_DV = 128
GLA_KEY = GLA_HEADS * GLA_DK
GLA_VAL = GLA_HEADS * GLA_DV
GLA_LOWRANK = 16
GLA_GATE_TEMP = 16.0
GLA_CHUNK = 64
TOP_K = 8
N_GROUPS = 8
TOPK_GROUPS = 4
ROUTED_SCALE = 2.5
LN_EPS = 1e-5
RMS_EPS = 1e-6

LANES = 128
SUBLANES = 8
VMEM_LIMIT = 56 * 1024 * 1024

P_V = 0
P_G = 512
P_K = 1024
P_Q = 1280
P_CONV = 1536
P_DEC = 3072
P_COLS = 3200

TOK_TILE = 256
MOE_BLOCK = 128
EXP_BUFS = 4
CMB_TILE = 128
DSP_TILE = 512
ROW_TILES = 8


def _cparams(sem):
    return pltpu.CompilerParams(dimension_semantics=sem, vmem_limit_bytes=VMEM_LIMIT)


def _mod_kernel(c_ref, w_ref, b_ref, o_ref):
    c = c_ref[...]
    a = c * jax.nn.sigmoid(c)
    o_ref[0] = jnp.dot(a, w_ref[0], preferred_element_type=F32, precision=lax.Precision.HIGHEST) + b_ref[0]


def _modulation(cond8, w_mod, b_mod):
    depth, d, n = w_mod.shape
    tn = 1536
    return pl.pallas_call(
        _mod_kernel,
        grid=(depth, n // tn),
        in_specs=[
            pl.BlockSpec((SUBLANES, d), lambda i, j: (0, 0)),
            pl.BlockSpec((1, d, tn), lambda i, j: (i, 0, j)),
            pl.BlockSpec((1, 1, tn), lambda i, j: (i, 0, j)),
        ],
        out_specs=pl.BlockSpec((1, SUBLANES, tn), lambda i, j: (i, 0, j)),
        out_shape=jax.ShapeDtypeStruct((depth, SUBLANES, n), F32),
        compiler_params=_cparams(("arbitrary", "arbitrary")),
        name="adaln_mod",
    )(cond8, w_mod, b_mod.reshape(depth, 1, n))


def _inproj_kernel(x_ref, sh_ref, sc_ref, w_ref, o_ref):
    h = x_ref[...] * (1.0 + sc_ref[0]) + sh_ref[0]
    o_ref[...] = jnp.dot(h.astype(BF16), w_ref[...], preferred_element_type=F32)


def _inproj(h, shift, scale, w, group_of_tile):
    t, d = h.shape
    n = w.shape[1]
    grp = lambda i: (group_of_tile(i), 0, 0)
    return pl.pallas_call(
        _inproj_kernel,
        grid=(t // TOK_TILE,),
        in_specs=[
            pl.BlockSpec((TOK_TILE, d), lambda i: (i, 0)),
            pl.BlockSpec((1, 1, d), grp),
            pl.BlockSpec((1, 1, d), grp),
            pl.BlockSpec((d, n), lambda i: (0, 0)),
        ],
        out_specs=pl.BlockSpec((TOK_TILE, n), lambda i: (i, 0)),
        out_shape=jax.ShapeDtypeStruct((t, n), F32),
        compiler_params=_cparams(("arbitrary",)),
        name="inproj",
    )(h, shift, scale, w)


def _split3(x):
    a = x.astype(BF16)
    r = x - a.astype(F32)
    b = r.astype(BF16)
    c = (r - b.astype(F32)).astype(BF16)
    return a, b, c


def _dot(a, b):
    return jnp.dot(a, b, preferred_element_type=F32)


def _dot_nt(a, b):
    return lax.dot_general(a, b, (((1,), (1,)), ((), ())), preferred_element_type=F32)


def _gla_direction(k2, q2, v2, dec, wup, bdec, s_ref, o_ref, reverse):
    t = TOK_TILE
    c = GLA_CHUNK
    z = jnp.dot(dec, wup, preferred_element_type=F32, precision=lax.Precision.HIGHEST) + bdec
    la = (jnp.minimum(z, 0.0) - jnp.log(1.0 + jnp.exp(-jnp.abs(z)))) * (1.0 / GLA_GATE_TEMP)

    row = lax.broadcasted_iota(I32, (t, t), 0)
    col = lax.broadcasted_iota(I32, (t, t), 1)
    same = (row // c) == (col // c)
    tri = jnp.logical_and(same, (col >= row) if reverse else (col <= row))
    tri_b = jnp.where(tri, 1.0, 0.0).astype(BF16)
    same_b = jnp.where(same, 1.0, 0.0).astype(BF16)
    la3 = _split3(la)
    bcum = _dot(tri_b, la3[0]) + _dot(tri_b, la3[1]) + _dot(tri_b, la3[2])
    btot = _dot(same_b, la3[0]) + _dot(same_b, la3[1]) + _dot(same_b, la3[2])

    q_dec = q2 * jnp.exp(bcum) * (GLA_DK ** -0.5)
    k_inv = (k2 * jnp.exp(-bcum)).astype(BF16)
    k_tail = k2 * jnp.exp(btot - bcum)
    chunk_decay = jnp.exp(btot)

    lane = lax.broadcasted_iota(I32, (t, 2 * GLA_DK), 1)
    order = range(t // c - 1, -1, -1) if reverse else range(t // c)
    for hh in range(2):
        in_head = (lane // GLA_DK) == hh
        qd = jnp.where(in_head, q_dec, 0.0).astype(BF16)
        kt = jnp.where(in_head, k_tail, 0.0).astype(BF16)
        vh = v2[:, hh * GLA_DV:(hh + 1) * GLA_DV]
        vh_b = vh.astype(BF16)
        att = jnp.where(tri, _dot_nt(qd, k_inv), 0.0).astype(BF16)
        o_intra = _dot(att, vh_b)
        s = s_ref[hh]
        for ci in order:
            rows = slice(ci * c, (ci + 1) * c)
            o_ref[0, rows, hh * GLA_DV:(hh + 1) * GLA_DV] = o_intra[rows] + _dot_nt(qd[rows], s.astype(BF16))
            kv_t = _dot(vh[rows].T.astype(BF16), kt[rows])
            s = s * chunk_decay[ci * c:ci * c + 1, :] + kv_t
        s_ref[hh] = s


def _gla_kernel(kf, qf, vf, df, kb, qb, vb, db, wupf, wupb, bf, bb, of_ref, ob_ref, sf_ref, sb_ref):
    @pl.when(pl.program_id(1) == 0)
    def _():
        sf_ref[...] = jnp.zeros_like(sf_ref)
        sb_ref[...] = jnp.zeros_like(sb_ref)

    _gla_direction(kf[0], qf[0], vf[0], df[0], wupf[...], bf[...], sf_ref, of_ref, False)
    _gla_direction(kb[0], qb[0], vb[0], db[0], wupb[...], bb[...], sb_ref, ob_ref, True)


def _gla_pair(src, col_k, col_q, col_v, col_dec, wupf, wupb, bf, bb):
    bsz, l, _ = src.shape
    nb = l // TOK_TILE
    fwd = lambda j: j
    bwd = lambda j: jnp.where(j == 0, 0, nb - j)

    def spec(width, colidx, order):
        return pl.BlockSpec((1, TOK_TILE, width), lambda b, j: (b, order(j), colidx))

    wspec = pl.BlockSpec((LANES, LANES), lambda b, j: (0, 0))
    bspec = pl.BlockSpec((1, LANES), lambda b, j: (0, 0))
    out = jax.ShapeDtypeStruct((bsz, l, 2 * GLA_DV), F32)
    return pl.pallas_call(
        _gla_kernel,
        grid=(bsz, nb),
        in_specs=[spec(LANES, col_k, fwd), spec(LANES, col_q, fwd), spec(2 * GLA_DV, col_v, fwd), spec(LANES, col_dec, fwd),
                  spec(LANES, col_k, bwd), spec(LANES, col_q, bwd), spec(2 * GLA_DV, col_v, bwd), spec(LANES, col_dec, bwd),
                  wspec, wspec, bspec, bspec],
        out_specs=[pl.BlockSpec((1, TOK_TILE, 2 * GLA_DV), lambda b, j: (b, fwd(j), 0)),
                   pl.BlockSpec((1, TOK_TILE, 2 * GLA_DV), lambda b, j: (b, bwd(j), 0))],
        out_shape=[out, out],
        scratch_shapes=[pltpu.VMEM((2, GLA_DV, LANES), F32), pltpu.VMEM((2, GLA_DV, LANES), F32)],
        compiler_params=_cparams(("arbitrary", "arbitrary")),
        name="gla_pair",
    )(src, src, src, src, src, src, src, src, wupf, wupb, bf, bb)


def _layer_norm(v, g, b):
    mu = jnp.mean(v, axis=-1, keepdims=True)
    dv = v - mu
    var = jnp.mean(dv * dv, axis=-1, keepdims=True)
    return dv * lax.rsqrt(var + LN_EPS) * g + b


def _outproj_kernel(conv_ref, gate_ref, ofr, obr, ofc, obc, h_ref, gt_ref, wconv_ref, gn_ref, wout_ref,
                    lng_ref, lnb_ref, o_ref, *, alpha, tiles_per_batch):
    tm = TOK_TILE
    is_ctx = (pl.program_id(0) % tiles_per_batch) == 0
    seg = jnp.where(is_ctx, tm, GRID_W)

    cv = conv_ref[...]
    b_gate = cv[:, :CONV_DIM]
    u = cv[:, CONV_DIM:2 * CONV_DIM] * cv[:, 2 * CONV_DIM:]
    pos = lax.broadcasted_iota(I32, (tm, CONV_DIM), 0) & (seg - 1)
    prev = jnp.where(pos == 0, 0.0, pltpu.roll(u, 1, 0))
    nxt = jnp.where(pos == seg - 1, 0.0, pltpu.roll(u, tm - 1, 0))
    wc = wconv_ref[...]
    mix_conv = b_gate * (wc[0:1, :] * prev + wc[1:2, :] * u + wc[2:3, :] * nxt)

    o_r = ofr[...] + obr[...]
    o_c = ofc[...] + obc[...]
    gate = gate_ref[...]
    gn = gn_ref[...]
    parts = [mix_conv.astype(BF16)]
    for hd in range(GLA_HEADS):
        src = o_r if hd < 2 else o_c
        o = src[:, (hd % 2) * GLA_DV:(hd % 2 + 1) * GLA_DV]
        o = o * lax.rsqrt(jnp.mean(o * o, axis=-1, keepdims=True) + RMS_EPS) * gn
        g = gate[:, hd * GLA_DV:(hd + 1) * GLA_DV]
        parts.append((o * (g * jax.nn.sigmoid(g))).astype(BF16))
    mix = jnp.concatenate(parts, axis=-1)
    y = _dot(mix, wout_ref[...])
    v = alpha * h_ref[...] + gt_ref[0] * y
    o_ref[...] = _layer_norm(v, lng_ref[...], lnb_ref[...])


def _outproj(p, o_fr, o_br, o_fc, o_bc, h, gate1, w_conv, g_norm, w_out, ln_g, ln_b, group_of_tile, alpha, tiles_per_batch):
    t, d = h.shape
    tm = TOK_TILE
    tok = lambda width, colidx: pl.BlockSpec((tm, width), lambda i: (i, colidx))
    full = lambda a: pl.BlockSpec(a.shape, lambda i: (0,) * a.ndim)
    return pl.pallas_call(
        functools.partial(_outproj_kernel, alpha=alpha, tiles_per_batch=tiles_per_batch),
        grid=(t // tm,),
        in_specs=[tok(3 * CONV_DIM, P_CONV // (3 * CONV_DIM)), tok(GLA_VAL, P_G // GLA_VAL),
                  tok(2 * GLA_DV, 0), tok(2 * GLA_DV, 0), tok(2 * GLA_DV, 0), tok(2 * GLA_DV, 0),
                  tok(d, 0), pl.BlockSpec((1, 1, d), lambda i: (group_of_tile(i), 0, 0)),
                  full(w_conv), full(g_norm), full(w_out), full(ln_g), full(ln_b)],
        out_specs=pl.BlockSpec((tm, d), lambda i: (i, 0)),
        out_shape=jax.ShapeDtypeStruct((t, d), F32),
        compiler_params=_cparams(("arbitrary",)),
        name="outproj_ln1",
    )(p, p, o_fr, o_br, o_fc, o_bc, h, gate1, w_conv, g_norm, w_out, ln_g, ln_b)


def _route_kernel(h_ref, sh_ref, sc_ref, wr_hi_ref, wr_lo_ref, rb_ref, wsg_ref, wsu_ref, wsd_ref,
                  x3_ref, idx_ref, wt_ref, rank_ref, cnt_ref, shared_ref, base_ref):
    tm = TOK_TILE
    n_exp = rb_ref.shape[1]
    gsz = n_exp // N_GROUPS

    @pl.when(pl.program_id(0) == 0)
    def _():
        base_ref[...] = jnp.zeros_like(base_ref)

    u = h_ref[...] * (1.0 + sc_ref[0]) + sh_ref[0]
    for j in range(ROW_TILES):
        x3_ref[pl.ds(j, tm, stride=ROW_TILES), :] = u[:, j * LANES:(j + 1) * LANES]

    u_hi = u.astype(BF16)
    u_lo = (u - u_hi.astype(F32)).astype(BF16)
    shared_ref[...] = _dot((jax.nn.silu(_dot(u_hi, wsg_ref[...])) * _dot(u_hi, wsu_ref[...])).astype(BF16), wsd_ref[...])

    wr_hi = wr_hi_ref[...]
    logits = _dot(u_hi, wr_hi) + _dot(u_lo, wr_hi) + _dot(u_hi, wr_lo_ref[...])
    s = jax.nn.sigmoid(logits)
    biased = s + rb_ref[...]

    lane = lax.broadcasted_iota(I32, (tm, n_exp), 1)
    grp = lane // gsz
    neg = -jnp.inf
    gscore = []
    for g in range(N_GROUPS):
        vals = jnp.where(grp == g, biased, neg)
        m1 = jnp.max(vals, axis=-1, keepdims=True)
        i1 = jnp.min(jnp.where(vals == m1, lane, n_exp), axis=-1, keepdims=True)
        m2 = jnp.max(jnp.where(lane == i1, neg, vals), axis=-1, keepdims=True)
        gscore.append(m1 + m2)
    keep = jnp.zeros((tm, n_exp), F32)
    for g in range(N_GROUPS):
        ahead = jnp.zeros((tm, 1), F32)
        for g2 in range(N_GROUPS):
            if g2 == g:
                continue
            wins = (gscore[g2] >= gscore[g]) if g2 < g else (gscore[g2] > gscore[g])
            ahead = ahead + jnp.where(wins, 1.0, 0.0)
        keep = jnp.where(grp == g, jnp.where(ahead < TOPK_GROUPS, 1.0, 0.0), keep)
    masked = jnp.where(keep > 0.0, biased, neg)

    onehot = jnp.zeros((tm, n_exp), F32)
    idxs = []
    for _ in range(TOP_K):
        m = jnp.max(masked, axis=-1, keepdims=True)
        i = jnp.min(jnp.where(masked == m, lane, n_exp), axis=-1, keepdims=True)
        hit = lane == i
        onehot = jnp.where(hit, 1.0, onehot)
        masked = jnp.where(hit, neg, masked)
        idxs.append(i)
    sel = onehot * s
    wnorm = sel / jnp.sum(sel, axis=-1, keepdims=True) * ROUTED_SCALE

    row = lax.broadcasted_iota(I32, (tm, tm), 0)
    col = lax.broadcasted_iota(I32, (tm, tm), 1)
    strict_lower = jnp.where(col < row, 1.0, 0.0).astype(BF16)
    rank_all = _dot(strict_lower, onehot.astype(BF16)) + base_ref[...]
    base_new = base_ref[...] + jnp.sum(onehot, axis=0, keepdims=True)
    base_ref[...] = base_new
    cnt_ref[...] = base_new

    out_lane = lax.broadcasted_iota(I32, (tm, LANES), 1)
    idx_o = jnp.zeros((tm, LANES), I32)
    wt_o = jnp.zeros((tm, LANES), F32)
    rank_o = jnp.zeros((tm, LANES), F32)
    for k in range(TOP_K):
        hit = lane == idxs[k]
        wk = jnp.sum(jnp.where(hit, wnorm, 0.0), axis=-1, keepdims=True)
        rk = jnp.sum(jnp.where(hit, rank_all, 0.0), axis=-1, keepdims=True)
        idx_o = jnp.where(out_lane == k, idxs[k], idx_o)
        wt_o = jnp.where(out_lane == k, wk, wt_o)
        rank_o = jnp.where(out_lane == k, rk, rank_o)
    idx_ref[...] = idx_o
    wt_ref[...] = wt_o
    rank_ref[...] = rank_o.astype(I32)


def _route(h1, shift, scale, wr_hi, wr_lo, rbias, wsg, wsu, wsd, group_of_tile):
    t, d = h1.shape
    tm = TOK_TILE
    n_exp = rbias.shape[1]
    grp = lambda i: (group_of_tile(i), 0, 0)
    full = lambda a: pl.BlockSpec(a.shape, lambda i: (0,) * a.ndim)
    lanes_out = lambda dt: jax.ShapeDtypeStruct((t, LANES), dt)
    return pl.pallas_call(
        _route_kernel,
        grid=(t // tm,),
        in_specs=[pl.BlockSpec((tm, d), lambda i: (i, 0)), pl.BlockSpec((1, 1, d), grp), pl.BlockSpec((1, 1, d), grp),
                  full(wr_hi), full(wr_lo), full(rbias), full(wsg), full(wsu), full(wsd)],
        out_specs=[pl.BlockSpec((tm * ROW_TILES, LANES), lambda i: (i, 0)),
                   pl.BlockSpec((tm, LANES), lambda i: (i, 0)),
                   pl.BlockSpec((tm, LANES), lambda i: (i, 0)),
                   pl.BlockSpec((tm, LANES), lambda i: (i, 0)),
                   pl.BlockSpec((1, n_exp), lambda i: (0, 0)),
                   pl.BlockSpec((tm, d), lambda i: (i, 0))],
        out_shape=[jax.ShapeDtypeStruct((t * ROW_TILES, LANES), F32), lanes_out(I32), lanes_out(F32), lanes_out(I32),
                   jax.ShapeDtypeStruct((1, n_exp), F32), jax.ShapeDtypeStruct((t, d), F32)],
        scratch_shapes=[pltpu.VMEM((1, n_exp), F32)],
        compiler_params=_cparams(("arbitrary",)),
        name="route_shared",
    )(h1, shift, scale, wr_hi, wr_lo, rbias, wsg, wsu, wsd)


def _slots_kernel(idx_ref, rank_ref, pstart_ref, o_ref):
    tm = TOK_TILE
    n_exp = pstart_ref.shape[1]
    idx = idx_ref[...]
    lane = lax.broadcasted_iota(I32, (tm, n_exp), 1)
    out_lane = lax.broadcasted_iota(I32, (tm, LANES), 1)
    start = jnp.zeros((tm, LANES), F32)
    for k in range(TOP_K):
        hit = lane == idx[:, k:k + 1]
        sk = jnp.sum(jnp.where(hit, pstart_ref[...], 0.0), axis=-1, keepdims=True)
        start = jnp.where(out_lane == k, sk, start)
    o_ref[...] = start.astype(I32) + rank_ref[...]


def _slots(idx, rank, pstart_f):
    t = idx.shape[0]
    tok = pl.BlockSpec((TOK_TILE, LANES), lambda i: (i, 0))
    return pl.pallas_call(
        _slots_kernel,
        grid=(t // TOK_TILE,),
        in_specs=[tok, tok, pl.BlockSpec(pstart_f.shape, lambda i: (0, 0))],
        out_specs=tok,
        out_shape=jax.ShapeDtypeStruct((t, LANES), I32),
        compiler_params=_cparams(("arbitrary",)),
        name="slots",
    )(idx, rank, pstart_f)


def _dispatch_kernel(slot_ref, x_ref, xs_hbm, sem):
    def body(i, carry):
        src = x_ref.at[pl.ds(pl.multiple_of(i * ROW_TILES, ROW_TILES), ROW_TILES)]
        for k in range(TOP_K):
            pltpu.make_async_copy(src, xs_hbm.at[slot_ref[i * TOP_K + k]], sem).start()
        return carry

    lax.fori_loop(0, DSP_TILE, body, 0)
    done = xs_hbm.at[pl.ds(0, DSP_TILE * TOP_K)]
    pltpu.make_async_copy(done, done, sem).wait()


def _dispatch(x2, slots, n_slots):
    t = x2.shape[0] // ROW_TILES
    return pl.pallas_call(
        _dispatch_kernel,
        grid=(t // DSP_TILE,),
        in_specs=[pl.BlockSpec((DSP_TILE * TOP_K,), lambda i: (i,), memory_space=pltpu.SMEM),
                  pl.BlockSpec((DSP_TILE * ROW_TILES, LANES), lambda i: (i, 0))],
        out_specs=pl.BlockSpec(memory_space=pl.ANY),
        out_shape=jax.ShapeDtypeStruct((n_slots, ROW_TILES, LANES), F32),
        scratch_shapes=[pltpu.SemaphoreType.DMA(())],
        compiler_params=_cparams(("arbitrary",)),
        name="dispatch",
    )(slots, x2)


def _expert_kernel(gstart_ref, nblk_ref, ntot_ref, xs_hbm, wg_ref, wu_ref, wd_ref, ys_hbm,
                   xbuf, ybuf, wg_b, wu_b, wd_b, sem_in, sem_out):
    e = pl.program_id(0)
    nb = nblk_ref[e]
    g0 = gstart_ref[e]
    total = ntot_ref[0]
    blk_rows = MOE_BLOCK * ROW_TILES
    ahead = EXP_BUFS - 1

    def rows_of(g):
        return pl.ds(pl.multiple_of(g * blk_rows, blk_rows), blk_rows)

    def in_copy(g):
        s = g % EXP_BUFS
        return pltpu.make_async_copy(xs_hbm.at[rows_of(g)], xbuf.at[s], sem_in.at[s])

    def out_copy(g):
        s = g % EXP_BUFS
        return pltpu.make_async_copy(ybuf.at[s], ys_hbm.at[rows_of(g)], sem_out.at[s])

    @pl.when(e == 0)
    def _():
        for g in range(ahead):
            @pl.when(g < total)
            def _():
                in_copy(g).start()

    @pl.when(nb > 0)
    def _():
        wg_b[...] = wg_ref[0, 0].astype(BF16)
        wu_b[...] = wu_ref[0, 0].astype(BF16)
        wd_b[...] = wd_ref[0, 0].astype(BF16)

    def body(b, carry):
        g = g0 + b
        s = g % EXP_BUFS

        @pl.when(g + ahead < total)
        def _():
            in_copy(g + ahead).start()

        in_copy(g).wait()

        @pl.when(g >= EXP_BUFS)
        def _():
            out_copy(g - EXP_BUFS).wait()

        x = jnp.concatenate([xbuf[s, pl.ds(j, MOE_BLOCK, stride=ROW_TILES), :] for j in range(ROW_TILES)], axis=-1)
        xb = x.astype(BF16)
        hid = jax.nn.silu(_dot(xb, wg_b[...])) * _dot(xb, wu_b[...])
        y = _dot(hid.astype(BF16), wd_b[...])
        for j in range(ROW_TILES):
            ybuf[s, pl.ds(j, MOE_BLOCK, stride=ROW_TILES), :] = y[:, j * LANES:(j + 1) * LANES]
        out_copy(g).start()
        return carry

    lax.fori_loop(0, nb, body, 0)

    @pl.when(e == pl.num_programs(0) - 1)
    def _():
        for back in range(EXP_BUFS):
            @pl.when(total - 1 - back >= 0)
            def _():
                out_copy(total - 1 - back).wait()


def _experts(xs2, gstart, nblk, ntot, layer, w_gate, w_up, w_down):
    n_rows = xs2.shape[0]
    _, n_exp, d, hdim = w_gate.shape
    blk_rows = MOE_BLOCK * ROW_TILES
    wmap = lambda e, gs, nb, nt: (layer, e, 0, 0)
    return pl.pallas_call(
        _expert_kernel,
        grid_spec=pltpu.PrefetchScalarGridSpec(
            num_scalar_prefetch=3,
            grid=(n_exp,),
            in_specs=[pl.BlockSpec(memory_space=pl.ANY),
                      pl.BlockSpec((1, 1, d, hdim), wmap),
                      pl.BlockSpec((1, 1, d, hdim), wmap),
                      pl.BlockSpec((1, 1, hdim, d), wmap)],
            out_specs=pl.BlockSpec(memory_space=pl.ANY),
            scratch_shapes=[pltpu.VMEM((EXP_BUFS, blk_rows, LANES), F32), pltpu.VMEM((EXP_BUFS, blk_rows, LANES), F32),
                            pltpu.VMEM((d, hdim), BF16), pltpu.VMEM((d, hdim), BF16), pltpu.VMEM((hdim, d), BF16),
                            pltpu.SemaphoreType.DMA((EXP_BUFS,)), pltpu.SemaphoreType.DMA((EXP_BUFS,))],
        ),
        out_shape=jax.ShapeDtypeStruct((n_rows, LANES), F32),
        compiler_params=_cparams(("arbitrary",)),
        name="experts",
    )(gstart, nblk, ntot, xs2, w_gate, w_up, w_down)


def _combine_kernel(slot_cur, slot_nxt, ys_hbm, wt_ref, shared_ref, h_ref, gt_ref, lng_ref, lnb_ref, o_ref, buf, sem,
                    *, alpha):
    i = pl.program_id(0)
    n = pl.num_programs(0)
    tt = CMB_TILE

    def issue(slot_ref, b):
        def body(tok, carry):
            for k in range(TOP_K):
                r = tok * TOP_K + k
                dst = buf.at[b, pl.ds(pl.multiple_of(r * ROW_TILES, ROW_TILES), ROW_TILES)]
                pltpu.make_async_copy(ys_hbm.at[slot_ref[r]], dst, sem.at[b]).start()
            return carry
        lax.fori_loop(0, tt, body, 0)

    @pl.when(i == 0)
    def _():
        issue(slot_cur, 0)

    @pl.when(i + 1 < n)
    def _():
        issue(slot_nxt, (i + 1) % 2)

    cur = i % 2
    pltpu.make_async_copy(buf.at[cur], buf.at[cur], sem.at[cur]).wait()

    wt = wt_ref[...]
    cols = []
    for j in range(ROW_TILES):
        acc = jnp.zeros((tt, LANES), F32)
        for k in range(TOP_K):
            acc = acc + wt[:, k:k + 1] * buf[cur, pl.ds(k * ROW_TILES + j, tt, stride=TOP_K * ROW_TILES), :]
        cols.append(acc)
    f = shared_ref[...] + jnp.concatenate(cols, axis=-1)
    v = alpha * h_ref[...] + gt_ref[0] * f
    o_ref[...] = _layer_norm(v, lng_ref[...], lnb_ref[...])


def _combine(ys3, slots, wts, shared, h1, gate2, ln_g, ln_b, group_of_tile, alpha):
    t, d = h1.shape
    tt = CMB_TILE
    n = t // tt
    scale = TOK_TILE // tt
    full = lambda a: pl.BlockSpec(a.shape, lambda i: (0,) * a.ndim)
    return pl.pallas_call(
        functools.partial(_combine_kernel, alpha=alpha),
        grid=(n,),
        in_specs=[pl.BlockSpec((tt * TOP_K,), lambda i: (i,), memory_space=pltpu.SMEM),
                  pl.BlockSpec((tt * TOP_K,), lambda i: (jnp.minimum(i + 1, n - 1),), memory_space=pltpu.SMEM),
                  pl.BlockSpec(memory_space=pl.ANY),
                  pl.BlockSpec((tt, LANES), lambda i: (i, 0)),
                  pl.BlockSpec((tt, d), lambda i: (i, 0)),
                  pl.BlockSpec((tt, d), lambda i: (i, 0)),
                  pl.BlockSpec((1, 1, d), lambda i: (group_of_tile(i // scale), 0, 0)),
                  full(ln_g), full(ln_b)],
        out_specs=pl.BlockSpec((tt, d), lambda i: (i, 0)),
        out_shape=jax.ShapeDtypeStruct((t, d), F32),
        scratch_shapes=[pltpu.VMEM((2, tt * TOP_K * ROW_TILES, LANES), F32), pltpu.SemaphoreType.DMA((2,))],
        compiler_params=_cparams(("arbitrary",)),
        name="combine_ln2",
    )(slots, slots, ys3, wts, shared, h1, gate2, ln_g, ln_b)


def _reorder_w_in(w):
    d = w.shape[0]
    o = 0
    k = w[:, o:o + GLA_KEY]; o += GLA_KEY
    v = w[:, o:o + GLA_VAL]; o += GLA_VAL
    dec = w[:, o:o + 2 * GLA_LOWRANK]; o += 2 * GLA_LOWRANK
    q = w[:, o:o + GLA_KEY]; o += GLA_KEY
    g = w[:, o:o + GLA_VAL]; o += GLA_VAL
    conv = w[:, o:o + 3 * CONV_DIM]
    pad = jnp.zeros((d, LANES - 2 * GLA_LOWRANK), w.dtype)
    return jnp.concatenate([v, g, k, q, conv, dec, pad], axis=1).astype(BF16)


def _decay_weights(w_up, b_dec, pair):
    lo = pair * 2 * GLA_DK
    outs = []
    for d in range(2):
        w = jnp.zeros((LANES, LANES), F32).at[d * GLA_LOWRANK:(d + 1) * GLA_LOWRANK, :].set(w_up[d][:, lo:lo + LANES])
        outs.append((w, b_dec[d][lo:lo + LANES].reshape(1, LANES)))
    return outs[0][0], outs[1][0], outs[0][1], outs[1][1]


def _to_scan_order(a, ctx_len, rows):
    b, _, f = a.shape
    lat = a[:, ctx_len:].reshape(b, rows, GRID_W, f).swapaxes(1, 2).reshape(b, rows * GRID_W, f)
    return jnp.concatenate([a[:, :ctx_len], lat], axis=1)


def _from_scan_order(a, ctx_len, rows):
    b, _, f = a.shape
    lat = a[:, ctx_len:].reshape(b, GRID_W, rows, f).swapaxes(1, 2).reshape(b, rows * GRID_W, f)
    return jnp.concatenate([a[:, :ctx_len], lat], axis=1)


def kernel(x, c, ctx, c_ctx, w_mod, b_mod, w_in, w_conv, w_decay_up, b_decay, g_gla_norm, w_out, ln1_g, ln1_b, w_router, router_bias, w_e_gate, w_e_up, w_e_down, w_s_gate, w_s_up, w_s_down, ln2_g, ln2_b):
    bsz, seq, d = x.shape
    ctx_len = ctx.shape[1]
    depth = w_mod.shape[0]
    n_exp = w_router.shape[2]
    rows = seq // GRID_W
    l = ctx_len + seq
    t = bsz * l
    assert ctx_len == TOK_TILE and seq % TOK_TILE == 0 and d == ROW_TILES * LANES
    assert t % DSP_TILE == 0 and bsz + 1 <= SUBLANES
    tiles_per_batch = l // TOK_TILE
    alpha = float((2 * depth) ** 0.25)

    def group_of_tile(i):
        return jnp.where(i % tiles_per_batch == 0, bsz, i // tiles_per_batch)

    cond8 = jnp.zeros((SUBLANES, d), F32).at[:bsz].set(c).at[bsz].set(c_ctx)
    mod = _modulation(cond8, w_mod, b_mod).reshape(depth, SUBLANES, 6, 1, d)

    nk = t * TOP_K
    n_blocks = -(-(nk + n_exp * (MOE_BLOCK - 1)) // MOE_BLOCK)
    n_slots = n_blocks * MOE_BLOCK

    h = jnp.concatenate([ctx, x], axis=1).reshape(t, d)
    for i in range(depth):
        m = [mod[i, :, j] for j in range(6)]
        p = _inproj(h, m[0], m[1], _reorder_w_in(w_in[i]), group_of_tile)
        p3 = p.reshape(bsz, l, P_COLS)

        wf, wb, bf, bb = _decay_weights(w_decay_up[i], b_decay[i], 0)
        o_fr, o_br = _gla_pair(p3, P_K // LANES, P_Q // LANES, P_V // (2 * GLA_DV), P_DEC // LANES, wf, wb, bf, bb)
        col_in = jnp.concatenate([p3[..., P_V + 2 * GLA_DV:P_V + 4 * GLA_DV], p3[..., P_K + LANES:P_K + 2 * LANES],
                                  p3[..., P_Q + LANES:P_Q + 2 * LANES], p3[..., P_DEC:P_DEC + LANES]], axis=-1)
        col_in = _to_scan_order(col_in, ctx_len, rows)
        wf, wb, bf, bb = _decay_weights(w_decay_up[i], b_decay[i], 1)
        o_fc, o_bc = _gla_pair(col_in, 2, 3, 0, 4, wf, wb, bf, bb)
        o_fc = _from_scan_order(o_fc, ctx_len, rows)
        o_bc = _from_scan_order(o_bc, ctx_len, rows)

        flat = lambda a: a.reshape(t, a.shape[-1])
        h1 = _outproj(p, flat(o_fr), flat(o_br), flat(o_fc), flat(o_bc), h, m[2], w_conv[i],
                      g_gla_norm[i].reshape(1, GLA_DV), w_out[i].astype(BF16), ln1_g[i].reshape(1, d), ln1_b[i].reshape(1, d),
                      group_of_tile, alpha, tiles_per_batch)

        wr = w_router[i]
        wr_hi = wr.astype(BF16)
        wr_lo = (wr - wr_hi.astype(F32)).astype(BF16)
        x3, idx, wts, rank, counts, shared = _route(
            h1, m[3], m[4], wr_hi, wr_lo, router_bias[i].reshape(1, n_exp),
            w_s_gate[i].astype(BF16), w_s_up[i].astype(BF16), w_s_down[i].astype(BF16), group_of_tile)

        cnt = counts[0].astype(I32)
        padded = (cnt + MOE_BLOCK - 1) // MOE_BLOCK * MOE_BLOCK
        pend = jnp.cumsum(padded)
        pstart = pend - padded
        slots = _slots(idx, rank, pstart.astype(F32).reshape(1, n_exp))[:, :TOP_K].reshape(nk)

        xs3 = _dispatch(x3, slots, n_slots)
        ys2 = _experts(xs3.reshape(n_slots * ROW_TILES, LANES), pstart // MOE_BLOCK, padded // MOE_BLOCK,
                       (pend[-1] // MOE_BLOCK).reshape(1), i, w_e_gate, w_e_up, w_e_down)
        h = _combine(ys2.reshape(n_slots, ROW_TILES, LANES), slots, wts, shared, h1, m[5],
                     ln2_g[i].reshape(1, d), ln2_b[i].reshape(1, d), group_of_tile, alpha)

    return h.reshape(bsz, l, d)[:, ctx_len:]
```

```python
import functools

import jax
import jax.numpy as jnp
from jax import lax
from jax.experimental import pallas as pl
from jax.experimental.pallas import tpu as pltpu

F32 = jnp.float32
BF16 = jnp.bfloat16
I32 = jnp.int32

GRID_W = 64
CONV_DIM = 512
GLA_HEADS = 4
GLA_DK = 64
GLA_DV = 128
GLA_KEY = GLA_HEADS * GLA_DK
GLA_VAL = GLA_HEADS * GLA_DV
GLA_LOWRANK = 16
GLA_GATE_TEMP = 16.0
GLA_CHUNK = 64
TOP_K = 8
N_GROUPS = 8
TOPK_GROUPS = 4
ROUTED_SCALE = 2.5
LN_EPS = 1e-5
RMS_EPS = 1e-6

LANES = 128
SUBLANES = 8
VMEM_LIMIT = 56 * 1024 * 1024

P_V = 0
P_G = 512
P_K = 1024
P_Q = 1280
P_CONV = 1536
P_DEC = 3072
P_COLS = 3200

TOK_TILE = 256
MOE_BLOCK = 256
EXP_BUFS = 4
CMB_TILE = 128
DSP_TILE = 512
ROW_TILES = 8


def _cparams(sem):
    return pltpu.CompilerParams(dimension_semantics=sem, vmem_limit_bytes=VMEM_LIMIT)


def _mod_kernel(c_ref, w_ref, b_ref, o_ref):
    c = c_ref[...]
    a = c * jax.nn.sigmoid(c)
    o_ref[0] = jnp.dot(a, w_ref[0], preferred_element_type=F32, precision=lax.Precision.HIGHEST) + b_ref[0]


def _modulation(cond8, w_mod, b_mod):
    depth, d, n = w_mod.shape
    tn = 1536
    return pl.pallas_call(
        _mod_kernel,
        grid=(depth, n // tn),
        in_specs=[
            pl.BlockSpec((SUBLANES, d), lambda i, j: (0, 0)),
            pl.BlockSpec((1, d, tn), lambda i, j: (i, 0, j)),
            pl.BlockSpec((1, 1, tn), lambda i, j: (i, 0, j)),
        ],
        out_specs=pl.BlockSpec((1, SUBLANES, tn), lambda i, j: (i, 0, j)),
        out_shape=jax.ShapeDtypeStruct((depth, SUBLANES, n), F32),
        compiler_params=_cparams(("arbitrary", "arbitrary")),
        name="adaln_mod",
    )(cond8, w_mod, b_mod.reshape(depth, 1, n))


def _inproj_kernel(x_ref, sh_ref, sc_ref, w_ref, o_ref):
    h = x_ref[...] * (1.0 + sc_ref[0]) + sh_ref[0]
    o_ref[...] = jnp.dot(h.astype(BF16), w_ref[...], preferred_element_type=F32)


def _inproj(h, shift, scale, w, group_of_tile):
    t, d = h.shape
    n = w.shape[1]
    grp = lambda i: (group_of_tile(i), 0, 0)
    return pl.pallas_call(
        _inproj_kernel,
        grid=(t // TOK_TILE,),
        in_specs=[
            pl.BlockSpec((TOK_TILE, d), lambda i: (i, 0)),
            pl.BlockSpec((1, 1, d), grp),
            pl.BlockSpec((1, 1, d), grp),
            pl.BlockSpec((d, n), lambda i: (0, 0)),
        ],
        out_specs=pl.BlockSpec((TOK_TILE, n), lambda i: (i, 0)),
        out_shape=jax.ShapeDtypeStruct((t, n), F32),
        compiler_params=_cparams(("arbitrary",)),
        name="inproj",
    )(h, shift, scale, w)


def _split3(x):
    a = x.astype(BF16)
    r = x - a.astype(F32)
    b = r.astype(BF16)
    c = (r - b.astype(F32)).astype(BF16)
    return a, b, c


def _dot(a, b):
    return jnp.dot(a, b, preferred_element_type=F32)


def _dot_nt(a, b):
    return lax.dot_general(a, b, (((1,), (1,)), ((), ())), preferred_element_type=F32)


def _gla_direction(k2, q2, v2, dec, wup, bdec, s_ref, o_ref, reverse):
    t = TOK_TILE
    c = GLA_CHUNK
    z = jnp.dot(dec, wup, preferred_element_type=F32, precision=lax.Precision.HIGHEST) + bdec
    la = (jnp.minimum(z, 0.0) - jnp.log(1.0 + jnp.exp(-jnp.abs(z)))) * (1.0 / GLA_GATE_TEMP)

    row = lax.broadcasted_iota(I32, (t, t), 0)
    col = lax.broadcasted_iota(I32, (t, t), 1)
    same = (row // c) == (col // c)
    tri = jnp.logical_and(same, (col >= row) if reverse else (col <= row))
    tri_b = jnp.where(tri, 1.0, 0.0).astype(BF16)
    same_b = jnp.where(same, 1.0, 0.0).astype(BF16)
    la3 = _split3(la)
    bcum = _dot(tri_b, la3[0]) + _dot(tri_b, la3[1]) + _dot(tri_b, la3[2])
    btot = _dot(same_b, la3[0]) + _dot(same_b, la3[1]) + _dot(same_b, la3[2])

    q_dec = q2 * jnp.exp(bcum) * (GLA_DK ** -0.5)
    k_inv = (k2 * jnp.exp(-bcum)).astype(BF16)
    k_tail = k2 * jnp.exp(btot - bcum)
    chunk_decay = jnp.exp(btot)

    lane = lax.broadcasted_iota(I32, (t, 2 * GLA_DK), 1)
    order = range(t // c - 1, -1, -1) if reverse else range(t // c)
    for hh in range(2):
        in_head = (lane // GLA_DK) == hh
        qd = jnp.where(in_head, q_dec, 0.0).astype(BF16)
        kt = jnp.where(in_head, k_tail, 0.0).astype(BF16)
        vh = v2[:, hh * GLA_DV:(hh + 1) * GLA_DV]
        vh_b = vh.astype(BF16)
        att = jnp.where(tri, _dot_nt(qd, k_inv), 0.0).astype(BF16)
        o_intra = _dot(att, vh_b)
        s = s_ref[hh]
        for ci in order:
            rows = slice(ci * c, (ci + 1) * c)
            o_ref[0, rows, hh * GLA_DV:(hh + 1) * GLA_DV] = o_intra[rows] + _dot_nt(qd[rows], s.astype(BF16))
            kv_t = _dot(vh[rows].T.astype(BF16), kt[rows])
            s = s * chunk_decay[ci * c:ci * c + 1, :] + kv_t
        s_ref[hh] = s


def _gla_kernel(kf, qf, vf, df, kb, qb, vb, db, wupf, wupb, bf, bb, of_ref, ob_ref, sf_ref, sb_ref):
    @pl.when(pl.program_id(1) == 0)
    def _():
        sf_ref[...] = jnp.zeros_like(sf_ref)
        sb_ref[...] = jnp.zeros_like(sb_ref)

    _gla_direction(kf[0], qf[0], vf[0], df[0], wupf[...], bf[...], sf_ref, of_ref, False)
    _gla_direction(kb[0], qb[0], vb[0], db[0], wupb[...], bb[...], sb_ref, ob_ref, True)


def _gla_pair(src, col_k, col_q, col_v, col_dec, wupf, wupb, bf, bb):
    bsz, l, _ = src.shape
    nb = l // TOK_TILE
    fwd = lambda j: j
    bwd = lambda j: jnp.where(j == 0, 0, nb - j)

    def spec(width, colidx, order):
        return pl.BlockSpec((1, TOK_TILE, width), lambda b, j: (b, order(j), colidx))

    wspec = pl.BlockSpec((LANES, LANES), lambda b, j: (0, 0))
    bspec = pl.BlockSpec((1, LANES), lambda b, j: (0, 0))
    out = jax.ShapeDtypeStruct((bsz, l, 2 * GLA_DV), F32)
    return pl.pallas_call(
        _gla_kernel,
        grid=(bsz, nb),
        in_specs=[spec(LANES, col_k, fwd), spec(LANES, col_q, fwd), spec(2 * GLA_DV, col_v, fwd), spec(LANES, col_dec, fwd),
                  spec(LANES, col_k, bwd), spec(LANES, col_q, bwd), spec(2 * GLA_DV, col_v, bwd), spec(LANES, col_dec, bwd),
                  wspec, wspec, bspec, bspec],
        out_specs=[pl.BlockSpec((1, TOK_TILE, 2 * GLA_DV), lambda b, j: (b, fwd(j), 0)),
                   pl.BlockSpec((1, TOK_TILE, 2 * GLA_DV), lambda b, j: (b, bwd(j), 0))],
        out_shape=[out, out],
        scratch_shapes=[pltpu.VMEM((2, GLA_DV, LANES), F32), pltpu.VMEM((2, GLA_DV, LANES), F32)],
        compiler_params=_cparams(("arbitrary", "arbitrary")),
        name="gla_pair",
    )(src, src, src, src, src, src, src, src, wupf, wupb, bf, bb)


def _layer_norm(v, g, b):
    mu = jnp.mean(v, axis=-1, keepdims=True)
    dv = v - mu
    var = jnp.mean(dv * dv, axis=-1, keepdims=True)
    return dv * lax.rsqrt(var + LN_EPS) * g + b


def _outproj_kernel(conv_ref, gate_ref, ofr, obr, ofc, obc, h_ref, gt_ref, wconv_ref, gn_ref, wout_ref,
                    lng_ref, lnb_ref, o_ref, *, alpha, tiles_per_batch):
    tm = TOK_TILE
    is_ctx = (pl.program_id(0) % tiles_per_batch) == 0
    seg = jnp.where(is_ctx, tm, GRID_W)

    cv = conv_ref[...]
    b_gate = cv[:, :CONV_DIM]
    u = cv[:, CONV_DIM:2 * CONV_DIM] * cv[:, 2 * CONV_DIM:]
    pos = lax.broadcasted_iota(I32, (tm, CONV_DIM), 0) & (seg - 1)
    prev = jnp.where(pos == 0, 0.0, pltpu.roll(u, 1, 0))
    nxt = jnp.where(pos == seg - 1, 0.0, pltpu.roll(u, tm - 1, 0))
    wc = wconv_ref[...]
    mix_conv = b_gate * (wc[0:1, :] * prev + wc[1:2, :] * u + wc[2:3, :] * nxt)

    o_r = ofr[...] + obr[...]
    o_c = ofc[...] + obc[...]
    gate = gate_ref[...]
    gn = gn_ref[...]
    parts = [mix_conv.astype(BF16)]
    for hd in range(GLA_HEADS):
        src = o_r if hd < 2 else o_c
        o = src[:, (hd % 2) * GLA_DV:(hd % 2 + 1) * GLA_DV]
        o = o * lax.rsqrt(jnp.mean(o * o, axis=-1, keepdims=True) + RMS_EPS) * gn
        g = gate[:, hd * GLA_DV:(hd + 1) * GLA_DV]
        parts.append((o * (g * jax.nn.sigmoid(g))).astype(BF16))
    mix = jnp.concatenate(parts, axis=-1)
    y = _dot(mix, wout_ref[...])
    v = alpha * h_ref[...] + gt_ref[0] * y
    o_ref[...] = _layer_norm(v, lng_ref[...], lnb_ref[...])


def _outproj(p, o_fr, o_br, o_fc, o_bc, h, gate1, w_conv, g_norm, w_out, ln_g, ln_b, group_of_tile, alpha, tiles_per_batch):
    t, d = h.shape
    tm = TOK_TILE
    tok = lambda width, colidx: pl.BlockSpec((tm, width), lambda i: (i, colidx))
    full = lambda a: pl.BlockSpec(a.shape, lambda i: (0,) * a.ndim)
    return pl.pallas_call(
        functools.partial(_outproj_kernel, alpha=alpha, tiles_per_batch=tiles_per_batch),
        grid=(t // tm,),
        in_specs=[tok(3 * CONV_DIM, P_CONV // (3 * CONV_DIM)), tok(GLA_VAL, P_G // GLA_VAL),
                  tok(2 * GLA_DV, 0), tok(2 * GLA_DV, 0), tok(2 * GLA_DV, 0), tok(2 * GLA_DV, 0),
                  tok(d, 0), pl.BlockSpec((1, 1, d), lambda i: (group_of_tile(i), 0, 0)),
                  full(w_conv), full(g_norm), full(w_out), full(ln_g), full(ln_b)],
        out_specs=pl.BlockSpec((tm, d), lambda i: (i, 0)),
        out_shape=jax.ShapeDtypeStruct((t, d), F32),
        compiler_params=_cparams(("arbitrary",)),
        name="outproj_ln1",
    )(p, p, o_fr, o_br, o_fc, o_bc, h, gate1, w_conv, g_norm, w_out, ln_g, ln_b)


def _route_kernel(h_ref, sh_ref, sc_ref, wr_hi_ref, wr_lo_ref, rb_ref, wsg_ref, wsu_ref, wsd_ref,
                  x3_ref, idx_ref, wt_ref, rank_ref, cnt_ref, shared_ref, base_ref):
    tm = TOK_TILE
    n_exp = rb_ref.shape[1]
    gsz = n_exp // N_GROUPS

    @pl.when(pl.program_id(0) == 0)
    def _():
        base_ref[...] = jnp.zeros_like(base_ref)

    u = h_ref[...] * (1.0 + sc_ref[0]) + sh_ref[0]
    for j in range(ROW_TILES):
        x3_ref[pl.ds(j, tm, stride=ROW_TILES), :] = u[:, j * LANES:(j + 1) * LANES]

    u_hi = u.astype(BF16)
    u_lo = (u - u_hi.astype(F32)).astype(BF16)
    shared_ref[...] = _dot((jax.nn.silu(_dot(u_hi, wsg_ref[...])) * _dot(u_hi, wsu_ref[...])).astype(BF16), wsd_ref[...])

    wr_hi = wr_hi_ref[...]
    logits = _dot(u_hi, wr_hi) + _dot(u_lo, wr_hi) + _dot(u_hi, wr_lo_ref[...])
    s = jax.nn.sigmoid(logits)
    biased = s + rb_ref[...]

    lane = lax.broadcasted_iota(I32, (tm, n_exp), 1)
    grp = lane // gsz
    neg = -jnp.inf
    gscore = []
    for g in range(N_GROUPS):
        vals = jnp.where(grp == g, biased, neg)
        m1 = jnp.max(vals, axis=-1, keepdims=True)
        i1 = jnp.min(jnp.where(vals == m1, lane, n_exp), axis=-1, keepdims=True)
        m2 = jnp.max(jnp.where(lane == i1, neg, vals), axis=-1, keepdims=True)
        gscore.append(m1 + m2)
    keep = jnp.zeros((tm, n_exp), F32)
    for g in range(N_GROUPS):
        ahead = jnp.zeros((tm, 1), F32)
        for g2 in range(N_GROUPS):
            if g2 == g:
                continue
            wins = (gscore[g2] >= gscore[g]) if g2 < g else (gscore[g2] > gscore[g])
            ahead = ahead + jnp.where(wins, 1.0, 0.0)
        keep = jnp.where(grp == g, jnp.where(ahead < TOPK_GROUPS, 1.0, 0.0), keep)
    masked = jnp.where(keep > 0.0, biased, neg)

    onehot = jnp.zeros((tm, n_exp), F32)
    idxs = []
    for _ in range(TOP_K):
        m = jnp.max(masked, axis=-1, keepdims=True)
        i = jnp.min(jnp.where(masked == m, lane, n_exp), axis=-1, keepdims=True)
        hit = lane == i
        onehot = jnp.where(hit, 1.0, onehot)
        masked = jnp.where(hit, neg, masked)
        idxs.append(i)
    sel = onehot * s
    wnorm = sel / jnp.sum(sel, axis=-1, keepdims=True) * ROUTED_SCALE

    row = lax.broadcasted_iota(I32, (tm, tm), 0)
    col = lax.broadcasted_iota(I32, (tm, tm), 1)
    strict_lower = jnp.where(col < row, 1.0, 0.0).astype(BF16)
    rank_all = _dot(strict_lower, onehot.astype(BF16)) + base_ref[...]
    base_new = base_ref[...] + jnp.sum(onehot, axis=0, keepdims=True)
    base_ref[...] = base_new
    cnt_ref[...] = base_new

    out_lane = lax.broadcasted_iota(I32, (tm, LANES), 1)
    idx_o = jnp.zeros((tm, LANES), I32)
    wt_o = jnp.zeros((tm, LANES), F32)
    rank_o = jnp.zeros((tm, LANES), F32)
    for k in range(TOP_K):
        hit = lane == idxs[k]
        wk = jnp.sum(jnp.where(hit, wnorm, 0.0), axis=-1, keepdims=True)
        rk = jnp.sum(jnp.where(hit, rank_all, 0.0), axis=-1, keepdims=True)
        idx_o = jnp.where(out_lane == k, idxs[k], idx_o)
        wt_o = jnp.where(out_lane == k, wk, wt_o)
        rank_o = jnp.where(out_lane == k, rk, rank_o)
    idx_ref[...] = idx_o
    wt_ref[...] = wt_o
    rank_ref[...] = rank_o.astype(I32)


def _route(h1, shift, scale, wr_hi, wr_lo, rbias, wsg, wsu, wsd, group_of_tile):
    t, d = h1.shape
    tm = TOK_TILE
    n_exp = rbias.shape[1]
    grp = lambda i: (group_of_tile(i), 0, 0)
    full = lambda a: pl.BlockSpec(a.shape, lambda i: (0,) * a.ndim)
    lanes_out = lambda dt: jax.ShapeDtypeStruct((t, LANES), dt)
    return pl.pallas_call(
        _route_kernel,
        grid=(t // tm,),
        in_specs=[pl.BlockSpec((tm, d), lambda i: (i, 0)), pl.BlockSpec((1, 1, d), grp), pl.BlockSpec((1, 1, d), grp),
                  full(wr_hi), full(wr_lo), full(rbias), full(wsg), full(wsu), full(wsd)],
        out_specs=[pl.BlockSpec((tm * ROW_TILES, LANES), lambda i: (i, 0)),
                   pl.BlockSpec((tm, LANES), lambda i: (i, 0)),
                   pl.BlockSpec((tm, LANES), lambda i: (i, 0)),
                   pl.BlockSpec((tm, LANES), lambda i: (i, 0)),
                   pl.BlockSpec((1, n_exp), lambda i: (0, 0)),
                   pl.BlockSpec((tm, d), lambda i: (i, 0))],
        out_shape=[jax.ShapeDtypeStruct((t * ROW_TILES, LANES), F32), lanes_out(I32), lanes_out(F32), lanes_out(I32),
                   jax.ShapeDtypeStruct((1, n_exp), F32), jax.ShapeDtypeStruct((t, d), F32)],
        scratch_shapes=[pltpu.VMEM((1, n_exp), F32)],
        compiler_params=_cparams(("arbitrary",)),
        name="route_shared",
    )(h1, shift, scale, wr_hi, wr_lo, rbias, wsg, wsu, wsd)


def _slots_kernel(idx_ref, rank_ref, pstart_ref, o_ref):
    tm = TOK_TILE
    n_exp = pstart_ref.shape[1]
    idx = idx_ref[...]
    lane = lax.broadcasted_iota(I32, (tm, n_exp), 1)
    out_lane = lax.broadcasted_iota(I32, (tm, LANES), 1)
    start = jnp.zeros((tm, LANES), F32)
    for k in range(TOP_K):
        hit = lane == idx[:, k:k + 1]
        sk = jnp.sum(jnp.where(hit, pstart_ref[...], 0.0), axis=-1, keepdims=True)
        start = jnp.where(out_lane == k, sk, start)
    o_ref[...] = start.astype(I32) + rank_ref[...]


def _slots(idx, rank, pstart_f):
    t = idx.shape[0]
    tok = pl.BlockSpec((TOK_TILE, LANES), lambda i: (i, 0))
    return pl.pallas_call(
        _slots_kernel,
        grid=(t // TOK_TILE,),
        in_specs=[tok, tok, pl.BlockSpec(pstart_f.shape, lambda i: (0, 0))],
        out_specs=tok,
        out_shape=jax.ShapeDtypeStruct((t, LANES), I32),
        compiler_params=_cparams(("arbitrary",)),
        name="slots",
    )(idx, rank, pstart_f)


def _dispatch_kernel(slot_ref, x_ref, xs_hbm, sem):
    def body(i, carry):
        src = x_ref.at[pl.ds(pl.multiple_of(i * ROW_TILES, ROW_TILES), ROW_TILES)]
        for k in range(TOP_K):
            pltpu.make_async_copy(src, xs_hbm.at[slot_ref[i * TOP_K + k]], sem).start()
        return carry

    lax.fori_loop(0, DSP_TILE, body, 0)
    done = xs_hbm.at[pl.ds(0, DSP_TILE * TOP_K)]
    pltpu.make_async_copy(done, done, sem).wait()


def _dispatch(x2, slots, n_slots):
    t = x2.shape[0] // ROW_TILES
    return pl.pallas_call(
        _dispatch_kernel,
        grid=(t // DSP_TILE,),
        in_specs=[pl.BlockSpec((DSP_TILE * TOP_K,), lambda i: (i,), memory_space=pltpu.SMEM),
                  pl.BlockSpec((DSP_TILE * ROW_TILES, LANES), lambda i: (i, 0))],
        out_specs=pl.BlockSpec(memory_space=pl.ANY),
        out_shape=jax.ShapeDtypeStruct((n_slots, ROW_TILES, LANES), F32),
        scratch_shapes=[pltpu.SemaphoreType.DMA(())],
        compiler_params=_cparams(("arbitrary",)),
        name="dispatch",
    )(slots, x2)


def _expert_kernel(gstart_ref, nblk_ref, ntot_ref, xs_hbm, wg_ref, wu_ref, wd_ref, ys_hbm,
                   xbuf, ybuf, wg_b, wu_b, wd_b, sem_in, sem_out):
    e = pl.program_id(0)
    nb = nblk_ref[e]
    g0 = gstart_ref[e]
    total = ntot_ref[0]
    blk_rows = MOE_BLOCK * ROW_TILES
    ahead = EXP_BUFS - 1

    def rows_of(g):
        return pl.ds(pl.multiple_of(g * blk_rows, blk_rows), blk_rows)

    def in_copy(g):
        s = g % EXP_BUFS
        return pltpu.make_async_copy(xs_hbm.at[rows_of(g)], xbuf.at[s], sem_in.at[s])

    def out_copy(g):
        s = g % EXP_BUFS
        return pltpu.make_async_copy(ybuf.at[s], ys_hbm.at[rows_of(g)], sem_out.at[s])

    @pl.when(e == 0)
    def _():
        for g in range(ahead):
            @pl.when(g < total)
            def _():
                in_copy(g).start()

    @pl.when(nb > 0)
    def _():
        wg_b[...] = wg_ref[0, 0].astype(BF16)
        wu_b[...] = wu_ref[0, 0].astype(BF16)
        wd_b[...] = wd_ref[0, 0].astype(BF16)

    def body(b, carry):
        g = g0 + b
        s = g % EXP_BUFS

        @pl.when(g + ahead < total)
        def _():
            in_copy(g + ahead).start()

        in_copy(g).wait()

        @pl.when(g >= EXP_BUFS)
        def _():
            out_copy(g - EXP_BUFS).wait()

        x = jnp.concatenate([xbuf[s, pl.ds(j, MOE_BLOCK, stride=ROW_TILES), :] for j in range(ROW_TILES)], axis=-1)
        xb = x.astype(BF16)
        hid = jax.nn.silu(_dot(xb, wg_b[...])) * _dot(xb, wu_b[...])
        y = _dot(hid.astype(BF16), wd_b[...])
        for j in range(ROW_TILES):
            ybuf[s, pl.ds(j, MOE_BLOCK, stride=ROW_TILES), :] = y[:, j * LANES:(j + 1) * LANES]
        out_copy(g).start()
        return carry

    lax.fori_loop(0, nb, body, 0)

    @pl.when(e == pl.num_programs(0) - 1)
    def _():
        for back in range(EXP_BUFS):
            @pl.when(total - 1 - back >= 0)
            def _():
                out_copy(total - 1 - back).wait()


def _experts(xs2, gstart, nblk, ntot, layer, w_gate, w_up, w_down):
    n_rows = xs2.shape[0]
    _, n_exp, d, hdim = w_gate.shape
    blk_rows = MOE_BLOCK * ROW_TILES
    wmap = lambda e, gs, nb, nt: (layer, e, 0, 0)
    return pl.pallas_call(
        _expert_kernel,
        grid_spec=pltpu.PrefetchScalarGridSpec(
            num_scalar_prefetch=3,
            grid=(n_exp,),
            in_specs=[pl.BlockSpec(memory_space=pl.ANY),
                      pl.BlockSpec((1, 1, d, hdim), wmap),
                      pl.BlockSpec((1, 1, d, hdim), wmap),
                      pl.BlockSpec((1, 1, hdim, d), wmap)],
            out_specs=pl.BlockSpec(memory_space=pl.ANY),
            scratch_shapes=[pltpu.VMEM((EXP_BUFS, blk_rows, LANES), F32), pltpu.VMEM((EXP_BUFS, blk_rows, LANES), F32),
                            pltpu.VMEM((d, hdim), BF16), pltpu.VMEM((d, hdim), BF16), pltpu.VMEM((hdim, d), BF16),
                            pltpu.SemaphoreType.DMA((EXP_BUFS,)), pltpu.SemaphoreType.DMA((EXP_BUFS,))],
        ),
        out_shape=jax.ShapeDtypeStruct((n_rows, LANES), F32),
        compiler_params=_cparams(("arbitrary",)),
        name="experts",
    )(gstart, nblk, ntot, xs2, w_gate, w_up, w_down)


def _combine_kernel(slot_cur, slot_nxt, ys_hbm, wt_ref, shared_ref, h_ref, gt_ref, lng_ref, lnb_ref, o_ref, buf, mixed, sem,
                    *, alpha):
    i = pl.program_id(0)
    n = pl.num_programs(0)
    tt = CMB_TILE

    def issue(slot_ref, b):
        def body(tok, carry):
            for k in range(TOP_K):
                r = tok * TOP_K + k
                dst = buf.at[b, pl.ds(pl.multiple_of(r * ROW_TILES, ROW_TILES), ROW_TILES)]
                pltpu.make_async_copy(ys_hbm.at[slot_ref[r]], dst, sem.at[b]).start()
            return carry
        lax.fori_loop(0, tt, body, 0)

    @pl.when(i == 0)
    def _():
        issue(slot_cur, 0)

    @pl.when(i + 1 < n)
    def _():
        issue(slot_nxt, (i + 1) % 2)

    cur = i % 2
    pltpu.make_async_copy(buf.at[cur], buf.at[cur], sem.at[cur]).wait()

    unroll = 4

    def mix_rows(step, carry):
        for u in range(unroll):
            tok = step * unroll + u
            first = tok * TOP_K
            acc = None
            for k in range(TOP_K):
                row = buf[cur, pl.ds(pl.multiple_of((first + k) * ROW_TILES, ROW_TILES), ROW_TILES), :]
                term = wt_ref[first + k] * row
                acc = term if acc is None else acc + term
            mixed[pl.ds(pl.multiple_of(tok * ROW_TILES, ROW_TILES), ROW_TILES), :] = acc
        return carry

    lax.fori_loop(0, tt // unroll, mix_rows, 0)
    routed = jnp.concatenate([mixed[pl.ds(j, tt, stride=ROW_TILES), :] for j in range(ROW_TILES)], axis=-1)
    f = shared_ref[...] + routed
    v = alpha * h_ref[...] + gt_ref[0] * f
    o_ref[...] = _layer_norm(v, lng_ref[...], lnb_ref[...])


def _combine(ys3, slots, wts, shared, h1, gate2, ln_g, ln_b, group_of_tile, alpha):
    t, d = h1.shape
    tt = CMB_TILE
    n = t // tt
    scale = TOK_TILE // tt
    full = lambda a: pl.BlockSpec(a.shape, lambda i: (0,) * a.ndim)
    return pl.pallas_call(
        functools.partial(_combine_kernel, alpha=alpha),
        grid=(n,),
        in_specs=[pl.BlockSpec((tt * TOP_K,), lambda i: (i,), memory_space=pltpu.SMEM),
                  pl.BlockSpec((tt * TOP_K,), lambda i: (jnp.minimum(i + 1, n - 1),), memory_space=pltpu.SMEM),
                  pl.BlockSpec(memory_space=pl.ANY),
                  pl.BlockSpec((tt * TOP_K,), lambda i: (i,), memory_space=pltpu.SMEM),
                  pl.BlockSpec((tt, d), lambda i: (i, 0)),
                  pl.BlockSpec((tt, d), lambda i: (i, 0)),
                  pl.BlockSpec((1, 1, d), lambda i: (group_of_tile(i // scale), 0, 0)),
                  full(ln_g), full(ln_b)],
        out_specs=pl.BlockSpec((tt, d), lambda i: (i, 0)),
        out_shape=jax.ShapeDtypeStruct((t, d), F32),
        scratch_shapes=[pltpu.VMEM((2, tt * TOP_K * ROW_TILES, LANES), F32), pltpu.VMEM((tt * ROW_TILES, LANES), F32),
                        pltpu.SemaphoreType.DMA((2,))],
        compiler_params=_cparams(("arbitrary",)),
        name="combine_ln2",
    )(slots, slots, ys3, wts, shared, h1, gate2, ln_g, ln_b)


def _reorder_w_in(w):
    d = w.shape[0]
    o = 0
    k = w[:, o:o + GLA_KEY]; o += GLA_KEY
    v = w[:, o:o + GLA_VAL]; o += GLA_VAL
    dec = w[:, o:o + 2 * GLA_LOWRANK]; o += 2 * GLA_LOWRANK
    q = w[:, o:o + GLA_KEY]; o += GLA_KEY
    g = w[:, o:o + GLA_VAL]; o += GLA_VAL
    conv = w[:, o:o + 3 * CONV_DIM]
    pad = jnp.zeros((d, LANES - 2 * GLA_LOWRANK), w.dtype)
    return jnp.concatenate([v, g, k, q, conv, dec, pad], axis=1).astype(BF16)


def _decay_weights(w_up, b_dec, pair):
    lo = pair * 2 * GLA_DK
    outs = []
    for d in range(2):
        w = jnp.zeros((LANES, LANES), F32).at[d * GLA_LOWRANK:(d + 1) * GLA_LOWRANK, :].set(w_up[d][:, lo:lo + LANES])
        outs.append((w, b_dec[d][lo:lo + LANES].reshape(1, LANES)))
    return outs[0][0], outs[1][0], outs[0][1], outs[1][1]


def _to_scan_order(a, ctx_len, rows):
    b, _, f = a.shape
    lat = a[:, ctx_len:].reshape(b, rows, GRID_W, f).swapaxes(1, 2).reshape(b, rows * GRID_W, f)
    return jnp.concatenate([a[:, :ctx_len], lat], axis=1)


def _from_scan_order(a, ctx_len, rows):
    b, _, f = a.shape
    lat = a[:, ctx_len:].reshape(b, GRID_W, rows, f).swapaxes(1, 2).reshape(b, rows * GRID_W, f)
    return jnp.concatenate([a[:, :ctx_len], lat], axis=1)


def kernel(x, c, ctx, c_ctx, w_mod, b_mod, w_in, w_conv, w_decay_up, b_decay, g_gla_norm, w_out, ln1_g, ln1_b, w_router, router_bias, w_e_gate, w_e_up, w_e_down, w_s_gate, w_s_up, w_s_down, ln2_g, ln2_b):
    bsz, seq, d = x.shape
    ctx_len = ctx.shape[1]
    depth = w_mod.shape[0]
    n_exp = w_router.shape[2]
    rows = seq // GRID_W
    l = ctx_len + seq
    t = bsz * l
    assert ctx_len == TOK_TILE and seq % TOK_TILE == 0 and d == ROW_TILES * LANES
    assert t % DSP_TILE == 0 and bsz + 1 <= SUBLANES
    tiles_per_batch = l // TOK_TILE
    alpha = float((2 * depth) ** 0.25)

    def group_of_tile(i):
        return jnp.where(i % tiles_per_batch == 0, bsz, i // tiles_per_batch)

    cond8 = jnp.zeros((SUBLANES, d), F32).at[:bsz].set(c).at[bsz].set(c_ctx)
    mod = _modulation(cond8, w_mod, b_mod).reshape(depth, SUBLANES, 6, 1, d)

    nk = t * TOP_K
    n_blocks = -(-(nk + n_exp * (MOE_BLOCK - 1)) // MOE_BLOCK)
    n_slots = n_blocks * MOE_BLOCK

    h = jnp.concatenate([ctx, x], axis=1).reshape(t, d)
    for i in range(depth):
        m = [mod[i, :, j] for j in range(6)]
        p = _inproj(h, m[0], m[1], _reorder_w_in(w_in[i]), group_of_tile)
        p3 = p.reshape(bsz, l, P_COLS)

        wf, wb, bf, bb = _decay_weights(w_decay_up[i], b_decay[i], 0)
        o_fr, o_br = _gla_pair(p3, P_K // LANES, P_Q // LANES, P_V // (2 * GLA_DV), P_DEC // LANES, wf, wb, bf, bb)
        col_in = jnp.concatenate([p3[..., P_V + 2 * GLA_DV:P_V + 4 * GLA_DV], p3[..., P_K + LANES:P_K + 2 * LANES],
                                  p3[..., P_Q + LANES:P_Q + 2 * LANES], p3[..., P_DEC:P_DEC + LANES]], axis=-1)
        col_in = _to_scan_order(col_in, ctx_len, rows)
        wf, wb, bf, bb = _decay_weights(w_decay_up[i], b_decay[i], 1)
        o_fc, o_bc = _gla_pair(col_in, 2, 3, 0, 4, wf, wb, bf, bb)
        o_fc = _from_scan_order(o_fc, ctx_len, rows)
        o_bc = _from_scan_order(o_bc, ctx_len, rows)

        flat = lambda a: a.reshape(t, a.shape[-1])
        h1 = _outproj(p, flat(o_fr), flat(o_br), flat(o_fc), flat(o_bc), h, m[2], w_conv[i],
                      g_gla_norm[i].reshape(1, GLA_DV), w_out[i].astype(BF16), ln1_g[i].reshape(1, d), ln1_b[i].reshape(1, d),
                      group_of_tile, alpha, tiles_per_batch)

        wr = w_router[i]
        wr_hi = wr.astype(BF16)
        wr_lo = (wr - wr_hi.astype(F32)).astype(BF16)
        x3, idx, wts, rank, counts, shared = _route(
            h1, m[3], m[4], wr_hi, wr_lo, router_bias[i].reshape(1, n_exp),
            w_s_gate[i].astype(BF16), w_s_up[i].astype(BF16), w_s_down[i].astype(BF16), group_of_tile)

        cnt = counts[0].astype(I32)
        padded = (cnt + MOE_BLOCK - 1) // MOE_BLOCK * MOE_BLOCK
        pend = jnp.cumsum(padded)
        pstart = pend - padded
        slots = _slots(idx, rank, pstart.astype(F32).reshape(1, n_exp))[:, :TOP_K].reshape(nk)

        xs3 = _dispatch(x3, slots, n_slots)
        ys2 = _experts(xs3.reshape(n_slots * ROW_TILES, LANES), pstart // MOE_BLOCK, padded // MOE_BLOCK,
                       (pend[-1] // MOE_BLOCK).reshape(1), i, w_e_gate, w_e_up, w_e_down)
        h = _combine(ys2.reshape(n_slots, ROW_TILES, LANES), slots, wts[:, :TOP_K].reshape(nk), shared, h1, m[5],
                     ln2_g[i].reshape(1, d), ln2_b[i].reshape(1, d), group_of_tile, alpha)

    return h.reshape(bsz, l, d)[:, ctx_len:]
```

```python
import functools

import jax
import jax.numpy as jnp
from jax import lax
from jax.experimental import pallas as pl
from jax.experimental.pallas import tpu as pltpu

F32 = jnp.float32
BF16 = jnp.bfloat16
I32 = jnp.int32

GRID_W = 64
CONV_DIM = 512
GLA_HEADS = 4
GLA_DK = 64
GLA_DV = 128
GLA_KEY = GLA_HEADS * GLA_DK
GLA_VAL = GLA_HEADS * GLA_DV
GLA_LOWRANK = 16
GLA_GATE_TEMP = 16.0
GLA_CHUNK = 64
TOP_K = 8
N_GROUPS = 8
TOPK_GROUPS = 4
ROUTED_SCALE = 2.5
LN_EPS = 1e-5
RMS_EPS = 1e-6

LANES = 128
SUBLANES = 8
VMEM_LIMIT = 56 * 1024 * 1024

P_V = 0
P_G = 512
P_K = 1024
P_Q = 1280
P_CONV = 1536
P_DEC = 3072
P_COLS = 3200

TOK_TILE = 256
MOE_BLOCK = 256
EXP_BUFS = 4
CMB_TILE = 128
DSP_TILE = 512
ROW_TILES = 8


def _cparams(sem):
    return pltpu.CompilerParams(dimension_semantics=sem, vmem_limit_bytes=VMEM_LIMIT)


def _mod_kernel(c_ref, w_ref, b_ref, o_ref):
    c = c_ref[...]
    a = c * jax.nn.sigmoid(c)
    o_ref[0] = jnp.dot(a, w_ref[0], preferred_element_type=F32, precision=lax.Precision.HIGHEST) + b_ref[0]


def _modulation(cond8, w_mod, b_mod):
    depth, d, n = w_mod.shape
    tn = 1536
    return pl.pallas_call(
        _mod_kernel,
        grid=(depth, n // tn),
        in_specs=[
            pl.BlockSpec((SUBLANES, d), lambda i, j: (0, 0)),
            pl.BlockSpec((1, d, tn), lambda i, j: (i, 0, j)),
            pl.BlockSpec((1, 1, tn), lambda i, j: (i, 0, j)),
        ],
        out_specs=pl.BlockSpec((1, SUBLANES, tn), lambda i, j: (i, 0, j)),
        out_shape=jax.ShapeDtypeStruct((depth, SUBLANES, n), F32),
        compiler_params=_cparams(("arbitrary", "arbitrary")),
        name="adaln_mod",
    )(cond8, w_mod, b_mod.reshape(depth, 1, n))


def _inproj_kernel(x_ref, sh_ref, sc_ref, w_ref, o_ref):
    h = x_ref[...] * (1.0 + sc_ref[0]) + sh_ref[0]
    o_ref[...] = jnp.dot(h.astype(BF16), w_ref[...], preferred_element_type=F32)


def _inproj(h, shift, scale, w, group_of_tile):
    t, d = h.shape
    n = w.shape[1]
    grp = lambda i: (group_of_tile(i), 0, 0)
    return pl.pallas_call(
        _inproj_kernel,
        grid=(t // TOK_TILE,),
        in_specs=[
            pl.BlockSpec((TOK_TILE, d), lambda i: (i, 0)),
            pl.BlockSpec((1, 1, d), grp),
            pl.BlockSpec((1, 1, d), grp),
            pl.BlockSpec((d, n), lambda i: (0, 0)),
        ],
        out_specs=pl.BlockSpec((TOK_TILE, n), lambda i: (i, 0)),
        out_shape=jax.ShapeDtypeStruct((t, n), F32),
        compiler_params=_cparams(("arbitrary",)),
        name="inproj",
    )(h, shift, scale, w)


def _dot(a, b):
    return jnp.dot(a, b, preferred_element_type=F32)


def _dot_nt(a, b):
    return lax.dot_general(a, b, (((1,), (1,)), ((), ())), preferred_element_type=F32)


def _gla_direction(k2, q2, v2, dec, wup, bdec, sums, tri, s_ref, o_ref, reverse):
    t = TOK_TILE
    c = GLA_CHUNK
    d_hi = dec.astype(BF16).astype(F32)
    packed = d_hi + pltpu.roll(dec - d_hi, 2 * GLA_LOWRANK, 1) + pltpu.roll(d_hi, 4 * GLA_LOWRANK, 1)
    z = _dot(packed.astype(BF16), wup) + bdec
    la = (jnp.minimum(z, 0.0) - jnp.log(1.0 + jnp.exp(-jnp.abs(z)))) * (1.0 / GLA_GATE_TEMP)

    la_hi = la.astype(BF16)
    la_lo = (la - la_hi.astype(F32)).astype(BF16)
    acc = _dot(sums, jnp.concatenate([la_hi, la_lo], axis=-1))
    bcum = acc[:t, :LANES] + acc[:t, LANES:]
    btot = acc[t:, :LANES] + acc[t:, LANES:]

    q_dec = q2 * jnp.exp(bcum) * (GLA_DK ** -0.5)
    k_inv = (k2 * jnp.exp(-bcum)).astype(BF16)
    k_tail = k2 * jnp.exp(btot - bcum)
    chunk_decay = jnp.exp(btot)

    lane = lax.broadcasted_iota(I32, (t, 2 * GLA_DK), 1)
    keep = tri > 0.0
    order = range(t // c - 1, -1, -1) if reverse else range(t // c)
    for hh in range(2):
        in_head = (lane // GLA_DK) == hh
        qd = jnp.where(in_head, q_dec, 0.0).astype(BF16)
        kt = jnp.where(in_head, k_tail, 0.0).astype(BF16)
        vh = v2[:, hh * GLA_DV:(hh + 1) * GLA_DV]
        vh_b = vh.astype(BF16)
        att = jnp.where(keep, _dot_nt(qd, k_inv), 0.0).astype(BF16)
        o_intra = _dot(att, vh_b)
        s = s_ref[hh]
        for ci in order:
            rows = slice(ci * c, (ci + 1) * c)
            o_ref[0, rows, hh * GLA_DV:(hh + 1) * GLA_DV] = o_intra[rows] + _dot_nt(qd[rows], s.astype(BF16))
            kv_t = _dot(vh[rows].T.astype(BF16), kt[rows])
            s = s * chunk_decay[ci * c:ci * c + 1, :] + kv_t
        s_ref[hh] = s


def _gla_kernel(kf, qf, vf, df, kb, qb, vb, db, wupf, wupb, bf, bb, sums_f, sums_b, tri_f, tri_b,
                of_ref, ob_ref, sf_ref, sb_ref):
    @pl.when(pl.program_id(1) == 0)
    def _():
        sf_ref[...] = jnp.zeros_like(sf_ref)
        sb_ref[...] = jnp.zeros_like(sb_ref)

    _gla_direction(kf[0], qf[0], vf[0], df[0], wupf[...], bf[...], sums_f[...], tri_f[...], sf_ref, of_ref, False)
    _gla_direction(kb[0], qb[0], vb[0], db[0], wupb[...], bb[...], sums_b[...], tri_b[...], sb_ref, ob_ref, True)


def _chunk_matrices():
    t = TOK_TILE
    r = jnp.arange(t)[:, None]
    q = jnp.arange(t)[None, :]
    same = (r // GLA_CHUNK) == (q // GLA_CHUNK)
    out = []
    for tri in (same & (q <= r), same & (q >= r)):
        out.append((jnp.concatenate([tri, same], axis=0).astype(BF16), tri.astype(F32)))
    return out[0][0], out[1][0], out[0][1], out[1][1]


def _gla_pair(src, col_k, col_q, col_v, col_dec, wupf, wupb, bf, bb):
    bsz, l, _ = src.shape
    nb = l // TOK_TILE
    fwd = lambda j: j
    bwd = lambda j: jnp.where(j == 0, 0, nb - j)

    def spec(width, colidx, order):
        return pl.BlockSpec((1, TOK_TILE, width), lambda b, j: (b, order(j), colidx))

    full = lambda a: pl.BlockSpec(a.shape, lambda b, j: (0,) * a.ndim)
    consts = _chunk_matrices()
    out = jax.ShapeDtypeStruct((bsz, l, 2 * GLA_DV), F32)
    return pl.pallas_call(
        _gla_kernel,
        grid=(bsz, nb),
        in_specs=[spec(LANES, col_k, fwd), spec(LANES, col_q, fwd), spec(2 * GLA_DV, col_v, fwd), spec(LANES, col_dec, fwd),
                  spec(LANES, col_k, bwd), spec(LANES, col_q, bwd), spec(2 * GLA_DV, col_v, bwd), spec(LANES, col_dec, bwd),
                  full(wupf), full(wupb), full(bf), full(bb)] + [full(a) for a in consts],
        out_specs=[pl.BlockSpec((1, TOK_TILE, 2 * GLA_DV), lambda b, j: (b, fwd(j), 0)),
                   pl.BlockSpec((1, TOK_TILE, 2 * GLA_DV), lambda b, j: (b, bwd(j), 0))],
        out_shape=[out, out],
        scratch_shapes=[pltpu.VMEM((2, GLA_DV, LANES), F32), pltpu.VMEM((2, GLA_DV, LANES), F32)],
        compiler_params=_cparams(("arbitrary", "arbitrary")),
        name="gla_pair",
    )(src, src, src, src, src, src, src, src, wupf, wupb, bf, bb, *consts)


def _layer_norm(v, g, b):
    mu = jnp.mean(v, axis=-1, keepdims=True)
    dv = v - mu
    var = jnp.mean(dv * dv, axis=-1, keepdims=True)
    return dv * lax.rsqrt(var + LN_EPS) * g + b


def _outproj_kernel(conv_ref, gate_ref, ofr, obr, ofc, obc, h_ref, gt_ref, wconv_ref, gn_ref, wout_ref,
                    lng_ref, lnb_ref, o_ref, *, alpha, tiles_per_batch):
    tm = TOK_TILE
    is_ctx = (pl.program_id(0) % tiles_per_batch) == 0
    seg = jnp.where(is_ctx, tm, GRID_W)

    cv = conv_ref[...]
    b_gate = cv[:, :CONV_DIM]
    u = cv[:, CONV_DIM:2 * CONV_DIM] * cv[:, 2 * CONV_DIM:]
    pos = lax.broadcasted_iota(I32, (tm, CONV_DIM), 0) & (seg - 1)
    prev = jnp.where(pos == 0, 0.0, pltpu.roll(u, 1, 0))
    nxt = jnp.where(pos == seg - 1, 0.0, pltpu.roll(u, tm - 1, 0))
    wc = wconv_ref[...]
    mix_conv = b_gate * (wc[0:1, :] * prev + wc[1:2, :] * u + wc[2:3, :] * nxt)

    o_r = ofr[...] + obr[...]
    o_c = ofc[...] + obc[...]
    gate = gate_ref[...]
    gn = gn_ref[...]
    parts = [mix_conv.astype(BF16)]
    for hd in range(GLA_HEADS):
        src = o_r if hd < 2 else o_c
        o = src[:, (hd % 2) * GLA_DV:(hd % 2 + 1) * GLA_DV]
        o = o * lax.rsqrt(jnp.mean(o * o, axis=-1, keepdims=True) + RMS_EPS) * gn
        g = gate[:, hd * GLA_DV:(hd + 1) * GLA_DV]
        parts.append((o * (g * jax.nn.sigmoid(g))).astype(BF16))
    mix = jnp.concatenate(parts, axis=-1)
    y = _dot(mix, wout_ref[...])
    v = alpha * h_ref[...] + gt_ref[0] * y
    o_ref[...] = _layer_norm(v, lng_ref[...], lnb_ref[...])


def _outproj(p, o_fr, o_br, o_fc, o_bc, h, gate1, w_conv, g_norm, w_out, ln_g, ln_b, group_of_tile, alpha, tiles_per_batch):
    t, d = h.shape
    tm = TOK_TILE
    tok = lambda width, colidx: pl.BlockSpec((tm, width), lambda i: (i, colidx))
    full = lambda a: pl.BlockSpec(a.shape, lambda i: (0,) * a.ndim)
    return pl.pallas_call(
        functools.partial(_outproj_kernel, alpha=alpha, tiles_per_batch=tiles_per_batch),
        grid=(t // tm,),
        in_specs=[tok(3 * CONV_DIM, P_CONV // (3 * CONV_DIM)), tok(GLA_VAL, P_G // GLA_VAL),
                  tok(2 * GLA_DV, 0), tok(2 * GLA_DV, 0), tok(2 * GLA_DV, 0), tok(2 * GLA_DV, 0),
                  tok(d, 0), pl.BlockSpec((1, 1, d), lambda i: (group_of_tile(i), 0, 0)),
                  full(w_conv), full(g_norm), full(w_out), full(ln_g), full(ln_b)],
        out_specs=pl.BlockSpec((tm, d), lambda i: (i, 0)),
        out_shape=jax.ShapeDtypeStruct((t, d), F32),
        compiler_params=_cparams(("arbitrary",)),
        name="outproj_ln1",
    )(p, p, o_fr, o_br, o_fc, o_bc, h, gate1, w_conv, g_norm, w_out, ln_g, ln_b)


def _route_kernel(h_ref, sh_ref, sc_ref, wrt_hi_ref, wrt_lo_ref, rbt_ref, wsg_ref, wsu_ref, wsd_ref, upper_ref,
                  x3_ref, idx_ref, wt_ref, rank_ref, cnt_ref, shared_ref, base_ref):
    tm = TOK_TILE
    n_exp = rbt_ref.shape[0]
    gsz = n_exp // N_GROUPS

    @pl.when(pl.program_id(0) == 0)
    def _():
        base_ref[...] = jnp.zeros_like(base_ref)

    u = h_ref[...] * (1.0 + sc_ref[0]) + sh_ref[0]
    for j in range(ROW_TILES):
        x3_ref[pl.ds(j, tm, stride=ROW_TILES), :] = u[:, j * LANES:(j + 1) * LANES]

    u_hi = u.astype(BF16)
    u_lo = (u - u_hi.astype(F32)).astype(BF16)
    shared_ref[...] = _dot((jax.nn.silu(_dot(u_hi, wsg_ref[...])) * _dot(u_hi, wsu_ref[...])).astype(BF16), wsd_ref[...])

    wrt_hi = wrt_hi_ref[...]
    logits = _dot_nt(wrt_hi, u_hi) + _dot_nt(wrt_hi, u_lo) + _dot_nt(wrt_lo_ref[...], u_hi)
    s = jax.nn.sigmoid(logits)
    biased = s + rbt_ref[...]

    neg = -jnp.inf
    eidx = lax.broadcasted_iota(I32, (n_exp, tm), 0)
    loc = lax.broadcasted_iota(I32, (gsz, tm), 0)
    gscore = []
    for g in range(N_GROUPS):
        vals = biased[g * gsz:(g + 1) * gsz, :]
        m1 = jnp.max(vals, axis=0, keepdims=True)
        i1 = jnp.min(jnp.where(vals == m1, loc, gsz), axis=0, keepdims=True)
        m2 = jnp.max(jnp.where(loc == i1, neg, vals), axis=0, keepdims=True)
        gscore.append(m1 + m2)
    pieces = []
    for g in range(N_GROUPS):
        ahead = jnp.zeros((1, tm), F32)
        for g2 in range(N_GROUPS):
            if g2 == g:
                continue
            wins = (gscore[g2] >= gscore[g]) if g2 < g else (gscore[g2] > gscore[g])
            ahead = ahead + jnp.where(wins, 1.0, 0.0)
        pieces.append(jnp.where(ahead < TOPK_GROUPS, biased[g * gsz:(g + 1) * gsz, :], neg))
    masked = jnp.concatenate(pieces, axis=0)

    onehot = jnp.zeros((n_exp, tm), F32)
    idxs = []
    for _ in range(TOP_K):
        m = jnp.max(masked, axis=0, keepdims=True)
        i = jnp.min(jnp.where(masked == m, eidx, n_exp), axis=0, keepdims=True)
        hit = eidx == i
        onehot = jnp.where(hit, 1.0, onehot)
        masked = jnp.where(hit, neg, masked)
        idxs.append(i)
    sel = onehot * s
    wnorm = sel / jnp.sum(sel, axis=0, keepdims=True) * ROUTED_SCALE

    onehot_b = onehot.astype(BF16)
    upper = upper_ref[...]
    rank_all = _dot(onehot_b, upper) + base_ref[...]
    base_new = base_ref[...] + _dot(onehot_b, jnp.ones((tm, tm), BF16))
    base_ref[...] = base_new
    cnt_ref[...] = base_new[:, :LANES]

    wts, ranks = [], []
    for k in range(TOP_K):
        hit = eidx == idxs[k]
        wts.append(jnp.sum(jnp.where(hit, wnorm, 0.0), axis=0, keepdims=True))
        ranks.append(jnp.sum(jnp.where(hit, rank_all, 0.0), axis=0, keepdims=True))
    idx_ref[...] = jnp.concatenate(idxs, axis=0)
    wt_ref[...] = jnp.concatenate(wts, axis=0)
    rank_ref[...] = jnp.concatenate(ranks, axis=0).astype(I32)


def _route(h1, shift, scale, wr, rbias, wsg, wsu, wsd, group_of_tile):
    t, d = h1.shape
    tm = TOK_TILE
    n_exp = wr.shape[1]
    wrt = wr.T
    wrt_hi = wrt.astype(BF16)
    wrt_lo = (wrt - wrt_hi.astype(F32)).astype(BF16)
    rbt = jnp.broadcast_to(rbias.reshape(n_exp, 1), (n_exp, tm))
    upper = (jnp.arange(tm)[:, None] < jnp.arange(tm)[None, :]).astype(BF16)
    grp = lambda i: (group_of_tile(i), 0, 0)
    full = lambda a: pl.BlockSpec(a.shape, lambda i: (0,) * a.ndim)
    choice = pl.BlockSpec((TOP_K, tm), lambda i: (0, i))
    choice_out = lambda dt: jax.ShapeDtypeStruct((TOP_K, t), dt)
    return pl.pallas_call(
        _route_kernel,
        grid=(t // tm,),
        in_specs=[pl.BlockSpec((tm, d), lambda i: (i, 0)), pl.BlockSpec((1, 1, d), grp), pl.BlockSpec((1, 1, d), grp),
                  full(wrt_hi), full(wrt_lo), full(rbt), full(wsg), full(wsu), full(wsd), full(upper)],
        out_specs=[pl.BlockSpec((tm * ROW_TILES, LANES), lambda i: (i, 0)), choice, choice, choice,
                   pl.BlockSpec((n_exp, LANES), lambda i: (0, 0)),
                   pl.BlockSpec((tm, d), lambda i: (i, 0))],
        out_shape=[jax.ShapeDtypeStruct((t * ROW_TILES, LANES), F32), choice_out(I32), choice_out(F32), choice_out(I32),
                   jax.ShapeDtypeStruct((n_exp, LANES), F32), jax.ShapeDtypeStruct((t, d), F32)],
        scratch_shapes=[pltpu.VMEM((n_exp, tm), F32)],
        compiler_params=_cparams(("arbitrary",)),
        name="route_shared",
    )(h1, shift, scale, wrt_hi, wrt_lo, rbt, wsg, wsu, wsd, upper)


def _dispatch_kernel(idx_ref, rank_ref, pstart_ref, x_ref, xs_hbm, sem):
    def body(i, carry):
        src = x_ref.at[pl.ds(pl.multiple_of(i * ROW_TILES, ROW_TILES), ROW_TILES)]
        for k in range(TOP_K):
            r = i * TOP_K + k
            slot = pstart_ref[idx_ref[r]] + rank_ref[r]
            pltpu.make_async_copy(src, xs_hbm.at[slot], sem).start()
        return carry

    lax.fori_loop(0, DSP_TILE, body, 0)
    done = xs_hbm.at[pl.ds(0, DSP_TILE * TOP_K)]
    pltpu.make_async_copy(done, done, sem).wait()


def _dispatch(x2, idx, rank, pstart, n_slots):
    t = x2.shape[0] // ROW_TILES
    choice = pl.BlockSpec((DSP_TILE * TOP_K,), lambda i: (i,), memory_space=pltpu.SMEM)
    return pl.pallas_call(
        _dispatch_kernel,
        grid=(t // DSP_TILE,),
        in_specs=[choice, choice, pl.BlockSpec(memory_space=pltpu.SMEM),
                  pl.BlockSpec((DSP_TILE * ROW_TILES, LANES), lambda i: (i, 0))],
        out_specs=pl.BlockSpec(memory_space=pl.ANY),
        out_shape=jax.ShapeDtypeStruct((n_slots, ROW_TILES, LANES), F32),
        scratch_shapes=[pltpu.SemaphoreType.DMA(())],
        compiler_params=_cparams(("arbitrary",)),
        name="dispatch",
    )(idx, rank, pstart, x2)


def _expert_kernel(gstart_ref, nblk_ref, ntot_ref, xs_hbm, wg_ref, wu_ref, wd_ref, ys_hbm,
                   xbuf, ybuf, wg_b, wu_b, wd_b, sem_in, sem_out):
    e = pl.program_id(0)
    nb = nblk_ref[e]
    g0 = gstart_ref[e]
    total = ntot_ref[0]
    blk_rows = MOE_BLOCK * ROW_TILES
    ahead = EXP_BUFS - 1

    def rows_of(g):
        return pl.ds(pl.multiple_of(g * blk_rows, blk_rows), blk_rows)

    def in_copy(g):
        s = g % EXP_BUFS
        return pltpu.make_async_copy(xs_hbm.at[rows_of(g)], xbuf.at[s], sem_in.at[s])

    def out_copy(g):
        s = g % EXP_BUFS
        return pltpu.make_async_copy(ybuf.at[s], ys_hbm.at[rows_of(g)], sem_out.at[s])

    @pl.when(e == 0)
    def _():
        for g in range(ahead):
            @pl.when(g < total)
            def _():
                in_copy(g).start()

    @pl.when(nb > 0)
    def _():
        wg_b[...] = wg_ref[0, 0].astype(BF16)
        wu_b[...] = wu_ref[0, 0].astype(BF16)
        wd_b[...] = wd_ref[0, 0].astype(BF16)

    def body(b, carry):
        g = g0 + b
        s = g % EXP_BUFS

        @pl.when(g + ahead < total)
        def _():
            in_copy(g + ahead).start()

        in_copy(g).wait()

        @pl.when(g >= EXP_BUFS)
        def _():
            out_copy(g - EXP_BUFS).wait()

        x = jnp.concatenate([xbuf[s, pl.ds(j, MOE_BLOCK, stride=ROW_TILES), :] for j in range(ROW_TILES)], axis=-1)
        xb = x.astype(BF16)
        hid = jax.nn.silu(_dot(xb, wg_b[...])) * _dot(xb, wu_b[...])
        y = _dot(hid.astype(BF16), wd_b[...])
        for j in range(ROW_TILES):
            ybuf[s, pl.ds(j, MOE_BLOCK, stride=ROW_TILES), :] = y[:, j * LANES:(j + 1) * LANES]
        out_copy(g).start()
        return carry

    lax.fori_loop(0, nb, body, 0)

    @pl.when(e == pl.num_programs(0) - 1)
    def _():
        for back in range(EXP_BUFS):
            @pl.when(total - 1 - back >= 0)
            def _():
                out_copy(total - 1 - back).wait()


def _experts(xs2, gstart, nblk, ntot, layer, w_gate, w_up, w_down):
    n_rows = xs2.shape[0]
    _, n_exp, d, hdim = w_gate.shape
    blk_rows = MOE_BLOCK * ROW_TILES
    wmap = lambda e, gs, nb, nt: (layer, e, 0, 0)
    return pl.pallas_call(
        _expert_kernel,
        grid_spec=pltpu.PrefetchScalarGridSpec(
            num_scalar_prefetch=3,
            grid=(n_exp,),
            in_specs=[pl.BlockSpec(memory_space=pl.ANY),
                      pl.BlockSpec((1, 1, d, hdim), wmap),
                      pl.BlockSpec((1, 1, d, hdim), wmap),
                      pl.BlockSpec((1, 1, hdim, d), wmap)],
            out_specs=pl.BlockSpec(memory_space=pl.ANY),
            scratch_shapes=[pltpu.VMEM((EXP_BUFS, blk_rows, LANES), F32), pltpu.VMEM((EXP_BUFS, blk_rows, LANES), F32),
                            pltpu.VMEM((d, hdim), BF16), pltpu.VMEM((d, hdim), BF16), pltpu.VMEM((hdim, d), BF16),
                            pltpu.SemaphoreType.DMA((EXP_BUFS,)), pltpu.SemaphoreType.DMA((EXP_BUFS,))],
        ),
        out_shape=jax.ShapeDtypeStruct((n_rows, LANES), F32),
        compiler_params=_cparams(("arbitrary",)),
        name="experts",
    )(gstart, nblk, ntot, xs2, w_gate, w_up, w_down)


def _combine_kernel(idx_cur, rank_cur, idx_nxt, rank_nxt, pstart_ref, ys_hbm, wt_ref, shared_ref, h_ref, gt_ref,
                    lng_ref, lnb_ref, o_ref, buf, mixed, sem, *, alpha):
    i = pl.program_id(0)
    n = pl.num_programs(0)
    tt = CMB_TILE

    def issue(idx_ref, rank_ref, b):
        def body(tok, carry):
            for k in range(TOP_K):
                r = tok * TOP_K + k
                slot = pstart_ref[idx_ref[r]] + rank_ref[r]
                dst = buf.at[b, pl.ds(pl.multiple_of(r * ROW_TILES, ROW_TILES), ROW_TILES)]
                pltpu.make_async_copy(ys_hbm.at[slot], dst, sem.at[b]).start()
            return carry
        lax.fori_loop(0, tt, body, 0)

    @pl.when(i == 0)
    def _():
        issue(idx_cur, rank_cur, 0)

    @pl.when(i + 1 < n)
    def _():
        issue(idx_nxt, rank_nxt, (i + 1) % 2)

    cur = i % 2
    pltpu.make_async_copy(buf.at[cur], buf.at[cur], sem.at[cur]).wait()

    unroll = 4

    def mix_rows(step, carry):
        for u in range(unroll):
            tok = step * unroll + u
            first = tok * TOP_K
            acc = None
            for k in range(TOP_K):
                row = buf[cur, pl.ds(pl.multiple_of((first + k) * ROW_TILES, ROW_TILES), ROW_TILES), :]
                term = wt_ref[first + k] * row
                acc = term if acc is None else acc + term
            mixed[pl.ds(pl.multiple_of(tok * ROW_TILES, ROW_TILES), ROW_TILES), :] = acc
        return carry

    lax.fori_loop(0, tt // unroll, mix_rows, 0)
    routed = jnp.concatenate([mixed[pl.ds(j, tt, stride=ROW_TILES), :] for j in range(ROW_TILES)], axis=-1)
    f = shared_ref[...] + routed
    v = alpha * h_ref[...] + gt_ref[0] * f
    o_ref[...] = _layer_norm(v, lng_ref[...], lnb_ref[...])


def _combine(ys3, idx, rank, pstart, wts, shared, h1, gate2, ln_g, ln_b, group_of_tile, alpha):
    t, d = h1.shape
    tt = CMB_TILE
    n = t // tt
    scale = TOK_TILE // tt
    full = lambda a: pl.BlockSpec(a.shape, lambda i: (0,) * a.ndim)
    cur = pl.BlockSpec((tt * TOP_K,), lambda i: (i,), memory_space=pltpu.SMEM)
    nxt = pl.BlockSpec((tt * TOP_K,), lambda i: (jnp.minimum(i + 1, n - 1),), memory_space=pltpu.SMEM)
    return pl.pallas_call(
        functools.partial(_combine_kernel, alpha=alpha),
        grid=(n,),
        in_specs=[cur, cur, nxt, nxt, pl.BlockSpec(memory_space=pltpu.SMEM),
                  pl.BlockSpec(memory_space=pl.ANY), cur,
                  pl.BlockSpec((tt, d), lambda i: (i, 0)),
                  pl.BlockSpec((tt, d), lambda i: (i, 0)),
                  pl.BlockSpec((1, 1, d), lambda i: (group_of_tile(i // scale), 0, 0)),
                  full(ln_g), full(ln_b)],
        out_specs=pl.BlockSpec((tt, d), lambda i: (i, 0)),
        out_shape=jax.ShapeDtypeStruct((t, d), F32),
        scratch_shapes=[pltpu.VMEM((2, tt * TOP_K * ROW_TILES, LANES), F32), pltpu.VMEM((tt * ROW_TILES, LANES), F32),
                        pltpu.SemaphoreType.DMA((2,))],
        compiler_params=_cparams(("arbitrary",)),
        name="combine_ln2",
    )(idx, rank, idx, rank, pstart, ys3, wts, shared, h1, gate2, ln_g, ln_b)


def _reorder_w_in(w):
    d = w.shape[0]
    o = 0
    k = w[:, o:o + GLA_KEY]; o += GLA_KEY
    v = w[:, o:o + GLA_VAL]; o += GLA_VAL
    dec = w[:, o:o + 2 * GLA_LOWRANK]; o += 2 * GLA_LOWRANK
    q = w[:, o:o + GLA_KEY]; o += GLA_KEY
    g = w[:, o:o + GLA_VAL]; o += GLA_VAL
    conv = w[:, o:o + 3 * CONV_DIM]
    pad = jnp.zeros((d, LANES - 2 * GLA_LOWRANK), w.dtype)
    return jnp.concatenate([v, g, k, q, conv, dec, pad], axis=1).astype(BF16)


def _decay_weights(w_up, b_dec, pair):
    lo = pair * 2 * GLA_DK
    nr = 2 * GLA_LOWRANK
    outs = []
    for d in range(2):
        w = jnp.zeros((nr, LANES), F32).at[d * GLA_LOWRANK:(d + 1) * GLA_LOWRANK, :].set(w_up[d][:, lo:lo + LANES])
        w_hi = w.astype(BF16)
        w_lo = (w - w_hi.astype(F32)).astype(BF16)
        packed = jnp.concatenate([w_hi, w_hi, w_lo, jnp.zeros((LANES - 3 * nr, LANES), BF16)], axis=0)
        outs.append((packed, b_dec[d][lo:lo + LANES].reshape(1, LANES)))
    return outs[0][0], outs[1][0], outs[0][1], outs[1][1]


def _to_scan_order(a, ctx_len, rows):
    b, _, f = a.shape
    lat = a[:, ctx_len:].reshape(b, rows, GRID_W, f).swapaxes(1, 2).reshape(b, rows * GRID_W, f)
    return jnp.concatenate([a[:, :ctx_len], lat], axis=1)


def _from_scan_order(a, ctx_len, rows):
    b, _, f = a.shape
    lat = a[:, ctx_len:].reshape(b, GRID_W, rows, f).swapaxes(1, 2).reshape(b, rows * GRID_W, f)
    return jnp.concatenate([a[:, :ctx_len], lat], axis=1)


def kernel(x, c, ctx, c_ctx, w_mod, b_mod, w_in, w_conv, w_decay_up, b_decay, g_gla_norm, w_out, ln1_g, ln1_b, w_router, router_bias, w_e_gate, w_e_up, w_e_down, w_s_gate, w_s_up, w_s_down, ln2_g, ln2_b):
    bsz, seq, d = x.shape
    ctx_len = ctx.shape[1]
    depth = w_mod.shape[0]
    n_exp = w_router.shape[2]
    rows = seq // GRID_W
    l = ctx_len + seq
    t = bsz * l
    assert ctx_len == TOK_TILE and seq % TOK_TILE == 0 and d == ROW_TILES * LANES
    assert t % DSP_TILE == 0 and bsz + 1 <= SUBLANES
    tiles_per_batch = l // TOK_TILE
    alpha = float((2 * depth) ** 0.25)

    def group_of_tile(i):
        return jnp.where(i % tiles_per_batch == 0, bsz, i // tiles_per_batch)

    cond8 = jnp.zeros((SUBLANES, d), F32).at[:bsz].set(c).at[bsz].set(c_ctx)
    mod = _modulation(cond8, w_mod, b_mod).reshape(depth, SUBLANES, 6, 1, d)

    nk = t * TOP_K
    n_blocks = -(-(nk + n_exp * (MOE_BLOCK - 1)) // MOE_BLOCK)
    n_slots = n_blocks * MOE_BLOCK

    h = jnp.concatenate([ctx, x], axis=1).reshape(t, d)
    for i in range(depth):
        m = [mod[i, :, j] for j in range(6)]
        p = _inproj(h, m[0], m[1], _reorder_w_in(w_in[i]), group_of_tile)
        p3 = p.reshape(bsz, l, P_COLS)

        wf, wb, bf, bb = _decay_weights(w_decay_up[i], b_decay[i], 0)
        o_fr, o_br = _gla_pair(p3, P_K // LANES, P_Q // LANES, P_V // (2 * GLA_DV), P_DEC // LANES, wf, wb, bf, bb)
        col_in = jnp.concatenate([p3[..., P_V + 2 * GLA_DV:P_V + 4 * GLA_DV], p3[..., P_K + LANES:P_K + 2 * LANES],
                                  p3[..., P_Q + LANES:P_Q + 2 * LANES], p3[..., P_DEC:P_DEC + LANES]], axis=-1)
        col_in = _to_scan_order(col_in, ctx_len, rows)
        wf, wb, bf, bb = _decay_weights(w_decay_up[i], b_decay[i], 1)
        o_fc, o_bc = _gla_pair(col_in, 2, 3, 0, 4, wf, wb, bf, bb)
        o_fc = _from_scan_order(o_fc, ctx_len, rows)
        o_bc = _from_scan_order(o_bc, ctx_len, rows)

        flat = lambda a: a.reshape(t, a.shape[-1])
        h1 = _outproj(p, flat(o_fr), flat(o_br), flat(o_fc), flat(o_bc), h, m[2], w_conv[i],
                      g_gla_norm[i].reshape(1, GLA_DV), w_out[i].astype(BF16), ln1_g[i].reshape(1, d), ln1_b[i].reshape(1, d),
                      group_of_tile, alpha, tiles_per_batch)

        x3, idx, wts, rank, counts, shared = _route(
            h1, m[3], m[4], w_router[i], router_bias[i],
            w_s_gate[i].astype(BF16), w_s_up[i].astype(BF16), w_s_down[i].astype(BF16), group_of_tile)

        cnt = counts[:, 0].astype(I32)
        padded = (cnt + MOE_BLOCK - 1) // MOE_BLOCK * MOE_BLOCK
        pend = jnp.cumsum(padded)
        pstart = pend - padded
        idx, rank, wts = idx.T.reshape(nk), rank.T.reshape(nk), wts.T.reshape(nk)

        xs3 = _dispatch(x3, idx, rank, pstart, n_slots)
        ys2 = _experts(xs3.reshape(n_slots * ROW_TILES, LANES), pstart // MOE_BLOCK, padded // MOE_BLOCK,
                       (pend[-1] // MOE_BLOCK).reshape(1), i, w_e_gate, w_e_up, w_e_down)
        h = _combine(ys2.reshape(n_slots, ROW_TILES, LANES), idx, rank, pstart, wts, shared, h1,
                     m[5], ln2_g[i].reshape(1, d), ln2_b[i].reshape(1, d), group_of_tile, alpha)

    return h.reshape(bsz, l, d)[:, ctx_len:]
```

```python
import functools

import jax
import jax.numpy as jnp
from jax import lax
from jax.experimental import pallas as pl
from jax.experimental.pallas import tpu as pltpu

F32 = jnp.float32
BF16 = jnp.bfloat16
I32 = jnp.int32

GRID_W = 64
CONV_DIM = 512
GLA_HEADS = 4
GLA_DK = 64
GLA_DV = 128
GLA_KEY = GLA_HEADS * GLA_DK
GLA_VAL = GLA_HEADS * GLA_DV
GLA_LOWRANK = 16
GLA_GATE_TEMP = 16.0
GLA_CHUNK = 64
TOP_K = 8
N_GROUPS = 8
TOPK_GROUPS = 4
ROUTED_SCALE = 2.5
LN_EPS = 1e-5
RMS_EPS = 1e-6

LANES = 128
SUBLANES = 8
VMEM_LIMIT = 56 * 1024 * 1024

P_CONV = 0
P_G = 1536
P_ROW = 2048
P_COL = 2560
P_DEC = 3072
P_COLS = 3200
PAIR_V, PAIR_K, PAIR_Q = 0, 256, 384

TOK_TILE = 256
MOE_BLOCK = 256
EXP_BUFS = 4
CMB_TILE = 256
DSP_TILE = 512
ROW_TILES = 8


def _cparams(sem):
    return pltpu.CompilerParams(dimension_semantics=sem, vmem_limit_bytes=VMEM_LIMIT)


def _mod_kernel(c_ref, w_ref, b_ref, o_ref):
    c = c_ref[...]
    a = c * jax.nn.sigmoid(c)
    o_ref[0] = jnp.dot(a, w_ref[0], preferred_element_type=F32, precision=lax.Precision.HIGHEST) + b_ref[0]


def _modulation(cond8, w_mod, b_mod):
    depth, d, n = w_mod.shape
    tn = 1536
    return pl.pallas_call(
        _mod_kernel,
        grid=(depth, n // tn),
        in_specs=[
            pl.BlockSpec((SUBLANES, d), lambda i, j: (0, 0)),
            pl.BlockSpec((1, d, tn), lambda i, j: (i, 0, j)),
            pl.BlockSpec((1, 1, tn), lambda i, j: (i, 0, j)),
        ],
        out_specs=pl.BlockSpec((1, SUBLANES, tn), lambda i, j: (i, 0, j)),
        out_shape=jax.ShapeDtypeStruct((depth, SUBLANES, n), F32),
        compiler_params=_cparams(("arbitrary", "arbitrary")),
        name="adaln_mod",
    )(cond8, w_mod, b_mod.reshape(depth, 1, n))


def _inproj_kernel(x_ref, sh_ref, sc_ref, w_ref, o_ref):
    h = x_ref[...] * (1.0 + sc_ref[0]) + sh_ref[0]
    o_ref[...] = jnp.dot(h.astype(BF16), w_ref[...], preferred_element_type=F32)


def _inproj(h, shift, scale, w, group_of_tile):
    t, d = h.shape
    n = w.shape[1]
    grp = lambda i: (group_of_tile(i), 0, 0)
    return pl.pallas_call(
        _inproj_kernel,
        grid=(t // TOK_TILE,),
        in_specs=[
            pl.BlockSpec((TOK_TILE, d), lambda i: (i, 0)),
            pl.BlockSpec((1, 1, d), grp),
            pl.BlockSpec((1, 1, d), grp),
            pl.BlockSpec((d, n), lambda i: (0, 0)),
        ],
        out_specs=pl.BlockSpec((TOK_TILE, n), lambda i: (i, 0)),
        out_shape=jax.ShapeDtypeStruct((t, n), F32),
        compiler_params=_cparams(("arbitrary",)),
        name="inproj",
    )(h, shift, scale, w)


def _dot(a, b):
    return jnp.dot(a, b, preferred_element_type=F32)


def _dot_nt(a, b):
    return lax.dot_general(a, b, (((1,), (1,)), ((), ())), preferred_element_type=F32)


def _gla_direction(k2, q2, v2, dec, wup, bdec, sums, tri, s_ref, o_ref, reverse):
    t = TOK_TILE
    c = GLA_CHUNK
    d_hi = dec.astype(BF16).astype(F32)
    packed = d_hi + pltpu.roll(dec - d_hi, 2 * GLA_LOWRANK, 1) + pltpu.roll(d_hi, 4 * GLA_LOWRANK, 1)
    z = _dot(packed.astype(BF16), wup) + bdec
    la = (jnp.minimum(z, 0.0) - jnp.log(1.0 + jnp.exp(-jnp.abs(z)))) * (1.0 / GLA_GATE_TEMP)

    la_hi = la.astype(BF16)
    la_lo = (la - la_hi.astype(F32)).astype(BF16)
    acc = _dot(sums, jnp.concatenate([la_hi, la_lo], axis=-1))
    bcum = acc[:t, :LANES] + acc[:t, LANES:]
    btot = acc[t:, :LANES] + acc[t:, LANES:]

    q_dec = q2 * jnp.exp(bcum) * (GLA_DK ** -0.5)
    k_inv = (k2 * jnp.exp(-bcum)).astype(BF16)
    k_tail = k2 * jnp.exp(btot - bcum)
    chunk_decay = jnp.exp(btot)

    lane = lax.broadcasted_iota(I32, (t, 2 * GLA_DK), 1)
    keep = tri > 0.0
    order = range(t // c - 1, -1, -1) if reverse else range(t // c)
    for hh in range(2):
        in_head = (lane // GLA_DK) == hh
        qd = jnp.where(in_head, q_dec, 0.0).astype(BF16)
        kt = jnp.where(in_head, k_tail, 0.0).astype(BF16)
        vh = v2[:, hh * GLA_DV:(hh + 1) * GLA_DV]
        vh_b = vh.astype(BF16)
        att = jnp.where(keep, _dot_nt(qd, k_inv), 0.0).astype(BF16)
        o_intra = _dot(att, vh_b)
        s = s_ref[hh]
        for ci in order:
            rows = slice(ci * c, (ci + 1) * c)
            o_ref[0, rows, hh * GLA_DV:(hh + 1) * GLA_DV] = o_intra[rows] + _dot_nt(qd[rows], s.astype(BF16))
            kv_t = _dot(vh[rows].T.astype(BF16), kt[rows])
            s = s * chunk_decay[ci * c:ci * c + 1, :] + kv_t
        s_ref[hh] = s


def _gla_kernel(kf, qf, vf, df, kb, qb, vb, db, wupf, wupb, bf, bb, sums_f, sums_b, tri_f, tri_b,
                of_ref, ob_ref, sf_ref, sb_ref):
    @pl.when(pl.program_id(1) == 0)
    def _():
        sf_ref[...] = jnp.zeros_like(sf_ref)
        sb_ref[...] = jnp.zeros_like(sb_ref)

    _gla_direction(kf[0], qf[0], vf[0], df[0], wupf[...], bf[...], sums_f[...], tri_f[...], sf_ref, of_ref, False)
    _gla_direction(kb[0], qb[0], vb[0], db[0], wupb[...], bb[...], sums_b[...], tri_b[...], sb_ref, ob_ref, True)


def _chunk_matrices():
    t = TOK_TILE
    r = jnp.arange(t)[:, None]
    q = jnp.arange(t)[None, :]
    same = (r // GLA_CHUNK) == (q // GLA_CHUNK)
    out = []
    for tri in (same & (q <= r), same & (q >= r)):
        out.append((jnp.concatenate([tri, same], axis=0).astype(BF16), tri.astype(F32)))
    return out[0][0], out[1][0], out[0][1], out[1][1]


def _gla_pair(src, col_k, col_q, col_v, col_dec, wupf, wupb, bf, bb):
    bsz, l, _ = src.shape
    nb = l // TOK_TILE
    fwd = lambda j: j
    bwd = lambda j: jnp.where(j == 0, 0, nb - j)

    def spec(width, colidx, order):
        return pl.BlockSpec((1, TOK_TILE, width), lambda b, j: (b, order(j), colidx))

    full = lambda a: pl.BlockSpec(a.shape, lambda b, j: (0,) * a.ndim)
    consts = _chunk_matrices()
    out = jax.ShapeDtypeStruct((bsz, l, 2 * GLA_DV), F32)
    return pl.pallas_call(
        _gla_kernel,
        grid=(bsz, nb),
        in_specs=[spec(LANES, col_k, fwd), spec(LANES, col_q, fwd), spec(2 * GLA_DV, col_v, fwd), spec(LANES, col_dec, fwd),
                  spec(LANES, col_k, bwd), spec(LANES, col_q, bwd), spec(2 * GLA_DV, col_v, bwd), spec(LANES, col_dec, bwd),
                  full(wupf), full(wupb), full(bf), full(bb)] + [full(a) for a in consts],
        out_specs=[pl.BlockSpec((1, TOK_TILE, 2 * GLA_DV), lambda b, j: (b, fwd(j), 0)),
                   pl.BlockSpec((1, TOK_TILE, 2 * GLA_DV), lambda b, j: (b, bwd(j), 0))],
        out_shape=[out, out],
        scratch_shapes=[pltpu.VMEM((2, GLA_DV, LANES), F32), pltpu.VMEM((2, GLA_DV, LANES), F32)],
        compiler_params=_cparams(("arbitrary", "arbitrary")),
        name="gla_pair",
    )(src, src, src, src, src, src, src, src, wupf, wupb, bf, bb, *consts)


def _layer_norm(v, g, b):
    mu = jnp.mean(v, axis=-1, keepdims=True)
    dv = v - mu
    var = jnp.mean(dv * dv, axis=-1, keepdims=True)
    return dv * lax.rsqrt(var + LN_EPS) * g + b


def _outproj_kernel(conv_ref, gate_ref, ofr, obr, ofc, obc, h_ref, gt_ref, wconv_ref, gn_ref, wout_ref,
                    lng_ref, lnb_ref, o_ref, *, alpha, tiles_per_batch):
    tm = TOK_TILE
    is_ctx = (pl.program_id(0) % tiles_per_batch) == 0
    seg = jnp.where(is_ctx, tm, GRID_W)

    cv = conv_ref[...]
    b_gate = cv[:, :CONV_DIM]
    u = cv[:, CONV_DIM:2 * CONV_DIM] * cv[:, 2 * CONV_DIM:]
    pos = lax.broadcasted_iota(I32, (tm, CONV_DIM), 0) & (seg - 1)
    prev = jnp.where(pos == 0, 0.0, pltpu.roll(u, 1, 0))
    nxt = jnp.where(pos == seg - 1, 0.0, pltpu.roll(u, tm - 1, 0))
    wc = wconv_ref[...]
    mix_conv = b_gate * (wc[0:1, :] * prev + wc[1:2, :] * u + wc[2:3, :] * nxt)

    o_r = ofr[...] + obr[...]
    o_c = ofc[...] + obc[...]
    gate = gate_ref[...]
    gn = gn_ref[...]
    parts = [mix_conv.astype(BF16)]
    for hd in range(GLA_HEADS):
        src = o_r if hd < 2 else o_c
        o = src[:, (hd % 2) * GLA_DV:(hd % 2 + 1) * GLA_DV]
        o = o * lax.rsqrt(jnp.mean(o * o, axis=-1, keepdims=True) + RMS_EPS) * gn
        g = gate[:, hd * GLA_DV:(hd + 1) * GLA_DV]
        parts.append((o * (g * jax.nn.sigmoid(g))).astype(BF16))
    mix = jnp.concatenate(parts, axis=-1)
    y = _dot(mix, wout_ref[...])
    v = alpha * h_ref[...] + gt_ref[0] * y
    o_ref[...] = _layer_norm(v, lng_ref[...], lnb_ref[...])


def _outproj(p, o_fr, o_br, o_fc, o_bc, h, gate1, w_conv, g_norm, w_out, ln_g, ln_b, group_of_tile, alpha, tiles_per_batch):
    t, d = h.shape
    tm = TOK_TILE
    tok = lambda width, colidx: pl.BlockSpec((tm, width), lambda i: (i, colidx))
    full = lambda a: pl.BlockSpec(a.shape, lambda i: (0,) * a.ndim)
    return pl.pallas_call(
        functools.partial(_outproj_kernel, alpha=alpha, tiles_per_batch=tiles_per_batch),
        grid=(t // tm,),
        in_specs=[tok(3 * CONV_DIM, P_CONV // (3 * CONV_DIM)), tok(GLA_VAL, P_G // GLA_VAL),
                  tok(2 * GLA_DV, 0), tok(2 * GLA_DV, 0), tok(2 * GLA_DV, 0), tok(2 * GLA_DV, 0),
                  tok(d, 0), pl.BlockSpec((1, 1, d), lambda i: (group_of_tile(i), 0, 0)),
                  full(w_conv), full(g_norm), full(w_out), full(ln_g), full(ln_b)],
        out_specs=pl.BlockSpec((tm, d), lambda i: (i, 0)),
        out_shape=jax.ShapeDtypeStruct((t, d), F32),
        compiler_params=_cparams(("arbitrary",)),
        name="outproj_ln1",
    )(p, p, o_fr, o_br, o_fc, o_bc, h, gate1, w_conv, g_norm, w_out, ln_g, ln_b)


def _route_kernel(h_ref, sh_ref, sc_ref, wrt_hi_ref, wrt_lo_ref, rbt_ref, wsg_ref, wsu_ref, wsd_ref, upper_ref,
                  x3_ref, idx_ref, wt_ref, rank_ref, cnt_ref, shared_ref, base_ref):
    tm = TOK_TILE
    n_exp = rbt_ref.shape[0]
    gsz = n_exp // N_GROUPS

    @pl.when(pl.program_id(0) == 0)
    def _():
        base_ref[...] = jnp.zeros_like(base_ref)

    u = h_ref[...] * (1.0 + sc_ref[0]) + sh_ref[0]
    for j in range(ROW_TILES):
        x3_ref[pl.ds(j, tm, stride=ROW_TILES), :] = u[:, j * LANES:(j + 1) * LANES]

    u_hi = u.astype(BF16)
    u_lo = (u - u_hi.astype(F32)).astype(BF16)
    shared_ref[...] = _dot((jax.nn.silu(_dot(u_hi, wsg_ref[...])) * _dot(u_hi, wsu_ref[...])).astype(BF16), wsd_ref[...])

    wrt_hi = wrt_hi_ref[...]
    logits = _dot_nt(wrt_hi, u_hi) + _dot_nt(wrt_hi, u_lo) + _dot_nt(wrt_lo_ref[...], u_hi)
    s = jax.nn.sigmoid(logits)
    biased = s + rbt_ref[...]

    neg = -jnp.inf
    eidx = lax.broadcasted_iota(I32, (n_exp, tm), 0)
    loc = lax.broadcasted_iota(I32, (gsz, tm), 0)
    gscore = []
    for g in range(N_GROUPS):
        vals = biased[g * gsz:(g + 1) * gsz, :]
        m1 = jnp.max(vals, axis=0, keepdims=True)
        i1 = jnp.min(jnp.where(vals == m1, loc, gsz), axis=0, keepdims=True)
        m2 = jnp.max(jnp.where(loc == i1, neg, vals), axis=0, keepdims=True)
        gscore.append(m1 + m2)
    pieces = []
    for g in range(N_GROUPS):
        ahead = jnp.zeros((1, tm), F32)
        for g2 in range(N_GROUPS):
            if g2 == g:
                continue
            wins = (gscore[g2] >= gscore[g]) if g2 < g else (gscore[g2] > gscore[g])
            ahead = ahead + jnp.where(wins, 1.0, 0.0)
        pieces.append(jnp.where(ahead < TOPK_GROUPS, biased[g * gsz:(g + 1) * gsz, :], neg))
    masked = jnp.concatenate(pieces, axis=0)

    onehot = jnp.zeros((n_exp, tm), F32)
    idxs = []
    for _ in range(TOP_K):
        m = jnp.max(masked, axis=0, keepdims=True)
        i = jnp.min(jnp.where(masked == m, eidx, n_exp), axis=0, keepdims=True)
        hit = eidx == i
        onehot = jnp.where(hit, 1.0, onehot)
        masked = jnp.where(hit, neg, masked)
        idxs.append(i)
    sel = onehot * s
    wnorm = sel / jnp.sum(sel, axis=0, keepdims=True) * ROUTED_SCALE

    onehot_b = onehot.astype(BF16)
    upper = upper_ref[...]
    rank_all = _dot(onehot_b, upper) + base_ref[...]
    base_new = base_ref[...] + _dot(onehot_b, jnp.ones((tm, tm), BF16))
    base_ref[...] = base_new
    cnt_ref[...] = base_new[:, :LANES]

    wts, ranks = [], []
    for k in range(TOP_K):
        hit = eidx == idxs[k]
        wts.append(jnp.sum(jnp.where(hit, wnorm, 0.0), axis=0, keepdims=True))
        ranks.append(jnp.sum(jnp.where(hit, rank_all, 0.0), axis=0, keepdims=True))
    idx_ref[...] = jnp.concatenate(idxs, axis=0)
    wt_ref[...] = jnp.concatenate(wts, axis=0)
    rank_ref[...] = jnp.concatenate(ranks, axis=0).astype(I32)


def _route(h1, shift, scale, wr, rbias, wsg, wsu, wsd, group_of_tile):
    t, d = h1.shape
    tm = TOK_TILE
    n_exp = wr.shape[1]
    wrt = wr.T
    wrt_hi = wrt.astype(BF16)
    wrt_lo = (wrt - wrt_hi.astype(F32)).astype(BF16)
    rbt = jnp.broadcast_to(rbias.reshape(n_exp, 1), (n_exp, tm))
    upper = (jnp.arange(tm)[:, None] < jnp.arange(tm)[None, :]).astype(BF16)
    grp = lambda i: (group_of_tile(i), 0, 0)
    full = lambda a: pl.BlockSpec(a.shape, lambda i: (0,) * a.ndim)
    choice = pl.BlockSpec((TOP_K, tm), lambda i: (0, i))
    choice_out = lambda dt: jax.ShapeDtypeStruct((TOP_K, t), dt)
    return pl.pallas_call(
        _route_kernel,
        grid=(t // tm,),
        in_specs=[pl.BlockSpec((tm, d), lambda i: (i, 0)), pl.BlockSpec((1, 1, d), grp), pl.BlockSpec((1, 1, d), grp),
                  full(wrt_hi), full(wrt_lo), full(rbt), full(wsg), full(wsu), full(wsd), full(upper)],
        out_specs=[pl.BlockSpec((tm * ROW_TILES, LANES), lambda i: (i, 0)), choice, choice, choice,
                   pl.BlockSpec((n_exp, LANES), lambda i: (0, 0)),
                   pl.BlockSpec((tm, d), lambda i: (i, 0))],
        out_shape=[jax.ShapeDtypeStruct((t * ROW_TILES, LANES), F32), choice_out(I32), choice_out(F32), choice_out(I32),
                   jax.ShapeDtypeStruct((n_exp, LANES), F32), jax.ShapeDtypeStruct((t, d), F32)],
        scratch_shapes=[pltpu.VMEM((n_exp, tm), F32)],
        compiler_params=_cparams(("arbitrary",)),
        name="route_shared",
    )(h1, shift, scale, wrt_hi, wrt_lo, rbt, wsg, wsu, wsd, upper)


def _dispatch_kernel(idx_ref, rank_ref, pstart_ref, x_ref, xs_hbm, slot_ref, sem):
    def body(i, carry):
        src = x_ref.at[pl.ds(pl.multiple_of(i * ROW_TILES, ROW_TILES), ROW_TILES)]
        for k in range(TOP_K):
            r = i * TOP_K + k
            slot = pstart_ref[idx_ref[r]] + rank_ref[r]
            slot_ref[r] = slot
            pltpu.make_async_copy(src, xs_hbm.at[slot], sem).start()
        return carry

    lax.fori_loop(0, DSP_TILE, body, 0)
    done = xs_hbm.at[pl.ds(0, DSP_TILE * TOP_K)]
    pltpu.make_async_copy(done, done, sem).wait()


def _dispatch(x2, idx, rank, pstart, n_slots):
    t = x2.shape[0] // ROW_TILES
    choice = pl.BlockSpec((DSP_TILE * TOP_K,), lambda i: (i,), memory_space=pltpu.SMEM)
    return pl.pallas_call(
        _dispatch_kernel,
        grid=(t // DSP_TILE,),
        in_specs=[choice, choice, pl.BlockSpec(memory_space=pltpu.SMEM),
                  pl.BlockSpec((DSP_TILE * ROW_TILES, LANES), lambda i: (i, 0))],
        out_specs=[pl.BlockSpec(memory_space=pl.ANY), choice],
        out_shape=[jax.ShapeDtypeStruct((n_slots, ROW_TILES, LANES), F32), jax.ShapeDtypeStruct(idx.shape, I32)],
        scratch_shapes=[pltpu.SemaphoreType.DMA(())],
        compiler_params=_cparams(("arbitrary",)),
        name="dispatch",
    )(idx, rank, pstart, x2)


def _expert_kernel(gstart_ref, nblk_ref, ntot_ref, xs_hbm, wg_ref, wu_ref, wd_ref, ys_hbm,
                   xbuf, ybuf, wg_b, wu_b, wd_b, sem_in, sem_out):
    e = pl.program_id(0)
    nb = nblk_ref[e]
    g0 = gstart_ref[e]
    total = ntot_ref[0]
    blk_rows = MOE_BLOCK * ROW_TILES
    ahead = EXP_BUFS - 1

    def rows_of(g):
        return pl.ds(pl.multiple_of(g * blk_rows, blk_rows), blk_rows)

    def in_copy(g):
        s = g % EXP_BUFS
        return pltpu.make_async_copy(xs_hbm.at[rows_of(g)], xbuf.at[s], sem_in.at[s])

    def out_copy(g):
        s = g % EXP_BUFS
        return pltpu.make_async_copy(ybuf.at[s], ys_hbm.at[rows_of(g)], sem_out.at[s])

    @pl.when(e == 0)
    def _():
        for g in range(ahead):
            @pl.when(g < total)
            def _():
                in_copy(g).start()

    @pl.when(nb > 0)
    def _():
        wg_b[...] = wg_ref[0, 0].astype(BF16)
        wu_b[...] = wu_ref[0, 0].astype(BF16)
        wd_b[...] = wd_ref[0, 0].astype(BF16)

    def body(b, carry):
        g = g0 + b
        s = g % EXP_BUFS

        @pl.when(g + ahead < total)
        def _():
            in_copy(g + ahead).start()

        in_copy(g).wait()

        @pl.when(g >= EXP_BUFS)
        def _():
            out_copy(g - EXP_BUFS).wait()

        x = jnp.concatenate([xbuf[s, pl.ds(j, MOE_BLOCK, stride=ROW_TILES), :] for j in range(ROW_TILES)], axis=-1)
        xb = x.astype(BF16)
        hid = jax.nn.silu(_dot(xb, wg_b[...])) * _dot(xb, wu_b[...])
        y = _dot(hid.astype(BF16), wd_b[...])
        for j in range(ROW_TILES):
            ybuf[s, pl.ds(j, MOE_BLOCK, stride=ROW_TILES), :] = y[:, j * LANES:(j + 1) * LANES]
        out_copy(g).start()
        return carry

    lax.fori_loop(0, nb, body, 0)

    @pl.when(e == pl.num_programs(0) - 1)
    def _():
        for back in range(EXP_BUFS):
            @pl.when(total - 1 - back >= 0)
            def _():
                out_copy(total - 1 - back).wait()


def _experts(xs2, gstart, nblk, ntot, layer, w_gate, w_up, w_down):
    n_rows = xs2.shape[0]
    _, n_exp, d, hdim = w_gate.shape
    blk_rows = MOE_BLOCK * ROW_TILES
    wmap = lambda e, gs, nb, nt: (layer, e, 0, 0)
    return pl.pallas_call(
        _expert_kernel,
        grid_spec=pltpu.PrefetchScalarGridSpec(
            num_scalar_prefetch=3,
            grid=(n_exp,),
            in_specs=[pl.BlockSpec(memory_space=pl.ANY),
                      pl.BlockSpec((1, 1, d, hdim), wmap),
                      pl.BlockSpec((1, 1, d, hdim), wmap),
                      pl.BlockSpec((1, 1, hdim, d), wmap)],
            out_specs=pl.BlockSpec(memory_space=pl.ANY),
            scratch_shapes=[pltpu.VMEM((EXP_BUFS, blk_rows, LANES), F32), pltpu.VMEM((EXP_BUFS, blk_rows, LANES), F32),
                            pltpu.VMEM((d, hdim), BF16), pltpu.VMEM((d, hdim), BF16), pltpu.VMEM((hdim, d), BF16),
                            pltpu.SemaphoreType.DMA((EXP_BUFS,)), pltpu.SemaphoreType.DMA((EXP_BUFS,))],
        ),
        out_shape=jax.ShapeDtypeStruct((n_rows, LANES), F32),
        compiler_params=_cparams(("arbitrary",)),
        name="experts",
    )(gstart, nblk, ntot, xs2, w_gate, w_up, w_down)


def _combine_kernel(slot_cur, slot_nxt, ys_hbm, wt_ref, shared_ref, h_ref, gt_ref, lng_ref, lnb_ref, o_ref,
                    buf, mixed, sem, *, alpha):
    i = pl.program_id(0)
    n = pl.num_programs(0)
    tt = CMB_TILE

    def issue(slot_ref, b):
        def body(tok, carry):
            for k in range(TOP_K):
                r = tok * TOP_K + k
                dst = buf.at[b, pl.ds(pl.multiple_of(r * ROW_TILES, ROW_TILES), ROW_TILES)]
                pltpu.make_async_copy(ys_hbm.at[slot_ref[r]], dst, sem.at[b]).start()
            return carry
        lax.fori_loop(0, tt, body, 0)

    @pl.when(i == 0)
    def _():
        issue(slot_cur, 0)

    @pl.when(i + 1 < n)
    def _():
        issue(slot_nxt, (i + 1) % 2)

    cur = i % 2
    pltpu.make_async_copy(buf.at[cur], buf.at[cur], sem.at[cur]).wait()

    unroll = 4

    def mix_rows(step, carry):
        for u in range(unroll):
            tok = step * unroll + u
            first = tok * TOP_K
            acc = None
            for k in range(TOP_K):
                row = buf[cur, pl.ds(pl.multiple_of((first + k) * ROW_TILES, ROW_TILES), ROW_TILES), :]
                term = wt_ref[first + k] * row
                acc = term if acc is None else acc + term
            mixed[pl.ds(pl.multiple_of(tok * ROW_TILES, ROW_TILES), ROW_TILES), :] = acc
        return carry

    lax.fori_loop(0, tt // unroll, mix_rows, 0)
    routed = jnp.concatenate([mixed[pl.ds(j, tt, stride=ROW_TILES), :] for j in range(ROW_TILES)], axis=-1)
    f = shared_ref[...] + routed
    v = alpha * h_ref[...] + gt_ref[0] * f
    o_ref[...] = _layer_norm(v, lng_ref[...], lnb_ref[...])


def _combine(ys3, slots, wts, shared, h1, gate2, ln_g, ln_b, group_of_tile, alpha):
    t, d = h1.shape
    tt = CMB_TILE
    n = t // tt
    scale = TOK_TILE // tt
    full = lambda a: pl.BlockSpec(a.shape, lambda i: (0,) * a.ndim)
    cur = pl.BlockSpec((tt * TOP_K,), lambda i: (i,), memory_space=pltpu.SMEM)
    nxt = pl.BlockSpec((tt * TOP_K,), lambda i: (jnp.minimum(i + 1, n - 1),), memory_space=pltpu.SMEM)
    return pl.pallas_call(
        functools.partial(_combine_kernel, alpha=alpha),
        grid=(n,),
        in_specs=[cur, nxt, pl.BlockSpec(memory_space=pl.ANY), cur,
                  pl.BlockSpec((tt, d), lambda i: (i, 0)),
                  pl.BlockSpec((tt, d), lambda i: (i, 0)),
                  pl.BlockSpec((1, 1, d), lambda i: (group_of_tile(i // scale), 0, 0)),
                  full(ln_g), full(ln_b)],
        out_specs=pl.BlockSpec((tt, d), lambda i: (i, 0)),
        out_shape=jax.ShapeDtypeStruct((t, d), F32),
        scratch_shapes=[pltpu.VMEM((2, tt * TOP_K * ROW_TILES, LANES), F32), pltpu.VMEM((tt * ROW_TILES, LANES), F32),
                        pltpu.SemaphoreType.DMA((2,))],
        compiler_params=_cparams(("arbitrary",)),
        name="combine_ln2",
    )(slots, slots, ys3, wts, shared, h1, gate2, ln_g, ln_b)


def _reorder_w_in(w):
    d = w.shape[0]
    o = 0
    k = w[:, o:o + GLA_KEY]; o += GLA_KEY
    v = w[:, o:o + GLA_VAL]; o += GLA_VAL
    dec = w[:, o:o + 2 * GLA_LOWRANK]; o += 2 * GLA_LOWRANK
    q = w[:, o:o + GLA_KEY]; o += GLA_KEY
    g = w[:, o:o + GLA_VAL]; o += GLA_VAL
    conv = w[:, o:o + 3 * CONV_DIM]
    pad = jnp.zeros((d, LANES - 2 * GLA_LOWRANK), w.dtype)
    hk, hv = 2 * GLA_DK, 2 * GLA_DV
    pairs = [jnp.concatenate([v[:, i * hv:(i + 1) * hv], k[:, i * hk:(i + 1) * hk], q[:, i * hk:(i + 1) * hk]], axis=1)
             for i in range(2)]
    return jnp.concatenate([conv, g, pairs[0], pairs[1], dec, pad], axis=1).astype(BF16)


def _decay_weights(w_up, b_dec, pair):
    lo = pair * 2 * GLA_DK
    nr = 2 * GLA_LOWRANK
    outs = []
    for d in range(2):
        w = jnp.zeros((nr, LANES), F32).at[d * GLA_LOWRANK:(d + 1) * GLA_LOWRANK, :].set(w_up[d][:, lo:lo + LANES])
        w_hi = w.astype(BF16)
        w_lo = (w - w_hi.astype(F32)).astype(BF16)
        packed = jnp.concatenate([w_hi, w_hi, w_lo, jnp.zeros((LANES - 3 * nr, LANES), BF16)], axis=0)
        outs.append((packed, b_dec[d][lo:lo + LANES].reshape(1, LANES)))
    return outs[0][0], outs[1][0], outs[0][1], outs[1][1]


def _to_scan_order(a, ctx_len, rows):
    b, _, f = a.shape
    lat = a[:, ctx_len:].reshape(b, rows, GRID_W, f).swapaxes(1, 2).reshape(b, rows * GRID_W, f)
    return jnp.concatenate([a[:, :ctx_len], lat], axis=1)


def _from_scan_order(a, ctx_len, rows):
    b, _, f = a.shape
    lat = a[:, ctx_len:].reshape(b, GRID_W, rows, f).swapaxes(1, 2).reshape(b, rows * GRID_W, f)
    return jnp.concatenate([a[:, :ctx_len], lat], axis=1)


def kernel(x, c, ctx, c_ctx, w_mod, b_mod, w_in, w_conv, w_decay_up, b_decay, g_gla_norm, w_out, ln1_g, ln1_b, w_router, router_bias, w_e_gate, w_e_up, w_e_down, w_s_gate, w_s_up, w_s_down, ln2_g, ln2_b):
    bsz, seq, d = x.shape
    ctx_len = ctx.shape[1]
    depth = w_mod.shape[0]
    n_exp = w_router.shape[2]
    rows = seq // GRID_W
    l = ctx_len + seq
    t = bsz * l
    assert ctx_len == TOK_TILE and seq % TOK_TILE == 0 and d == ROW_TILES * LANES
    assert t % DSP_TILE == 0 and bsz + 1 <= SUBLANES
    tiles_per_batch = l // TOK_TILE
    alpha = float((2 * depth) ** 0.25)

    def group_of_tile(i):
        return jnp.where(i % tiles_per_batch == 0, bsz, i // tiles_per_batch)

    cond8 = jnp.zeros((SUBLANES, d), F32).at[:bsz].set(c).at[bsz].set(c_ctx)
    mod = _modulation(cond8, w_mod, b_mod).reshape(depth, SUBLANES, 6, 1, d)

    nk = t * TOP_K
    n_blocks = -(-(nk + n_exp * (MOE_BLOCK - 1)) // MOE_BLOCK)
    n_slots = n_blocks * MOE_BLOCK

    h = jnp.concatenate([ctx, x], axis=1).reshape(t, d)
    for i in range(depth):
        m = [mod[i, :, j] for j in range(6)]
        p = _inproj(h, m[0], m[1], _reorder_w_in(w_in[i]), group_of_tile)
        p3 = p.reshape(bsz, l, P_COLS)

        wf, wb, bf, bb = _decay_weights(w_decay_up[i], b_decay[i], 0)
        o_fr, o_br = _gla_pair(p3, (P_ROW + PAIR_K) // LANES, (P_ROW + PAIR_Q) // LANES, (P_ROW + PAIR_V) // (2 * GLA_DV),
                               P_DEC // LANES, wf, wb, bf, bb)
        col_in = _to_scan_order(p3[..., P_COL:], ctx_len, rows)
        wf, wb, bf, bb = _decay_weights(w_decay_up[i], b_decay[i], 1)
        o_fc, o_bc = _gla_pair(col_in, PAIR_K // LANES, PAIR_Q // LANES, PAIR_V // (2 * GLA_DV), (P_DEC - P_COL) // LANES,
                               wf, wb, bf, bb)
        o_fc = _from_scan_order(o_fc, ctx_len, rows)
        o_bc = _from_scan_order(o_bc, ctx_len, rows)

        flat = lambda a: a.reshape(t, a.shape[-1])
        h1 = _outproj(p, flat(o_fr), flat(o_br), flat(o_fc), flat(o_bc), h, m[2], w_conv[i],
                      g_gla_norm[i].reshape(1, GLA_DV), w_out[i].astype(BF16), ln1_g[i].reshape(1, d), ln1_b[i].reshape(1, d),
                      group_of_tile, alpha, tiles_per_batch)

        x3, idx, wts, rank, counts, shared = _route(
            h1, m[3], m[4], w_router[i], router_bias[i],
            w_s_gate[i].astype(BF16), w_s_up[i].astype(BF16), w_s_down[i].astype(BF16), group_of_tile)

        cnt = counts[:, 0].astype(I32)
        padded = (cnt + MOE_BLOCK - 1) // MOE_BLOCK * MOE_BLOCK
        pend = jnp.cumsum(padded)
        pstart = pend - padded
        idx, rank, wts = idx.T.reshape(nk), rank.T.reshape(nk), wts.T.reshape(nk)

        xs3, slots = _dispatch(x3, idx, rank, pstart, n_slots)
        ys2 = _experts(xs3.reshape(n_slots * ROW_TILES, LANES), pstart // MOE_BLOCK, padded // MOE_BLOCK,
                       (pend[-1] // MOE_BLOCK).reshape(1), i, w_e_gate, w_e_up, w_e_down)
        h = _combine(ys2.reshape(n_slots, ROW_TILES, LANES), slots, wts, shared, h1, m[5],
                     ln2_g[i].reshape(1, d), ln2_b[i].reshape(1, d), group_of_tile, alpha)

    return h.reshape(bsz, l, d)[:, ctx_len:]
```

```python
import functools

import jax
import jax.numpy as jnp
from jax import lax
from jax.experimental import pallas as pl
from jax.experimental.pallas import tpu as pltpu

F32 = jnp.float32
BF16 = jnp.bfloat16
I32 = jnp.int32

GRID_W = 64
CONV_DIM = 512
GLA_HEADS = 4
GLA_DK = 64
GLA_DV = 128
GLA_KEY = GLA_HEADS * GLA_DK
GLA_VAL = GLA_HEADS * GLA_DV
GLA_LOWRANK = 16
GLA_GATE_TEMP = 16.0
GLA_CHUNK = 64
TOP_K = 8
N_GROUPS = 8
TOPK_GROUPS = 4
ROUTED_SCALE = 2.5
LN_EPS = 1e-5
RMS_EPS = 1e-6

LANES = 128
SUBLANES = 8
VMEM_LIMIT = 56 * 1024 * 1024

P_CONV = 0
P_G = 1536
P_ROW = 2048
P_COL = 2560
P_DEC = 3072
P_COLS = 3200
PAIR_V, PAIR_K, PAIR_Q = 0, 256, 384

TOK_TILE = 256
MOE_BLOCK = 256
EXP_BUFS = 4
CMB_TILE = 256
DSP_TILE = 512
ROW_TILES = 8
DMA_QUEUES = 2


def _cparams(sem):
    return pltpu.CompilerParams(dimension_semantics=sem, vmem_limit_bytes=VMEM_LIMIT)


def _mod_kernel(c_ref, w_ref, b_ref, o_ref):
    c = c_ref[...]
    a = c * jax.nn.sigmoid(c)
    o_ref[0] = jnp.dot(a, w_ref[0], preferred_element_type=F32, precision=lax.Precision.HIGHEST) + b_ref[0]


def _modulation(cond8, w_mod, b_mod):
    depth, d, n = w_mod.shape
    tn = 1536
    return pl.pallas_call(
        _mod_kernel,
        grid=(depth, n // tn),
        in_specs=[
            pl.BlockSpec((SUBLANES, d), lambda i, j: (0, 0)),
            pl.BlockSpec((1, d, tn), lambda i, j: (i, 0, j)),
            pl.BlockSpec((1, 1, tn), lambda i, j: (i, 0, j)),
        ],
        out_specs=pl.BlockSpec((1, SUBLANES, tn), lambda i, j: (i, 0, j)),
        out_shape=jax.ShapeDtypeStruct((depth, SUBLANES, n), F32),
        compiler_params=_cparams(("arbitrary", "arbitrary")),
        name="adaln_mod",
    )(cond8, w_mod, b_mod.reshape(depth, 1, n))


def _inproj_kernel(x_ref, sh_ref, sc_ref, w_ref, o_ref):
    h = x_ref[...] * (1.0 + sc_ref[0]) + sh_ref[0]
    o_ref[...] = jnp.dot(h.astype(BF16), w_ref[...], preferred_element_type=F32)


def _inproj(h, shift, scale, w, group_of_tile):
    t, d = h.shape
    n = w.shape[1]
    grp = lambda i: (group_of_tile(i), 0, 0)
    return pl.pallas_call(
        _inproj_kernel,
        grid=(t // TOK_TILE,),
        in_specs=[
            pl.BlockSpec((TOK_TILE, d), lambda i: (i, 0)),
            pl.BlockSpec((1, 1, d), grp),
            pl.BlockSpec((1, 1, d), grp),
            pl.BlockSpec((d, n), lambda i: (0, 0)),
        ],
        out_specs=pl.BlockSpec((TOK_TILE, n), lambda i: (i, 0)),
        out_shape=jax.ShapeDtypeStruct((t, n), F32),
        compiler_params=_cparams(("arbitrary",)),
        name="inproj",
    )(h, shift, scale, w)


def _dot(a, b):
    return jnp.dot(a, b, preferred_element_type=F32)


def _dot_nt(a, b):
    return lax.dot_general(a, b, (((1,), (1,)), ((), ())), preferred_element_type=F32)


def _gla_direction(k2, q2, v2, dec, wup, bdec, sums, tri, s_ref, o_ref, reverse):
    t = TOK_TILE
    c = GLA_CHUNK
    d_hi = dec.astype(BF16).astype(F32)
    packed = d_hi + pltpu.roll(dec - d_hi, 2 * GLA_LOWRANK, 1) + pltpu.roll(d_hi, 4 * GLA_LOWRANK, 1)
    z = _dot(packed.astype(BF16), wup) + bdec
    la = (jnp.minimum(z, 0.0) - jnp.log(1.0 + jnp.exp(-jnp.abs(z)))) * (1.0 / GLA_GATE_TEMP)

    la_hi = la.astype(BF16)
    la_lo = (la - la_hi.astype(F32)).astype(BF16)
    acc = _dot(sums, jnp.concatenate([la_hi, la_lo], axis=-1))
    bcum = acc[:t, :LANES] + acc[:t, LANES:]
    btot = acc[t:, :LANES] + acc[t:, LANES:]

    q_dec = q2 * jnp.exp(bcum) * (GLA_DK ** -0.5)
    k_inv = (k2 * jnp.exp(-bcum)).astype(BF16)
    k_tail = k2 * jnp.exp(btot - bcum)
    chunk_decay = jnp.exp(btot)

    lane = lax.broadcasted_iota(I32, (t, 2 * GLA_DK), 1)
    keep = tri > 0.0
    order = range(t // c - 1, -1, -1) if reverse else range(t // c)
    for hh in range(2):
        in_head = (lane // GLA_DK) == hh
        qd = jnp.where(in_head, q_dec, 0.0).astype(BF16)
        kt = jnp.where(in_head, k_tail, 0.0).astype(BF16)
        vh = v2[:, hh * GLA_DV:(hh + 1) * GLA_DV]
        vh_b = vh.astype(BF16)
        att = jnp.where(keep, _dot_nt(qd, k_inv), 0.0).astype(BF16)
        o_intra = _dot(att, vh_b)
        s = s_ref[hh]
        for ci in order:
            rows = slice(ci * c, (ci + 1) * c)
            o_ref[0, rows, hh * GLA_DV:(hh + 1) * GLA_DV] = o_intra[rows] + _dot_nt(qd[rows], s.astype(BF16))
            kv_t = _dot(vh[rows].T.astype(BF16), kt[rows])
            s = s * chunk_decay[ci * c:ci * c + 1, :] + kv_t
        s_ref[hh] = s


def _gla_kernel(kf, qf, vf, df, kb, qb, vb, db, wupf, wupb, bf, bb, sums_f, sums_b, tri_f, tri_b,
                of_ref, ob_ref, sf_ref, sb_ref):
    @pl.when(pl.program_id(1) == 0)
    def _():
        sf_ref[...] = jnp.zeros_like(sf_ref)
        sb_ref[...] = jnp.zeros_like(sb_ref)

    _gla_direction(kf[0], qf[0], vf[0], df[0], wupf[...], bf[...], sums_f[...], tri_f[...], sf_ref, of_ref, False)
    _gla_direction(kb[0], qb[0], vb[0], db[0], wupb[...], bb[...], sums_b[...], tri_b[...], sb_ref, ob_ref, True)


def _chunk_matrices():
    t = TOK_TILE
    r = jnp.arange(t)[:, None]
    q = jnp.arange(t)[None, :]
    same = (r // GLA_CHUNK) == (q // GLA_CHUNK)
    out = []
    for tri in (same & (q <= r), same & (q >= r)):
        out.append((jnp.concatenate([tri, same], axis=0).astype(BF16), tri.astype(F32)))
    return out[0][0], out[1][0], out[0][1], out[1][1]


def _gla_pair(src, col_k, col_q, col_v, col_dec, wupf, wupb, bf, bb):
    bsz, l, _ = src.shape
    nb = l // TOK_TILE
    fwd = lambda j: j
    bwd = lambda j: jnp.where(j == 0, 0, nb - j)

    def spec(width, colidx, order):
        return pl.BlockSpec((1, TOK_TILE, width), lambda b, j: (b, order(j), colidx))

    full = lambda a: pl.BlockSpec(a.shape, lambda b, j: (0,) * a.ndim)
    consts = _chunk_matrices()
    out = jax.ShapeDtypeStruct((bsz, l, 2 * GLA_DV), F32)
    return pl.pallas_call(
        _gla_kernel,
        grid=(bsz, nb),
        in_specs=[spec(LANES, col_k, fwd), spec(LANES, col_q, fwd), spec(2 * GLA_DV, col_v, fwd), spec(LANES, col_dec, fwd),
                  spec(LANES, col_k, bwd), spec(LANES, col_q, bwd), spec(2 * GLA_DV, col_v, bwd), spec(LANES, col_dec, bwd),
                  full(wupf), full(wupb), full(bf), full(bb)] + [full(a) for a in consts],
        out_specs=[pl.BlockSpec((1, TOK_TILE, 2 * GLA_DV), lambda b, j: (b, fwd(j), 0)),
                   pl.BlockSpec((1, TOK_TILE, 2 * GLA_DV), lambda b, j: (b, bwd(j), 0))],
        out_shape=[out, out],
        scratch_shapes=[pltpu.VMEM((2, GLA_DV, LANES), F32), pltpu.VMEM((2, GLA_DV, LANES), F32)],
        compiler_params=_cparams(("arbitrary", "arbitrary")),
        name="gla_pair",
    )(src, src, src, src, src, src, src, src, wupf, wupb, bf, bb, *consts)


def _layer_norm(v, g, b):
    mu = jnp.mean(v, axis=-1, keepdims=True)
    dv = v - mu
    var = jnp.mean(dv * dv, axis=-1, keepdims=True)
    return dv * lax.rsqrt(var + LN_EPS) * g + b


def _outproj_kernel(conv_ref, gate_ref, ofr, obr, ofc, obc, h_ref, gt_ref, wconv_ref, gn_ref, wout_ref,
                    lng_ref, lnb_ref, o_ref, *, alpha, tiles_per_batch):
    tm = TOK_TILE
    is_ctx = (pl.program_id(0) % tiles_per_batch) == 0
    seg = jnp.where(is_ctx, tm, GRID_W)

    cv = conv_ref[...]
    b_gate = cv[:, :CONV_DIM]
    u = cv[:, CONV_DIM:2 * CONV_DIM] * cv[:, 2 * CONV_DIM:]
    pos = lax.broadcasted_iota(I32, (tm, CONV_DIM), 0) & (seg - 1)
    prev = jnp.where(pos == 0, 0.0, pltpu.roll(u, 1, 0))
    nxt = jnp.where(pos == seg - 1, 0.0, pltpu.roll(u, tm - 1, 0))
    wc = wconv_ref[...]
    mix_conv = b_gate * (wc[0:1, :] * prev + wc[1:2, :] * u + wc[2:3, :] * nxt)

    o_r = ofr[...] + obr[...]
    o_c = ofc[...] + obc[...]
    gate = gate_ref[...]
    gn = gn_ref[...]
    parts = [mix_conv.astype(BF16)]
    for hd in range(GLA_HEADS):
        src = o_r if hd < 2 else o_c
        o = src[:, (hd % 2) * GLA_DV:(hd % 2 + 1) * GLA_DV]
        o = o * lax.rsqrt(jnp.mean(o * o, axis=-1, keepdims=True) + RMS_EPS) * gn
        g = gate[:, hd * GLA_DV:(hd + 1) * GLA_DV]
        parts.append((o * (g * jax.nn.sigmoid(g))).astype(BF16))
    mix = jnp.concatenate(parts, axis=-1)
    y = _dot(mix, wout_ref[...])
    v = alpha * h_ref[...] + gt_ref[0] * y
    o_ref[...] = _layer_norm(v, lng_ref[...], lnb_ref[...])


def _outproj(p, o_fr, o_br, o_fc, o_bc, h, gate1, w_conv, g_norm, w_out, ln_g, ln_b, group_of_tile, alpha, tiles_per_batch):
    t, d = h.shape
    tm = TOK_TILE
    tok = lambda width, colidx: pl.BlockSpec((tm, width), lambda i: (i, colidx))
    full = lambda a: pl.BlockSpec(a.shape, lambda i: (0,) * a.ndim)
    return pl.pallas_call(
        functools.partial(_outproj_kernel, alpha=alpha, tiles_per_batch=tiles_per_batch),
        grid=(t // tm,),
        in_specs=[tok(3 * CONV_DIM, P_CONV // (3 * CONV_DIM)), tok(GLA_VAL, P_G // GLA_VAL),
                  tok(2 * GLA_DV, 0), tok(2 * GLA_DV, 0), tok(2 * GLA_DV, 0), tok(2 * GLA_DV, 0),
                  tok(d, 0), pl.BlockSpec((1, 1, d), lambda i: (group_of_tile(i), 0, 0)),
                  full(w_conv), full(g_norm), full(w_out), full(ln_g), full(ln_b)],
        out_specs=pl.BlockSpec((tm, d), lambda i: (i, 0)),
        out_shape=jax.ShapeDtypeStruct((t, d), F32),
        compiler_params=_cparams(("arbitrary",)),
        name="outproj_ln1",
    )(p, p, o_fr, o_br, o_fc, o_bc, h, gate1, w_conv, g_norm, w_out, ln_g, ln_b)


def _route_kernel(h_ref, sh_ref, sc_ref, wrt_hi_ref, wrt_lo_ref, rbt_ref, wsg_ref, wsu_ref, wsd_ref, upper_ref,
                  x3_ref, idx_ref, wt_ref, rank_ref, cnt_ref, shared_ref, base_ref):
    tm = TOK_TILE
    n_exp = rbt_ref.shape[0]
    gsz = n_exp // N_GROUPS

    @pl.when(pl.program_id(0) == 0)
    def _():
        base_ref[...] = jnp.zeros_like(base_ref)

    u = h_ref[...] * (1.0 + sc_ref[0]) + sh_ref[0]
    for j in range(ROW_TILES):
        x3_ref[pl.ds(j, tm, stride=ROW_TILES), :] = u[:, j * LANES:(j + 1) * LANES]

    u_hi = u.astype(BF16)
    u_lo = (u - u_hi.astype(F32)).astype(BF16)
    shared_ref[...] = _dot((jax.nn.silu(_dot(u_hi, wsg_ref[...])) * _dot(u_hi, wsu_ref[...])).astype(BF16), wsd_ref[...])

    wrt_hi = wrt_hi_ref[...]
    logits = _dot_nt(wrt_hi, u_hi) + _dot_nt(wrt_hi, u_lo) + _dot_nt(wrt_lo_ref[...], u_hi)
    s = jax.nn.sigmoid(logits)
    biased = s + rbt_ref[...]

    neg = -jnp.inf
    eidx = lax.broadcasted_iota(I32, (n_exp, tm), 0)
    loc = lax.broadcasted_iota(I32, (gsz, tm), 0)
    gscore = []
    for g in range(N_GROUPS):
        vals = biased[g * gsz:(g + 1) * gsz, :]
        m1 = jnp.max(vals, axis=0, keepdims=True)
        i1 = jnp.min(jnp.where(vals == m1, loc, gsz), axis=0, keepdims=True)
        m2 = jnp.max(jnp.where(loc == i1, neg, vals), axis=0, keepdims=True)
        gscore.append(m1 + m2)
    pieces = []
    for g in range(N_GROUPS):
        ahead = jnp.zeros((1, tm), F32)
        for g2 in range(N_GROUPS):
            if g2 == g:
                continue
            wins = (gscore[g2] >= gscore[g]) if g2 < g else (gscore[g2] > gscore[g])
            ahead = ahead + jnp.where(wins, 1.0, 0.0)
        pieces.append(jnp.where(ahead < TOPK_GROUPS, biased[g * gsz:(g + 1) * gsz, :], neg))
    masked = jnp.concatenate(pieces, axis=0)

    onehot = jnp.zeros((n_exp, tm), F32)
    idxs = []
    for _ in range(TOP_K):
        m = jnp.max(masked, axis=0, keepdims=True)
        i = jnp.min(jnp.where(masked == m, eidx, n_exp), axis=0, keepdims=True)
        hit = eidx == i
        onehot = jnp.where(hit, 1.0, onehot)
        masked = jnp.where(hit, neg, masked)
        idxs.append(i)
    sel = onehot * s
    wnorm = sel / jnp.sum(sel, axis=0, keepdims=True) * ROUTED_SCALE

    onehot_b = onehot.astype(BF16)
    upper = upper_ref[...]
    rank_all = _dot(onehot_b, upper) + base_ref[...]
    base_new = base_ref[...] + _dot(onehot_b, jnp.ones((tm, tm), BF16))
    base_ref[...] = base_new
    cnt_ref[...] = base_new[:, :LANES]

    wts, ranks = [], []
    for k in range(TOP_K):
        hit = eidx == idxs[k]
        wts.append(jnp.sum(jnp.where(hit, wnorm, 0.0), axis=0, keepdims=True))
        ranks.append(jnp.sum(jnp.where(hit, rank_all, 0.0), axis=0, keepdims=True))
    idx_ref[...] = jnp.concatenate(idxs, axis=0)
    wt_ref[...] = jnp.concatenate(wts, axis=0)
    rank_ref[...] = jnp.concatenate(ranks, axis=0).astype(I32)


def _route(h1, shift, scale, wr, rbias, wsg, wsu, wsd, group_of_tile):
    t, d = h1.shape
    tm = TOK_TILE
    n_exp = wr.shape[1]
    wrt = wr.T
    wrt_hi = wrt.astype(BF16)
    wrt_lo = (wrt - wrt_hi.astype(F32)).astype(BF16)
    rbt = jnp.broadcast_to(rbias.reshape(n_exp, 1), (n_exp, tm))
    upper = (jnp.arange(tm)[:, None] < jnp.arange(tm)[None, :]).astype(BF16)
    grp = lambda i: (group_of_tile(i), 0, 0)
    full = lambda a: pl.BlockSpec(a.shape, lambda i: (0,) * a.ndim)
    choice = pl.BlockSpec((TOP_K, tm), lambda i: (0, i))
    choice_out = lambda dt: jax.ShapeDtypeStruct((TOP_K, t), dt)
    return pl.pallas_call(
        _route_kernel,
        grid=(t // tm,),
        in_specs=[pl.BlockSpec((tm, d), lambda i: (i, 0)), pl.BlockSpec((1, 1, d), grp), pl.BlockSpec((1, 1, d), grp),
                  full(wrt_hi), full(wrt_lo), full(rbt), full(wsg), full(wsu), full(wsd), full(upper)],
        out_specs=[pl.BlockSpec((tm * ROW_TILES, LANES), lambda i: (i, 0)), choice, choice, choice,
                   pl.BlockSpec((n_exp, LANES), lambda i: (0, 0)),
                   pl.BlockSpec((tm, d), lambda i: (i, 0))],
        out_shape=[jax.ShapeDtypeStruct((t * ROW_TILES, LANES), F32), choice_out(I32), choice_out(F32), choice_out(I32),
                   jax.ShapeDtypeStruct((n_exp, LANES), F32), jax.ShapeDtypeStruct((t, d), F32)],
        scratch_shapes=[pltpu.VMEM((n_exp, tm), F32)],
        compiler_params=_cparams(("arbitrary",)),
        name="route_shared",
    )(h1, shift, scale, wrt_hi, wrt_lo, rbt, wsg, wsu, wsd, upper)


def _dispatch_kernel(idx_ref, rank_ref, pstart_ref, x_ref, xs_hbm, slot_ref, sem):
    def body(i, carry):
        src = x_ref.at[pl.ds(pl.multiple_of(i * ROW_TILES, ROW_TILES), ROW_TILES)]
        for k in range(TOP_K):
            r = i * TOP_K + k
            slot = pstart_ref[idx_ref[r]] + rank_ref[r]
            slot_ref[r] = slot
            pltpu.make_async_copy(src, xs_hbm.at[slot], sem).start(priority=k % DMA_QUEUES)
        return carry

    lax.fori_loop(0, DSP_TILE, body, 0)
    done = xs_hbm.at[pl.ds(0, DSP_TILE * TOP_K)]
    pltpu.make_async_copy(done, done, sem).wait()


def _dispatch(x2, idx, rank, pstart, n_slots):
    t = x2.shape[0] // ROW_TILES
    choice = pl.BlockSpec((DSP_TILE * TOP_K,), lambda i: (i,), memory_space=pltpu.SMEM)
    return pl.pallas_call(
        _dispatch_kernel,
        grid=(t // DSP_TILE,),
        in_specs=[choice, choice, pl.BlockSpec(memory_space=pltpu.SMEM),
                  pl.BlockSpec((DSP_TILE * ROW_TILES, LANES), lambda i: (i, 0))],
        out_specs=[pl.BlockSpec(memory_space=pl.ANY), choice],
        out_shape=[jax.ShapeDtypeStruct((n_slots, ROW_TILES, LANES), F32), jax.ShapeDtypeStruct(idx.shape, I32)],
        scratch_shapes=[pltpu.SemaphoreType.DMA(())],
        compiler_params=_cparams(("arbitrary",)),
        name="dispatch",
    )(idx, rank, pstart, x2)


def _expert_kernel(gstart_ref, nblk_ref, ntot_ref, xs_hbm, wg_ref, wu_ref, wd_ref, ys_hbm,
                   xbuf, ybuf, wg_b, wu_b, wd_b, sem_in, sem_out):
    e = pl.program_id(0)
    nb = nblk_ref[e]
    g0 = gstart_ref[e]
    total = ntot_ref[0]
    blk_rows = MOE_BLOCK * ROW_TILES
    ahead = EXP_BUFS - 1

    def rows_of(g):
        return pl.ds(pl.multiple_of(g * blk_rows, blk_rows), blk_rows)

    def in_copy(g):
        s = g % EXP_BUFS
        return pltpu.make_async_copy(xs_hbm.at[rows_of(g)], xbuf.at[s], sem_in.at[s])

    def out_copy(g):
        s = g % EXP_BUFS
        return pltpu.make_async_copy(ybuf.at[s], ys_hbm.at[rows_of(g)], sem_out.at[s])

    @pl.when(e == 0)
    def _():
        for g in range(ahead):
            @pl.when(g < total)
            def _():
                in_copy(g).start()

    @pl.when(nb > 0)
    def _():
        wg_b[...] = wg_ref[0, 0].astype(BF16)
        wu_b[...] = wu_ref[0, 0].astype(BF16)
        wd_b[...] = wd_ref[0, 0].astype(BF16)

    def body(b, carry):
        g = g0 + b
        s = g % EXP_BUFS

        @pl.when(g + ahead < total)
        def _():
            in_copy(g + ahead).start()

        in_copy(g).wait()

        @pl.when(g >= EXP_BUFS)
        def _():
            out_copy(g - EXP_BUFS).wait()

        x = jnp.concatenate([xbuf[s, pl.ds(j, MOE_BLOCK, stride=ROW_TILES), :] for j in range(ROW_TILES)], axis=-1)
        xb = x.astype(BF16)
        hid = jax.nn.silu(_dot(xb, wg_b[...])) * _dot(xb, wu_b[...])
        y = _dot(hid.astype(BF16), wd_b[...])
        for j in range(ROW_TILES):
            ybuf[s, pl.ds(j, MOE_BLOCK, stride=ROW_TILES), :] = y[:, j * LANES:(j + 1) * LANES]
        out_copy(g).start()
        return carry

    lax.fori_loop(0, nb, body, 0)

    @pl.when(e == pl.num_programs(0) - 1)
    def _():
        for back in range(EXP_BUFS):
            @pl.when(total - 1 - back >= 0)
            def _():
                out_copy(total - 1 - back).wait()


def _experts(xs2, gstart, nblk, ntot, layer, w_gate, w_up, w_down):
    n_rows = xs2.shape[0]
    _, n_exp, d, hdim = w_gate.shape
    blk_rows = MOE_BLOCK * ROW_TILES
    wmap = lambda e, gs, nb, nt: (layer, e, 0, 0)
    return pl.pallas_call(
        _expert_kernel,
        grid_spec=pltpu.PrefetchScalarGridSpec(
            num_scalar_prefetch=3,
            grid=(n_exp,),
            in_specs=[pl.BlockSpec(memory_space=pl.ANY),
                      pl.BlockSpec((1, 1, d, hdim), wmap),
                      pl.BlockSpec((1, 1, d, hdim), wmap),
                      pl.BlockSpec((1, 1, hdim, d), wmap)],
            out_specs=pl.BlockSpec(memory_space=pl.ANY),
            scratch_shapes=[pltpu.VMEM((EXP_BUFS, blk_rows, LANES), F32), pltpu.VMEM((EXP_BUFS, blk_rows, LANES), F32),
                            pltpu.VMEM((d, hdim), BF16), pltpu.VMEM((d, hdim), BF16), pltpu.VMEM((hdim, d), BF16),
                            pltpu.SemaphoreType.DMA((EXP_BUFS,)), pltpu.SemaphoreType.DMA((EXP_BUFS,))],
        ),
        out_shape=jax.ShapeDtypeStruct((n_rows, LANES), F32),
        compiler_params=_cparams(("arbitrary",)),
        name="experts",
    )(gstart, nblk, ntot, xs2, w_gate, w_up, w_down)


def _combine_kernel(slot_cur, slot_nxt, ys_hbm, wt_ref, shared_ref, h_ref, gt_ref, lng_ref, lnb_ref, o_ref,
                    buf, mixed, sem, *, alpha):
    i = pl.program_id(0)
    n = pl.num_programs(0)
    tt = CMB_TILE

    def issue(slot_ref, b):
        def body(tok, carry):
            for k in range(TOP_K):
                r = tok * TOP_K + k
                dst = buf.at[b, pl.ds(pl.multiple_of(r * ROW_TILES, ROW_TILES), ROW_TILES)]
                pltpu.make_async_copy(ys_hbm.at[slot_ref[r]], dst, sem.at[b]).start(priority=k % DMA_QUEUES)
            return carry
        lax.fori_loop(0, tt, body, 0)

    @pl.when(i == 0)
    def _():
        issue(slot_cur, 0)

    @pl.when(i + 1 < n)
    def _():
        issue(slot_nxt, (i + 1) % 2)

    cur = i % 2
    pltpu.make_async_copy(buf.at[cur], buf.at[cur], sem.at[cur]).wait()

    unroll = 4

    def mix_rows(step, carry):
        for u in range(unroll):
            tok = step * unroll + u
            first = tok * TOP_K
            acc = None
            for k in range(TOP_K):
                row = buf[cur, pl.ds(pl.multiple_of((first + k) * ROW_TILES, ROW_TILES), ROW_TILES), :]
                term = wt_ref[first + k] * row
                acc = term if acc is None else acc + term
            mixed[pl.ds(pl.multiple_of(tok * ROW_TILES, ROW_TILES), ROW_TILES), :] = acc
        return carry

    lax.fori_loop(0, tt // unroll, mix_rows, 0)
    routed = jnp.concatenate([mixed[pl.ds(j, tt, stride=ROW_TILES), :] for j in range(ROW_TILES)], axis=-1)
    f = shared_ref[...] + routed
    v = alpha * h_ref[...] + gt_ref[0] * f
    o_ref[...] = _layer_norm(v, lng_ref[...], lnb_ref[...])


def _combine(ys3, slots, wts, shared, h1, gate2, ln_g, ln_b, group_of_tile, alpha):
    t, d = h1.shape
    tt = CMB_TILE
    n = t // tt
    scale = TOK_TILE // tt
    full = lambda a: pl.BlockSpec(a.shape, lambda i: (0,) * a.ndim)
    cur = pl.BlockSpec((tt * TOP_K,), lambda i: (i,), memory_space=pltpu.SMEM)
    nxt = pl.BlockSpec((tt * TOP_K,), lambda i: (jnp.minimum(i + 1, n - 1),), memory_space=pltpu.SMEM)
    return pl.pallas_call(
        functools.partial(_combine_kernel, alpha=alpha),
        grid=(n,),
        in_specs=[cur, nxt, pl.BlockSpec(memory_space=pl.ANY), cur,
                  pl.BlockSpec((tt, d), lambda i: (i, 0)),
                  pl.BlockSpec((tt, d), lambda i: (i, 0)),
                  pl.BlockSpec((1, 1, d), lambda i: (group_of_tile(i // scale), 0, 0)),
                  full(ln_g), full(ln_b)],
        out_specs=pl.BlockSpec((tt, d), lambda i: (i, 0)),
        out_shape=jax.ShapeDtypeStruct((t, d), F32),
        scratch_shapes=[pltpu.VMEM((2, tt * TOP_K * ROW_TILES, LANES), F32), pltpu.VMEM((tt * ROW_TILES, LANES), F32),
                        pltpu.SemaphoreType.DMA((2,))],
        compiler_params=_cparams(("arbitrary",)),
        name="combine_ln2",
    )(slots, slots, ys3, wts, shared, h1, gate2, ln_g, ln_b)


def _reorder_w_in(w):
    d = w.shape[0]
    o = 0
    k = w[:, o:o + GLA_KEY]; o += GLA_KEY
    v = w[:, o:o + GLA_VAL]; o += GLA_VAL
    dec = w[:, o:o + 2 * GLA_LOWRANK]; o += 2 * GLA_LOWRANK
    q = w[:, o:o + GLA_KEY]; o += GLA_KEY
    g = w[:, o:o + GLA_VAL]; o += GLA_VAL
    conv = w[:, o:o + 3 * CONV_DIM]
    pad = jnp.zeros((d, LANES - 2 * GLA_LOWRANK), w.dtype)
    hk, hv = 2 * GLA_DK, 2 * GLA_DV
    pairs = [jnp.concatenate([v[:, i * hv:(i + 1) * hv], k[:, i * hk:(i + 1) * hk], q[:, i * hk:(i + 1) * hk]], axis=1)
             for i in range(2)]
    return jnp.concatenate([conv, g, pairs[0], pairs[1], dec, pad], axis=1).astype(BF16)


def _decay_weights(w_up, b_dec, pair):
    lo = pair * 2 * GLA_DK
    nr = 2 * GLA_LOWRANK
    outs = []
    for d in range(2):
        w = jnp.zeros((nr, LANES), F32).at[d * GLA_LOWRANK:(d + 1) * GLA_LOWRANK, :].set(w_up[d][:, lo:lo + LANES])
        w_hi = w.astype(BF16)
        w_lo = (w - w_hi.astype(F32)).astype(BF16)
        packed = jnp.concatenate([w_hi, w_hi, w_lo, jnp.zeros((LANES - 3 * nr, LANES), BF16)], axis=0)
        outs.append((packed, b_dec[d][lo:lo + LANES].reshape(1, LANES)))
    return outs[0][0], outs[1][0], outs[0][1], outs[1][1]


def _to_scan_order(a, ctx_len, rows):
    b, _, f = a.shape
    lat = a[:, ctx_len:].reshape(b, rows, GRID_W, f).swapaxes(1, 2).reshape(b, rows * GRID_W, f)
    return jnp.concatenate([a[:, :ctx_len], lat], axis=1)


def _from_scan_order(a, ctx_len, rows):
    b, _, f = a.shape
    lat = a[:, ctx_len:].reshape(b, GRID_W, rows, f).swapaxes(1, 2).reshape(b, rows * GRID_W, f)
    return jnp.concatenate([a[:, :ctx_len], lat], axis=1)


def kernel(x, c, ctx, c_ctx, w_mod, b_mod, w_in, w_conv, w_decay_up, b_decay, g_gla_norm, w_out, ln1_g, ln1_b, w_router, router_bias, w_e_gate, w_e_up, w_e_down, w_s_gate, w_s_up, w_s_down, ln2_g, ln2_b):
    bsz, seq, d = x.shape
    ctx_len = ctx.shape[1]
    depth = w_mod.shape[0]
    n_exp = w_router.shape[2]
    rows = seq // GRID_W
    l = ctx_len + seq
    t = bsz * l
    assert ctx_len == TOK_TILE and seq % TOK_TILE == 0 and d == ROW_TILES * LANES
    assert t % DSP_TILE == 0 and bsz + 1 <= SUBLANES
    tiles_per_batch = l // TOK_TILE
    alpha = float((2 * depth) ** 0.25)

    def group_of_tile(i):
        return jnp.where(i % tiles_per_batch == 0, bsz, i // tiles_per_batch)

    cond8 = jnp.zeros((SUBLANES, d), F32).at[:bsz].set(c).at[bsz].set(c_ctx)
    mod = _modulation(cond8, w_mod, b_mod).reshape(depth, SUBLANES, 6, 1, d)

    nk = t * TOP_K
    n_blocks = -(-(nk + n_exp * (MOE_BLOCK - 1)) // MOE_BLOCK)
    n_slots = n_blocks * MOE_BLOCK

    h = jnp.concatenate([ctx, x], axis=1).reshape(t, d)
    for i in range(depth):
        m = [mod[i, :, j] for j in range(6)]
        p = _inproj(h, m[0], m[1], _reorder_w_in(w_in[i]), group_of_tile)
        p3 = p.reshape(bsz, l, P_COLS)

        wf, wb, bf, bb = _decay_weights(w_decay_up[i], b_decay[i], 0)
        o_fr, o_br = _gla_pair(p3, (P_ROW + PAIR_K) // LANES, (P_ROW + PAIR_Q) // LANES, (P_ROW + PAIR_V) // (2 * GLA_DV),
                               P_DEC // LANES, wf, wb, bf, bb)
        col_in = _to_scan_order(p3[..., P_COL:], ctx_len, rows)
        wf, wb, bf, bb = _decay_weights(w_decay_up[i], b_decay[i], 1)
        o_fc, o_bc = _gla_pair(col_in, PAIR_K // LANES, PAIR_Q // LANES, PAIR_V // (2 * GLA_DV), (P_DEC - P_COL) // LANES,
                               wf, wb, bf, bb)
        o_fc = _from_scan_order(o_fc, ctx_len, rows)
        o_bc = _from_scan_order(o_bc, ctx_len, rows)

        flat = lambda a: a.reshape(t, a.shape[-1])
        h1 = _outproj(p, flat(o_fr), flat(o_br), flat(o_fc), flat(o_bc), h, m[2], w_conv[i],
                      g_gla_norm[i].reshape(1, GLA_DV), w_out[i].astype(BF16), ln1_g[i].reshape(1, d), ln1_b[i].reshape(1, d),
                      group_of_tile, alpha, tiles_per_batch)

        x3, idx, wts, rank, counts, shared = _route(
            h1, m[3], m[4], w_router[i], router_bias[i],
            w_s_gate[i].astype(BF16), w_s_up[i].astype(BF16), w_s_down[i].astype(BF16), group_of_tile)

        cnt = counts[:, 0].astype(I32)
        padded = (cnt + MOE_BLOCK - 1) // MOE_BLOCK * MOE_BLOCK
        pend = jnp.cumsum(padded)
        pstart = pend - padded
        idx, rank, wts = idx.T.reshape(nk), rank.T.reshape(nk), wts.T.reshape(nk)

        xs3, slots = _dispatch(x3, idx, rank, pstart, n_slots)
        ys2 = _experts(xs3.reshape(n_slots * ROW_TILES, LANES), pstart // MOE_BLOCK, padded // MOE_BLOCK,
                       (pend[-1] // MOE_BLOCK).reshape(1), i, w_e_gate, w_e_up, w_e_down)
        h = _combine(ys2.reshape(n_slots, ROW_TILES, LANES), slots, wts, shared, h1, m[5],
                     ln2_g[i].reshape(1, d), ln2_b[i].reshape(1, d), group_of_tile, alpha)

    return h.reshape(bsz, l, d)[:, ctx_len:]
```

```python
import functools

import jax
import jax.numpy as jnp
from jax import lax
from jax.experimental import pallas as pl
from jax.experimental.pallas import tpu as pltpu

F32 = jnp.float32
BF16 = jnp.bfloat16
I32 = jnp.int32

GRID_W = 64
CONV_DIM = 512
GLA_HEADS = 4
GLA_DK = 64
GLA_DV = 128
GLA_KEY = GLA_HEADS * GLA_DK
GLA_VAL = GLA_HEADS * GLA_DV
GLA_LOWRANK = 16
GLA_GATE_TEMP = 16.0
GLA_CHUNK = 64
TOP_K = 8
N_GROUPS = 8
TOPK_GROUPS = 4
ROUTED_SCALE = 2.5
LN_EPS = 1e-5
RMS_EPS = 1e-6

LANES = 128
SUBLANES = 8
VMEM_LIMIT = 56 * 1024 * 1024

P_CONV = 0
P_G = 1536
P_ROW = 2048
P_COL = 2560
P_DEC = 3072
P_COLS = 3200
PAIR_V, PAIR_K, PAIR_Q = 0, 256, 384

GLA_BATCH = 2
TOK_TILE = 256
MOE_BLOCK = 256
EXP_BUFS = 4
CMB_TILE = 256
DSP_TILE = 512
ROW_TILES = 8
DMA_QUEUES = 2


def _cparams(sem):
    return pltpu.CompilerParams(dimension_semantics=sem, vmem_limit_bytes=VMEM_LIMIT)


def _mod_kernel(c_ref, w_ref, b_ref, o_ref):
    c = c_ref[...]
    a = c * jax.nn.sigmoid(c)
    o_ref[0] = jnp.dot(a, w_ref[0], preferred_element_type=F32, precision=lax.Precision.HIGHEST) + b_ref[0]


def _modulation(cond8, w_mod, b_mod):
    depth, d, n = w_mod.shape
    tn = 1536
    return pl.pallas_call(
        _mod_kernel,
        grid=(depth, n // tn),
        in_specs=[
            pl.BlockSpec((SUBLANES, d), lambda i, j: (0, 0)),
            pl.BlockSpec((1, d, tn), lambda i, j: (i, 0, j)),
            pl.BlockSpec((1, 1, tn), lambda i, j: (i, 0, j)),
        ],
        out_specs=pl.BlockSpec((1, SUBLANES, tn), lambda i, j: (i, 0, j)),
        out_shape=jax.ShapeDtypeStruct((depth, SUBLANES, n), F32),
        compiler_params=_cparams(("arbitrary", "arbitrary")),
        name="adaln_mod",
    )(cond8, w_mod, b_mod.reshape(depth, 1, n))


def _inproj_kernel(x_ref, sh_ref, sc_ref, w_ref, o_ref):
    h = x_ref[...] * (1.0 + sc_ref[0]) + sh_ref[0]
    o_ref[...] = jnp.dot(h.astype(BF16), w_ref[...], preferred_element_type=F32)


def _inproj(h, shift, scale, w, group_of_tile):
    t, d = h.shape
    n = w.shape[1]
    grp = lambda i: (group_of_tile(i), 0, 0)
    return pl.pallas_call(
        _inproj_kernel,
        grid=(t // TOK_TILE,),
        in_specs=[
            pl.BlockSpec((TOK_TILE, d), lambda i: (i, 0)),
            pl.BlockSpec((1, 1, d), grp),
            pl.BlockSpec((1, 1, d), grp),
            pl.BlockSpec((d, n), lambda i: (0, 0)),
        ],
        out_specs=pl.BlockSpec((TOK_TILE, n), lambda i: (i, 0)),
        out_shape=jax.ShapeDtypeStruct((t, n), F32),
        compiler_params=_cparams(("arbitrary",)),
        name="inproj",
    )(h, shift, scale, w)


def _dot(a, b):
    return jnp.dot(a, b, preferred_element_type=F32)


def _dot_nt(a, b):
    return lax.dot_general(a, b, (((1,), (1,)), ((), ())), preferred_element_type=F32)


def _gla_direction(k2, q2, v2, dec, wup, bdec, sums, tri, s_ref, o_ref, bi, reverse):
    t = TOK_TILE
    c = GLA_CHUNK
    d_hi = dec.astype(BF16).astype(F32)
    packed = d_hi + pltpu.roll(dec - d_hi, 2 * GLA_LOWRANK, 1) + pltpu.roll(d_hi, 4 * GLA_LOWRANK, 1)
    z = _dot(packed.astype(BF16), wup) + bdec
    la = (jnp.minimum(z, 0.0) - jnp.log(1.0 + jnp.exp(-jnp.abs(z)))) * (1.0 / GLA_GATE_TEMP)

    la_hi = la.astype(BF16)
    la_lo = (la - la_hi.astype(F32)).astype(BF16)
    acc = _dot(sums, jnp.concatenate([la_hi, la_lo], axis=-1))
    bcum = acc[:t, :LANES] + acc[:t, LANES:]
    btot = acc[t:, :LANES] + acc[t:, LANES:]

    q_dec = q2 * jnp.exp(bcum) * (GLA_DK ** -0.5)
    k_inv = (k2 * jnp.exp(-bcum)).astype(BF16)
    k_tail = k2 * jnp.exp(btot - bcum)
    chunk_decay = jnp.exp(btot)

    lane = lax.broadcasted_iota(I32, (t, 2 * GLA_DK), 1)
    keep = tri > 0.0
    order = range(t // c - 1, -1, -1) if reverse else range(t // c)
    for hh in range(2):
        in_head = (lane // GLA_DK) == hh
        qd = jnp.where(in_head, q_dec, 0.0).astype(BF16)
        kt = jnp.where(in_head, k_tail, 0.0).astype(BF16)
        vh = v2[:, hh * GLA_DV:(hh + 1) * GLA_DV]
        vh_b = vh.astype(BF16)
        att = jnp.where(keep, _dot_nt(qd, k_inv), 0.0).astype(BF16)
        o_intra = _dot(att, vh_b)
        s = s_ref[bi, hh]
        for ci in order:
            rows = slice(ci * c, (ci + 1) * c)
            o_ref[bi, rows, hh * GLA_DV:(hh + 1) * GLA_DV] = o_intra[rows] + _dot_nt(qd[rows], s.astype(BF16))
            kv_t = _dot(vh[rows].T.astype(BF16), kt[rows])
            s = s * chunk_decay[ci * c:ci * c + 1, :] + kv_t
        s_ref[bi, hh] = s


def _gla_kernel(kf, qf, vf, df, kb, qb, vb, db, wupf, wupb, bf, bb, sums_f, sums_b, tri_f, tri_b,
                of_ref, ob_ref, sf_ref, sb_ref):
    @pl.when(pl.program_id(1) == 0)
    def _():
        sf_ref[...] = jnp.zeros_like(sf_ref)
        sb_ref[...] = jnp.zeros_like(sb_ref)

    for bi in range(GLA_BATCH):
        _gla_direction(kf[bi], qf[bi], vf[bi], df[bi], wupf[...], bf[...], sums_f[...], tri_f[...], sf_ref, of_ref, bi, False)
        _gla_direction(kb[bi], qb[bi], vb[bi], db[bi], wupb[...], bb[...], sums_b[...], tri_b[...], sb_ref, ob_ref, bi, True)


def _chunk_matrices():
    t = TOK_TILE
    r = jnp.arange(t)[:, None]
    q = jnp.arange(t)[None, :]
    same = (r // GLA_CHUNK) == (q // GLA_CHUNK)
    out = []
    for tri in (same & (q <= r), same & (q >= r)):
        out.append((jnp.concatenate([tri, same], axis=0).astype(BF16), tri.astype(F32)))
    return out[0][0], out[1][0], out[0][1], out[1][1]


def _gla_pair(src, col_k, col_q, col_v, col_dec, wupf, wupb, bf, bb):
    bsz, l, _ = src.shape
    nb = l // TOK_TILE
    fwd = lambda j: j
    bwd = lambda j: jnp.where(j == 0, 0, nb - j)

    def spec(width, colidx, order):
        return pl.BlockSpec((GLA_BATCH, TOK_TILE, width), lambda b, j: (b, order(j), colidx))

    full = lambda a: pl.BlockSpec(a.shape, lambda b, j: (0,) * a.ndim)
    consts = _chunk_matrices()
    out = jax.ShapeDtypeStruct((bsz, l, 2 * GLA_DV), F32)
    return pl.pallas_call(
        _gla_kernel,
        grid=(bsz // GLA_BATCH, nb),
        in_specs=[spec(LANES, col_k, fwd), spec(LANES, col_q, fwd), spec(2 * GLA_DV, col_v, fwd), spec(LANES, col_dec, fwd),
                  spec(LANES, col_k, bwd), spec(LANES, col_q, bwd), spec(2 * GLA_DV, col_v, bwd), spec(LANES, col_dec, bwd),
                  full(wupf), full(wupb), full(bf), full(bb)] + [full(a) for a in consts],
        out_specs=[pl.BlockSpec((GLA_BATCH, TOK_TILE, 2 * GLA_DV), lambda b, j: (b, fwd(j), 0)),
                   pl.BlockSpec((GLA_BATCH, TOK_TILE, 2 * GLA_DV), lambda b, j: (b, bwd(j), 0))],
        out_shape=[out, out],
        scratch_shapes=[pltpu.VMEM((GLA_BATCH, 2, GLA_DV, LANES), F32), pltpu.VMEM((GLA_BATCH, 2, GLA_DV, LANES), F32)],
        compiler_params=_cparams(("arbitrary", "arbitrary")),
        name="gla_pair",
    )(src, src, src, src, src, src, src, src, wupf, wupb, bf, bb, *consts)


def _layer_norm(v, g, b):
    mu = jnp.mean(v, axis=-1, keepdims=True)
    dv = v - mu
    var = jnp.mean(dv * dv, axis=-1, keepdims=True)
    return dv * lax.rsqrt(var + LN_EPS) * g + b


def _outproj_kernel(conv_ref, gate_ref, ofr, obr, ofc, obc, h_ref, gt_ref, wconv_ref, gn_ref, wout_ref,
                    lng_ref, lnb_ref, o_ref, *, alpha, tiles_per_batch):
    tm = TOK_TILE
    is_ctx = (pl.program_id(0) % tiles_per_batch) == 0
    seg = jnp.where(is_ctx, tm, GRID_W)

    cv = conv_ref[...]
    b_gate = cv[:, :CONV_DIM]
    u = cv[:, CONV_DIM:2 * CONV_DIM] * cv[:, 2 * CONV_DIM:]
    pos = lax.broadcasted_iota(I32, (tm, CONV_DIM), 0) & (seg - 1)
    prev = jnp.where(pos == 0, 0.0, pltpu.roll(u, 1, 0))
    nxt = jnp.where(pos == seg - 1, 0.0, pltpu.roll(u, tm - 1, 0))
    wc = wconv_ref[...]
    mix_conv = b_gate * (wc[0:1, :] * prev + wc[1:2, :] * u + wc[2:3, :] * nxt)

    o_r = ofr[...] + obr[...]
    o_c = ofc[...] + obc[...]
    gate = gate_ref[...]
    gn = gn_ref[...]
    parts = [mix_conv.astype(BF16)]
    for hd in range(GLA_HEADS):
        src = o_r if hd < 2 else o_c
        o = src[:, (hd % 2) * GLA_DV:(hd % 2 + 1) * GLA_DV]
        o = o * lax.rsqrt(jnp.mean(o * o, axis=-1, keepdims=True) + RMS_EPS) * gn
        g = gate[:, hd * GLA_DV:(hd + 1) * GLA_DV]
        parts.append((o * (g * jax.nn.sigmoid(g))).astype(BF16))
    mix = jnp.concatenate(parts, axis=-1)
    y = _dot(mix, wout_ref[...])
    v = alpha * h_ref[...] + gt_ref[0] * y
    o_ref[...] = _layer_norm(v, lng_ref[...], lnb_ref[...])


def _outproj(p, o_fr, o_br, o_fc, o_bc, h, gate1, w_conv, g_norm, w_out, ln_g, ln_b, group_of_tile, alpha, tiles_per_batch):
    t, d = h.shape
    tm = TOK_TILE
    tok = lambda width, colidx: pl.BlockSpec((tm, width), lambda i: (i, colidx))
    full = lambda a: pl.BlockSpec(a.shape, lambda i: (0,) * a.ndim)
    return pl.pallas_call(
        functools.partial(_outproj_kernel, alpha=alpha, tiles_per_batch=tiles_per_batch),
        grid=(t // tm,),
        in_specs=[tok(3 * CONV_DIM, P_CONV // (3 * CONV_DIM)), tok(GLA_VAL, P_G // GLA_VAL),
                  tok(2 * GLA_DV, 0), tok(2 * GLA_DV, 0), tok(2 * GLA_DV, 0), tok(2 * GLA_DV, 0),
                  tok(d, 0), pl.BlockSpec((1, 1, d), lambda i: (group_of_tile(i), 0, 0)),
                  full(w_conv), full(g_norm), full(w_out), full(ln_g), full(ln_b)],
        out_specs=pl.BlockSpec((tm, d), lambda i: (i, 0)),
        out_shape=jax.ShapeDtypeStruct((t, d), F32),
        compiler_params=_cparams(("arbitrary",)),
        name="outproj_ln1",
    )(p, p, o_fr, o_br, o_fc, o_bc, h, gate1, w_conv, g_norm, w_out, ln_g, ln_b)


def _route_kernel(h_ref, sh_ref, sc_ref, wrt_hi_ref, wrt_lo_ref, rbt_ref, wsg_ref, wsu_ref, wsd_ref, upper_ref,
                  x3_ref, idx_ref, wt_ref, rank_ref, cnt_ref, shared_ref, base_ref):
    tm = TOK_TILE
    n_exp = rbt_ref.shape[0]
    gsz = n_exp // N_GROUPS

    @pl.when(pl.program_id(0) == 0)
    def _():
        base_ref[...] = jnp.zeros_like(base_ref)

    u = h_ref[...] * (1.0 + sc_ref[0]) + sh_ref[0]
    for j in range(ROW_TILES):
        x3_ref[pl.ds(j, tm, stride=ROW_TILES), :] = u[:, j * LANES:(j + 1) * LANES]

    u_hi = u.astype(BF16)
    u_lo = (u - u_hi.astype(F32)).astype(BF16)
    shared_ref[...] = _dot((jax.nn.silu(_dot(u_hi, wsg_ref[...])) * _dot(u_hi, wsu_ref[...])).astype(BF16), wsd_ref[...])

    wrt_hi = wrt_hi_ref[...]
    logits = _dot_nt(wrt_hi, u_hi) + _dot_nt(wrt_hi, u_lo) + _dot_nt(wrt_lo_ref[...], u_hi)
    s = jax.nn.sigmoid(logits)
    biased = s + rbt_ref[...]

    neg = -jnp.inf
    eidx = lax.broadcasted_iota(I32, (n_exp, tm), 0)
    loc = lax.broadcasted_iota(I32, (gsz, tm), 0)
    gscore = []
    for g in range(N_GROUPS):
        vals = biased[g * gsz:(g + 1) * gsz, :]
        m1 = jnp.max(vals, axis=0, keepdims=True)
        i1 = jnp.min(jnp.where(vals == m1, loc, gsz), axis=0, keepdims=True)
        m2 = jnp.max(jnp.where(loc == i1, neg, vals), axis=0, keepdims=True)
        gscore.append(m1 + m2)
    pieces = []
    for g in range(N_GROUPS):
        ahead = jnp.zeros((1, tm), F32)
        for g2 in range(N_GROUPS):
            if g2 == g:
                continue
            wins = (gscore[g2] >= gscore[g]) if g2 < g else (gscore[g2] > gscore[g])
            ahead = ahead + jnp.where(wins, 1.0, 0.0)
        pieces.append(jnp.where(ahead < TOPK_GROUPS, biased[g * gsz:(g + 1) * gsz, :], neg))
    masked = jnp.concatenate(pieces, axis=0)

    onehot = jnp.zeros((n_exp, tm), F32)
    idxs = []
    for _ in range(TOP_K):
        m = jnp.max(masked, axis=0, keepdims=True)
        i = jnp.min(jnp.where(masked == m, eidx, n_exp), axis=0, keepdims=True)
        hit = eidx == i
        onehot = jnp.where(hit, 1.0, onehot)
        masked = jnp.where(hit, neg, masked)
        idxs.append(i)
    sel = onehot * s
    wnorm = sel / jnp.sum(sel, axis=0, keepdims=True) * ROUTED_SCALE

    onehot_b = onehot.astype(BF16)
    upper = upper_ref[...]
    rank_all = _dot(onehot_b, upper) + base_ref[...]
    base_new = base_ref[...] + _dot(onehot_b, jnp.ones((tm, tm), BF16))
    base_ref[...] = base_new
    cnt_ref[...] = base_new[:, :LANES]

    wts, ranks = [], []
    for k in range(TOP_K):
        hit = eidx == idxs[k]
        wts.append(jnp.sum(jnp.where(hit, wnorm, 0.0), axis=0, keepdims=True))
        ranks.append(jnp.sum(jnp.where(hit, rank_all, 0.0), axis=0, keepdims=True))
    idx_ref[...] = jnp.concatenate(idxs, axis=0)
    wt_ref[...] = jnp.concatenate(wts, axis=0)
    rank_ref[...] = jnp.concatenate(ranks, axis=0).astype(I32)


def _route(h1, shift, scale, wr, rbias, wsg, wsu, wsd, group_of_tile):
    t, d = h1.shape
    tm = TOK_TILE
    n_exp = wr.shape[1]
    wrt = wr.T
    wrt_hi = wrt.astype(BF16)
    wrt_lo = (wrt - wrt_hi.astype(F32)).astype(BF16)
    rbt = jnp.broadcast_to(rbias.reshape(n_exp, 1), (n_exp, tm))
    upper = (jnp.arange(tm)[:, None] < jnp.arange(tm)[None, :]).astype(BF16)
    grp = lambda i: (group_of_tile(i), 0, 0)
    full = lambda a: pl.BlockSpec(a.shape, lambda i: (0,) * a.ndim)
    choice = pl.BlockSpec((TOP_K, tm), lambda i: (0, i))
    choice_out = lambda dt: jax.ShapeDtypeStruct((TOP_K, t), dt)
    return pl.pallas_call(
        _route_kernel,
        grid=(t // tm,),
        in_specs=[pl.BlockSpec((tm, d), lambda i: (i, 0)), pl.BlockSpec((1, 1, d), grp), pl.BlockSpec((1, 1, d), grp),
                  full(wrt_hi), full(wrt_lo), full(rbt), full(wsg), full(wsu), full(wsd), full(upper)],
        out_specs=[pl.BlockSpec((tm * ROW_TILES, LANES), lambda i: (i, 0)), choice, choice, choice,
                   pl.BlockSpec((n_exp, LANES), lambda i: (0, 0)),
                   pl.BlockSpec((tm, d), lambda i: (i, 0))],
        out_shape=[jax.ShapeDtypeStruct((t * ROW_TILES, LANES), F32), choice_out(I32), choice_out(F32), choice_out(I32),
                   jax.ShapeDtypeStruct((n_exp, LANES), F32), jax.ShapeDtypeStruct((t, d), F32)],
        scratch_shapes=[pltpu.VMEM((n_exp, tm), F32)],
        compiler_params=_cparams(("arbitrary",)),
        name="route_shared",
    )(h1, shift, scale, wrt_hi, wrt_lo, rbt, wsg, wsu, wsd, upper)


def _dispatch_kernel(idx_ref, rank_ref, pstart_ref, x_ref, xs_hbm, slot_ref, sem):
    def body(i, carry):
        src = x_ref.at[pl.ds(pl.multiple_of(i * ROW_TILES, ROW_TILES), ROW_TILES)]
        for k in range(TOP_K):
            r = i * TOP_K + k
            slot = pstart_ref[idx_ref[r]] + rank_ref[r]
            slot_ref[r] = slot
            pltpu.make_async_copy(src, xs_hbm.at[slot], sem).start(priority=k % DMA_QUEUES)
        return carry

    lax.fori_loop(0, DSP_TILE, body, 0)
    done = xs_hbm.at[pl.ds(0, DSP_TILE * TOP_K)]
    pltpu.make_async_copy(done, done, sem).wait()


def _dispatch(x2, idx, rank, pstart, n_slots):
    t = x2.shape[0] // ROW_TILES
    choice = pl.BlockSpec((DSP_TILE * TOP_K,), lambda i: (i,), memory_space=pltpu.SMEM)
    return pl.pallas_call(
        _dispatch_kernel,
        grid=(t // DSP_TILE,),
        in_specs=[choice, choice, pl.BlockSpec(memory_space=pltpu.SMEM),
                  pl.BlockSpec((DSP_TILE * ROW_TILES, LANES), lambda i: (i, 0))],
        out_specs=[pl.BlockSpec(memory_space=pl.ANY), choice],
        out_shape=[jax.ShapeDtypeStruct((n_slots, ROW_TILES, LANES), F32), jax.ShapeDtypeStruct(idx.shape, I32)],
        scratch_shapes=[pltpu.SemaphoreType.DMA(())],
        compiler_params=_cparams(("arbitrary",)),
        name="dispatch",
    )(idx, rank, pstart, x2)


def _expert_kernel(gstart_ref, nblk_ref, ntot_ref, xs_hbm, wg_ref, wu_ref, wd_ref, ys_hbm,
                   xbuf, ybuf, wg_b, wu_b, wd_b, sem_in, sem_out):
    e = pl.program_id(0)
    nb = nblk_ref[e]
    g0 = gstart_ref[e]
    total = ntot_ref[0]
    blk_rows = MOE_BLOCK * ROW_TILES
    ahead = EXP_BUFS - 1

    def rows_of(g):
        return pl.ds(pl.multiple_of(g * blk_rows, blk_rows), blk_rows)

    def in_copy(g):
        s = g % EXP_BUFS
        return pltpu.make_async_copy(xs_hbm.at[rows_of(g)], xbuf.at[s], sem_in.at[s])

    def out_copy(g):
        s = g % EXP_BUFS
        return pltpu.make_async_copy(ybuf.at[s], ys_hbm.at[rows_of(g)], sem_out.at[s])

    @pl.when(e == 0)
    def _():
        for g in range(ahead):
            @pl.when(g < total)
            def _():
                in_copy(g).start()

    @pl.when(nb > 0)
    def _():
        wg_b[...] = wg_ref[0, 0].astype(BF16)
        wu_b[...] = wu_ref[0, 0].astype(BF16)
        wd_b[...] = wd_ref[0, 0].astype(BF16)

    def body(b, carry):
        g = g0 + b
        s = g % EXP_BUFS

        @pl.when(g + ahead < total)
        def _():
            in_copy(g + ahead).start()

        in_copy(g).wait()

        @pl.when(g >= EXP_BUFS)
        def _():
            out_copy(g - EXP_BUFS).wait()

        x = jnp.concatenate([xbuf[s, pl.ds(j, MOE_BLOCK, stride=ROW_TILES), :] for j in range(ROW_TILES)], axis=-1)
        xb = x.astype(BF16)
        hid = jax.nn.silu(_dot(xb, wg_b[...])) * _dot(xb, wu_b[...])
        y = _dot(hid.astype(BF16), wd_b[...])
        for j in range(ROW_TILES):
            ybuf[s, pl.ds(j, MOE_BLOCK, stride=ROW_TILES), :] = y[:, j * LANES:(j + 1) * LANES]
        out_copy(g).start()
        return carry

    lax.fori_loop(0, nb, body, 0)

    @pl.when(e == pl.num_programs(0) - 1)
    def _():
        for back in range(EXP_BUFS):
            @pl.when(total - 1 - back >= 0)
            def _():
                out_copy(total - 1 - back).wait()


def _experts(xs2, gstart, nblk, ntot, layer, w_gate, w_up, w_down):
    n_rows = xs2.shape[0]
    _, n_exp, d, hdim = w_gate.shape
    blk_rows = MOE_BLOCK * ROW_TILES
    wmap = lambda e, gs, nb, nt: (layer, e, 0, 0)
    return pl.pallas_call(
        _expert_kernel,
        grid_spec=pltpu.PrefetchScalarGridSpec(
            num_scalar_prefetch=3,
            grid=(n_exp,),
            in_specs=[pl.BlockSpec(memory_space=pl.ANY),
                      pl.BlockSpec((1, 1, d, hdim), wmap),
                      pl.BlockSpec((1, 1, d, hdim), wmap),
                      pl.BlockSpec((1, 1, hdim, d), wmap)],
            out_specs=pl.BlockSpec(memory_space=pl.ANY),
            scratch_shapes=[pltpu.VMEM((EXP_BUFS, blk_rows, LANES), F32), pltpu.VMEM((EXP_BUFS, blk_rows, LANES), F32),
                            pltpu.VMEM((d, hdim), BF16), pltpu.VMEM((d, hdim), BF16), pltpu.VMEM((hdim, d), BF16),
                            pltpu.SemaphoreType.DMA((EXP_BUFS,)), pltpu.SemaphoreType.DMA((EXP_BUFS,))],
        ),
        out_shape=jax.ShapeDtypeStruct((n_rows, LANES), F32),
        compiler_params=_cparams(("arbitrary",)),
        name="experts",
    )(gstart, nblk, ntot, xs2, w_gate, w_up, w_down)


def _combine_kernel(slot_cur, slot_nxt, ys_hbm, wt_ref, shared_ref, h_ref, gt_ref, lng_ref, lnb_ref, o_ref,
                    buf, mixed, sem, *, alpha):
    i = pl.program_id(0)
    n = pl.num_programs(0)
    tt = CMB_TILE

    cur = i % 2
    unroll = 4

    def issue_token(slot_ref, b, tok):
        for k in range(TOP_K):
            r = tok * TOP_K + k
            dst = buf.at[b, pl.ds(pl.multiple_of(r * ROW_TILES, ROW_TILES), ROW_TILES)]
            pltpu.make_async_copy(ys_hbm.at[slot_ref[r]], dst, sem.at[b]).start(priority=k % DMA_QUEUES)

    def mix_token(tok):
        first = tok * TOP_K
        acc = None
        for k in range(TOP_K):
            row = buf[cur, pl.ds(pl.multiple_of((first + k) * ROW_TILES, ROW_TILES), ROW_TILES), :]
            term = wt_ref[first + k] * row
            acc = term if acc is None else acc + term
        mixed[pl.ds(pl.multiple_of(tok * ROW_TILES, ROW_TILES), ROW_TILES), :] = acc

    @pl.when(i == 0)
    def _():
        def body(tok, carry):
            issue_token(slot_cur, 0, tok)
            return carry
        lax.fori_loop(0, tt, body, 0)

    pltpu.make_async_copy(buf.at[cur], buf.at[cur], sem.at[cur]).wait()

    @pl.when(i + 1 < n)
    def _():
        def body(step, carry):
            for u in range(unroll):
                issue_token(slot_nxt, (i + 1) % 2, step * unroll + u)
                mix_token(step * unroll + u)
            return carry
        lax.fori_loop(0, tt // unroll, body, 0)

    @pl.when(i + 1 == n)
    def _():
        def body(step, carry):
            for u in range(unroll):
                mix_token(step * unroll + u)
            return carry
        lax.fori_loop(0, tt // unroll, body, 0)

    routed = jnp.concatenate([mixed[pl.ds(j, tt, stride=ROW_TILES), :] for j in range(ROW_TILES)], axis=-1)
    f = shared_ref[...] + routed
    v = alpha * h_ref[...] + gt_ref[0] * f
    o_ref[...] = _layer_norm(v, lng_ref[...], lnb_ref[...])


def _combine(ys3, slots, wts, shared, h1, gate2, ln_g, ln_b, group_of_tile, alpha, latent_only_tiles=None):
    t, d = h1.shape
    tt = CMB_TILE
    n = t // tt
    scale = TOK_TILE // tt
    if latent_only_tiles is None:
        out_rows, out_block = t, lambda i: (i, 0)
    else:
        assert tt == TOK_TILE
        per = latent_only_tiles
        out_rows = t - (n // per) * tt
        out_block = lambda i: ((i // per) * (per - 1) + jnp.maximum(i % per - 1, 0), 0)
    full = lambda a: pl.BlockSpec(a.shape, lambda i: (0,) * a.ndim)
    cur = pl.BlockSpec((tt * TOP_K,), lambda i: (i,), memory_space=pltpu.SMEM)
    nxt = pl.BlockSpec((tt * TOP_K,), lambda i: (jnp.minimum(i + 1, n - 1),), memory_space=pltpu.SMEM)
    return pl.pallas_call(
        functools.partial(_combine_kernel, alpha=alpha),
        grid=(n,),
        in_specs=[cur, nxt, pl.BlockSpec(memory_space=pl.ANY), cur,
                  pl.BlockSpec((tt, d), lambda i: (i, 0)),
                  pl.BlockSpec((tt, d), lambda i: (i, 0)),
                  pl.BlockSpec((1, 1, d), lambda i: (group_of_tile(i // scale), 0, 0)),
                  full(ln_g), full(ln_b)],
        out_specs=pl.BlockSpec((tt, d), out_block),
        out_shape=jax.ShapeDtypeStruct((out_rows, d), F32),
        scratch_shapes=[pltpu.VMEM((2, tt * TOP_K * ROW_TILES, LANES), F32), pltpu.VMEM((tt * ROW_TILES, LANES), F32),
                        pltpu.SemaphoreType.DMA((2,))],
        compiler_params=_cparams(("arbitrary",)),
        name="combine_ln2",
    )(slots, slots, ys3, wts, shared, h1, gate2, ln_g, ln_b)


def _reorder_w_in(w):
    d = w.shape[0]
    o = 0
    k = w[:, o:o + GLA_KEY]; o += GLA_KEY
    v = w[:, o:o + GLA_VAL]; o += GLA_VAL
    dec = w[:, o:o + 2 * GLA_LOWRANK]; o += 2 * GLA_LOWRANK
    q = w[:, o:o + GLA_KEY]; o += GLA_KEY
    g = w[:, o:o + GLA_VAL]; o += GLA_VAL
    conv = w[:, o:o + 3 * CONV_DIM]
    pad = jnp.zeros((d, LANES - 2 * GLA_LOWRANK), w.dtype)
    hk, hv = 2 * GLA_DK, 2 * GLA_DV
    pairs = [jnp.concatenate([v[:, i * hv:(i + 1) * hv], k[:, i * hk:(i + 1) * hk], q[:, i * hk:(i + 1) * hk]], axis=1)
             for i in range(2)]
    return jnp.concatenate([conv, g, pairs[0], pairs[1], dec, pad], axis=1).astype(BF16)


def _decay_weights(w_up, b_dec, pair):
    lo = pair * 2 * GLA_DK
    nr = 2 * GLA_LOWRANK
    outs = []
    for d in range(2):
        w = jnp.zeros((nr, LANES), F32).at[d * GLA_LOWRANK:(d + 1) * GLA_LOWRANK, :].set(w_up[d][:, lo:lo + LANES])
        w_hi = w.astype(BF16)
        w_lo = (w - w_hi.astype(F32)).astype(BF16)
        packed = jnp.concatenate([w_hi, w_hi, w_lo, jnp.zeros((LANES - 3 * nr, LANES), BF16)], axis=0)
        outs.append((packed, b_dec[d][lo:lo + LANES].reshape(1, LANES)))
    return outs[0][0], outs[1][0], outs[0][1], outs[1][1]


def _to_scan_order(a, ctx_len, rows):
    b, _, f = a.shape
    lat = a[:, ctx_len:].reshape(b, rows, GRID_W, f).swapaxes(1, 2).reshape(b, rows * GRID_W, f)
    return jnp.concatenate([a[:, :ctx_len], lat], axis=1)


def _from_scan_order(a, ctx_len, rows):
    b, _, f = a.shape
    lat = a[:, ctx_len:].reshape(b, GRID_W, rows, f).swapaxes(1, 2).reshape(b, rows * GRID_W, f)
    return jnp.concatenate([a[:, :ctx_len], lat], axis=1)


def kernel(x, c, ctx, c_ctx, w_mod, b_mod, w_in, w_conv, w_decay_up, b_decay, g_gla_norm, w_out, ln1_g, ln1_b, w_router, router_bias, w_e_gate, w_e_up, w_e_down, w_s_gate, w_s_up, w_s_down, ln2_g, ln2_b):
    bsz, seq, d = x.shape
    ctx_len = ctx.shape[1]
    depth = w_mod.shape[0]
    n_exp = w_router.shape[2]
    rows = seq // GRID_W
    l = ctx_len + seq
    t = bsz * l
    assert ctx_len == TOK_TILE and seq % TOK_TILE == 0 and d == ROW_TILES * LANES
    assert t % DSP_TILE == 0 and bsz + 1 <= SUBLANES and bsz % GLA_BATCH == 0
    tiles_per_batch = l // TOK_TILE
    alpha = float((2 * depth) ** 0.25)

    def group_of_tile(i):
        return jnp.where(i % tiles_per_batch == 0, bsz, i // tiles_per_batch)

    cond8 = jnp.zeros((SUBLANES, d), F32).at[:bsz].set(c).at[bsz].set(c_ctx)
    mod = _modulation(cond8, w_mod, b_mod).reshape(depth, SUBLANES, 6, 1, d)

    nk = t * TOP_K
    n_blocks = -(-(nk + n_exp * (MOE_BLOCK - 1)) // MOE_BLOCK)
    n_slots = n_blocks * MOE_BLOCK

    h = jnp.concatenate([ctx, x], axis=1).reshape(t, d)
    for i in range(depth):
        m = [mod[i, :, j] for j in range(6)]
        p = _inproj(h, m[0], m[1], _reorder_w_in(w_in[i]), group_of_tile)
        p3 = p.reshape(bsz, l, P_COLS)

        wf, wb, bf, bb = _decay_weights(w_decay_up[i], b_decay[i], 0)
        o_fr, o_br = _gla_pair(p3, (P_ROW + PAIR_K) // LANES, (P_ROW + PAIR_Q) // LANES, (P_ROW + PAIR_V) // (2 * GLA_DV),
                               P_DEC // LANES, wf, wb, bf, bb)
        col_in = _to_scan_order(p3[..., P_COL:], ctx_len, rows)
        wf, wb, bf, bb = _decay_weights(w_decay_up[i], b_decay[i], 1)
        o_fc, o_bc = _gla_pair(col_in, PAIR_K // LANES, PAIR_Q // LANES, PAIR_V // (2 * GLA_DV), (P_DEC - P_COL) // LANES,
                               wf, wb, bf, bb)
        o_fc = _from_scan_order(o_fc, ctx_len, rows)
        o_bc = _from_scan_order(o_bc, ctx_len, rows)

        flat = lambda a: a.reshape(t, a.shape[-1])
        h1 = _outproj(p, flat(o_fr), flat(o_br), flat(o_fc), flat(o_bc), h, m[2], w_conv[i],
                      g_gla_norm[i].reshape(1, GLA_DV), w_out[i].astype(BF16), ln1_g[i].reshape(1, d), ln1_b[i].reshape(1, d),
                      group_of_tile, alpha, tiles_per_batch)

        x3, idx, wts, rank, counts, shared = _route(
            h1, m[3], m[4], w_router[i], router_bias[i],
            w_s_gate[i].astype(BF16), w_s_up[i].astype(BF16), w_s_down[i].astype(BF16), group_of_tile)

        cnt = counts[:, 0].astype(I32)
        padded = (cnt + MOE_BLOCK - 1) // MOE_BLOCK * MOE_BLOCK
        pend = jnp.cumsum(padded)
        pstart = pend - padded
        idx, rank, wts = idx.T.reshape(nk), rank.T.reshape(nk), wts.T.reshape(nk)

        xs3, slots = _dispatch(x3, idx, rank, pstart, n_slots)
        ys2 = _experts(xs3.reshape(n_slots * ROW_TILES, LANES), pstart // MOE_BLOCK, padded // MOE_BLOCK,
                       (pend[-1] // MOE_BLOCK).reshape(1), i, w_e_gate, w_e_up, w_e_down)
        h = _combine(ys2.reshape(n_slots, ROW_TILES, LANES), slots, wts, shared, h1, m[5],
                     ln2_g[i].reshape(1, d), ln2_b[i].reshape(1, d), group_of_tile, alpha,
                     latent_only_tiles=tiles_per_batch if i == depth - 1 else None)

    return h.reshape(bsz, seq, d)
```

```python
import functools

import jax
import jax.numpy as jnp
from jax import lax
from jax.experimental import pallas as pl
from jax.experimental.pallas import tpu as pltpu

F32 = jnp.float32
BF16 = jnp.bfloat16
I32 = jnp.int32

GRID_W = 64
CONV_DIM = 512
GLA_HEADS = 4
GLA_DK = 64
GLA_DV = 128
GLA_KEY = GLA_HEADS * GLA_DK
GLA_VAL = GLA_HEADS * GLA_DV
GLA_LOWRANK = 16
GLA_GATE_TEMP = 16.0
GLA_CHUNK = 64
TOP_K = 8
N_GROUPS = 8
TOPK_GROUPS = 4
ROUTED_SCALE = 2.5
LN_EPS = 1e-5
RMS_EPS = 1e-6

LANES = 128
SUBLANES = 8
VMEM_LIMIT = 56 * 1024 * 1024

P_CONV = 0
P_G = 1536
P_ROW = 2048
P_COL = 2560
P_DEC = 3072
P_COLS = 3200
PAIR_V, PAIR_K, PAIR_Q = 0, 256, 384

GLA_BATCH = 2
TOK_TILE = 256
MOE_BLOCK = 256
EXP_BUFS = 4
CMB_TILE = 256
DSP_TILE = 512
ROW_TILES = 8
DMA_QUEUES = 2


def _cparams(sem):
    return pltpu.CompilerParams(dimension_semantics=sem, vmem_limit_bytes=VMEM_LIMIT)


def _mod_kernel(c_ref, w_ref, b_ref, o_ref):
    c = c_ref[...]
    a = c * jax.nn.sigmoid(c)
    o_ref[0] = jnp.dot(a, w_ref[0], preferred_element_type=F32, precision=lax.Precision.HIGHEST) + b_ref[0]


def _modulation(cond8, w_mod, b_mod):
    depth, d, n = w_mod.shape
    tn = 1536
    return pl.pallas_call(
        _mod_kernel,
        grid=(depth, n // tn),
        in_specs=[
            pl.BlockSpec((SUBLANES, d), lambda i, j: (0, 0)),
            pl.BlockSpec((1, d, tn), lambda i, j: (i, 0, j)),
            pl.BlockSpec((1, 1, tn), lambda i, j: (i, 0, j)),
        ],
        out_specs=pl.BlockSpec((1, SUBLANES, tn), lambda i, j: (i, 0, j)),
        out_shape=jax.ShapeDtypeStruct((depth, SUBLANES, n), F32),
        compiler_params=_cparams(("arbitrary", "arbitrary")),
        name="adaln_mod",
    )(cond8, w_mod, b_mod.reshape(depth, 1, n))


def _inproj_kernel(x_ref, sh_ref, sc_ref, w_ref, o_ref):
    h = x_ref[...] * (1.0 + sc_ref[0]) + sh_ref[0]
    o_ref[...] = jnp.dot(h.astype(BF16), w_ref[...], preferred_element_type=F32)


def _inproj(h, shift, scale, w, group_of_tile):
    t, d = h.shape
    n = w.shape[1]
    grp = lambda i: (group_of_tile(i), 0, 0)
    return pl.pallas_call(
        _inproj_kernel,
        grid=(t // TOK_TILE,),
        in_specs=[
            pl.BlockSpec((TOK_TILE, d), lambda i: (i, 0)),
            pl.BlockSpec((1, 1, d), grp),
            pl.BlockSpec((1, 1, d), grp),
            pl.BlockSpec((d, n), lambda i: (0, 0)),
        ],
        out_specs=pl.BlockSpec((TOK_TILE, n), lambda i: (i, 0)),
        out_shape=jax.ShapeDtypeStruct((t, n), F32),
        compiler_params=_cparams(("arbitrary",)),
        name="inproj",
    )(h, shift, scale, w)


def _dot(a, b):
    return jnp.dot(a, b, preferred_element_type=F32)


def _dot_nt(a, b):
    return lax.dot_general(a, b, (((1,), (1,)), ((), ())), preferred_element_type=F32)


def _gla_direction(k2, q2, v2, dec, wup, bdec, sums, tri, s_ref, o_ref, bi, reverse):
    t = TOK_TILE
    c = GLA_CHUNK
    d_hi = dec.astype(BF16).astype(F32)
    packed = d_hi + pltpu.roll(dec - d_hi, 2 * GLA_LOWRANK, 1) + pltpu.roll(d_hi, 4 * GLA_LOWRANK, 1)
    z = _dot(packed.astype(BF16), wup) + bdec
    la = (jnp.minimum(z, 0.0) - jnp.log(1.0 + jnp.exp(-jnp.abs(z)))) * (1.0 / GLA_GATE_TEMP)

    la_hi = la.astype(BF16)
    la_lo = (la - la_hi.astype(F32)).astype(BF16)
    acc = _dot(sums, jnp.concatenate([la_hi, la_lo], axis=-1))
    bcum = acc[:t, :LANES] + acc[:t, LANES:]
    btot = acc[t:, :LANES] + acc[t:, LANES:]

    q_dec = q2 * jnp.exp(bcum) * (GLA_DK ** -0.5)
    k_inv = (k2 * jnp.exp(-bcum)).astype(BF16)
    k_tail = k2 * jnp.exp(btot - bcum)
    chunk_decay = jnp.exp(btot)

    lane = lax.broadcasted_iota(I32, (t, 2 * GLA_DK), 1)
    keep = tri > 0.0
    order = range(t // c - 1, -1, -1) if reverse else range(t // c)
    for hh in range(2):
        in_head = (lane // GLA_DK) == hh
        qd = jnp.where(in_head, q_dec, 0.0).astype(BF16)
        kt = jnp.where(in_head, k_tail, 0.0).astype(BF16)
        vh = v2[:, hh * GLA_DV:(hh + 1) * GLA_DV]
        vh_b = vh.astype(BF16)
        att = jnp.where(keep, _dot_nt(qd, k_inv), 0.0).astype(BF16)
        o_intra = _dot(att, vh_b)
        s = s_ref[bi, hh]
        for ci in order:
            rows = slice(ci * c, (ci + 1) * c)
            o_ref[bi, rows, hh * GLA_DV:(hh + 1) * GLA_DV] = o_intra[rows] + _dot_nt(qd[rows], s.astype(BF16))
            kv_t = _dot(vh[rows].T.astype(BF16), kt[rows])
            s = s * chunk_decay[ci * c:ci * c + 1, :] + kv_t
        s_ref[bi, hh] = s


def _gla_kernel(kc, qc, vc, dc, kf, qf, vf, df, kb, qb, vb, db, wupf, wupb, bf, bb, sums_f, sums_b, tri_f, tri_b,
                of_ref, ob_ref, sf_ref, sb_ref):
    first = pl.program_id(1) == 0

    @pl.when(first)
    def _():
        sf_ref[...] = jnp.zeros_like(sf_ref)
        sb_ref[...] = jnp.zeros_like(sb_ref)

    for bi in range(GLA_BATCH):
        pick = lambda c, m: jnp.where(first, c[bi], m[bi])
        _gla_direction(pick(kc, kf), pick(qc, qf), pick(vc, vf), pick(dc, df), wupf[...], bf[...], sums_f[...], tri_f[...],
                       sf_ref, of_ref, bi, False)
        _gla_direction(pick(kc, kb), pick(qc, qb), pick(vc, vb), pick(dc, db), wupb[...], bb[...], sums_b[...], tri_b[...],
                       sb_ref, ob_ref, bi, True)


def _chunk_matrices():
    t = TOK_TILE
    r = jnp.arange(t)[:, None]
    q = jnp.arange(t)[None, :]
    same = (r // GLA_CHUNK) == (q // GLA_CHUNK)
    out = []
    for tri in (same & (q <= r), same & (q >= r)):
        out.append((jnp.concatenate([tri, same], axis=0).astype(BF16), tri.astype(F32)))
    return out[0][0], out[1][0], out[0][1], out[1][1]


def _gla_pair(ctx_src, ctx_cols, lat_src, lat_cols, wupf, wupb, bf, bb):
    bsz, l, _ = ctx_src.shape
    nb = l // TOK_TILE
    nl = nb - 1
    skip = lat_src.shape[1] // TOK_TILE - nl
    assert skip in (0, 1)
    fwd = lambda j: j
    bwd = lambda j: jnp.where(j == 0, 0, nb - j)
    lat_fwd = lambda j: jnp.maximum(j - 1, 0) + skip
    lat_bwd = lambda j: jnp.where(j == 0, nl - 1, nl - j) + skip

    def spec(width, colidx, order):
        return pl.BlockSpec((GLA_BATCH, TOK_TILE, width), lambda b, j: (b, order(j), colidx))

    widths = (LANES, LANES, 2 * GLA_DV, LANES)
    ctx_specs = [spec(w, c, lambda j: 0) for w, c in zip(widths, ctx_cols)]
    fwd_specs = [spec(w, c, lat_fwd) for w, c in zip(widths, lat_cols)]
    bwd_specs = [spec(w, c, lat_bwd) for w, c in zip(widths, lat_cols)]

    full = lambda a: pl.BlockSpec(a.shape, lambda b, j: (0,) * a.ndim)
    consts = _chunk_matrices()
    out = jax.ShapeDtypeStruct((bsz, l, 2 * GLA_DV), F32)
    return pl.pallas_call(
        _gla_kernel,
        grid=(bsz // GLA_BATCH, nb),
        in_specs=ctx_specs + fwd_specs + bwd_specs + [full(wupf), full(wupb), full(bf), full(bb)] + [full(a) for a in consts],
        out_specs=[pl.BlockSpec((GLA_BATCH, TOK_TILE, 2 * GLA_DV), lambda b, j: (b, fwd(j), 0)),
                   pl.BlockSpec((GLA_BATCH, TOK_TILE, 2 * GLA_DV), lambda b, j: (b, bwd(j), 0))],
        out_shape=[out, out],
        scratch_shapes=[pltpu.VMEM((GLA_BATCH, 2, GLA_DV, LANES), F32), pltpu.VMEM((GLA_BATCH, 2, GLA_DV, LANES), F32)],
        compiler_params=_cparams(("arbitrary", "arbitrary")),
        name="gla_pair",
    )(*([ctx_src] * 4 + [lat_src] * 8), wupf, wupb, bf, bb, *consts)


def _layer_norm(v, g, b):
    mu = jnp.mean(v, axis=-1, keepdims=True)
    dv = v - mu
    var = jnp.mean(dv * dv, axis=-1, keepdims=True)
    return dv * lax.rsqrt(var + LN_EPS) * g + b


def _outproj_kernel(conv_ref, gate_ref, ofr, obr, ofc_ctx, obc_ctx, ofc_lat, obc_lat, h_ref, gt_ref, wconv_ref, gn_ref,
                    wout_ref, lng_ref, lnb_ref, o_ref, *, alpha, tiles_per_batch):
    tm = TOK_TILE
    is_ctx = (pl.program_id(0) % tiles_per_batch) == 0
    seg = jnp.where(is_ctx, tm, GRID_W)

    cv = conv_ref[...]
    b_gate = cv[:, :CONV_DIM]
    u = cv[:, CONV_DIM:2 * CONV_DIM] * cv[:, 2 * CONV_DIM:]
    pos = lax.broadcasted_iota(I32, (tm, CONV_DIM), 0) & (seg - 1)
    prev = jnp.where(pos == 0, 0.0, pltpu.roll(u, 1, 0))
    nxt = jnp.where(pos == seg - 1, 0.0, pltpu.roll(u, tm - 1, 0))
    wc = wconv_ref[...]
    mix_conv = b_gate * (wc[0:1, :] * prev + wc[1:2, :] * u + wc[2:3, :] * nxt)

    o_r = ofr[...] + obr[...]
    o_c = jnp.where(is_ctx, ofc_ctx[...] + obc_ctx[...], ofc_lat[...] + obc_lat[...])
    gate = gate_ref[...]
    gn = gn_ref[...]
    parts = [mix_conv.astype(BF16)]
    for hd in range(GLA_HEADS):
        src = o_r if hd < 2 else o_c
        o = src[:, (hd % 2) * GLA_DV:(hd % 2 + 1) * GLA_DV]
        o = o * lax.rsqrt(jnp.mean(o * o, axis=-1, keepdims=True) + RMS_EPS) * gn
        g = gate[:, hd * GLA_DV:(hd + 1) * GLA_DV]
        parts.append((o * (g * jax.nn.sigmoid(g))).astype(BF16))
    mix = jnp.concatenate(parts, axis=-1)
    y = _dot(mix, wout_ref[...])
    v = alpha * h_ref[...] + gt_ref[0] * y
    o_ref[...] = _layer_norm(v, lng_ref[...], lnb_ref[...])


def _outproj(p, o_fr, o_br, o_fc, o_bc, o_fc_lat, o_bc_lat, h, gate1, w_conv, g_norm, w_out, ln_g, ln_b, group_of_tile,
             alpha, tiles_per_batch):
    t, d = h.shape
    tm = TOK_TILE
    per = tiles_per_batch
    tok = lambda width, colidx: pl.BlockSpec((tm, width), lambda i: (i, colidx))
    ctx_rows = pl.BlockSpec((tm, 2 * GLA_DV), lambda i: (i // per * per, 0))
    lat_rows = pl.BlockSpec((tm, 2 * GLA_DV), lambda i: (i // per * (per - 1) + jnp.maximum(i % per - 1, 0), 0))
    full = lambda a: pl.BlockSpec(a.shape, lambda i: (0,) * a.ndim)
    return pl.pallas_call(
        functools.partial(_outproj_kernel, alpha=alpha, tiles_per_batch=tiles_per_batch),
        grid=(t // tm,),
        in_specs=[tok(3 * CONV_DIM, P_CONV // (3 * CONV_DIM)), tok(GLA_VAL, P_G // GLA_VAL),
                  tok(2 * GLA_DV, 0), tok(2 * GLA_DV, 0), ctx_rows, ctx_rows, lat_rows, lat_rows,
                  tok(d, 0), pl.BlockSpec((1, 1, d), lambda i: (group_of_tile(i), 0, 0)),
                  full(w_conv), full(g_norm), full(w_out), full(ln_g), full(ln_b)],
        out_specs=pl.BlockSpec((tm, d), lambda i: (i, 0)),
        out_shape=jax.ShapeDtypeStruct((t, d), F32),
        compiler_params=_cparams(("arbitrary",)),
        name="outproj_ln1",
    )(p, p, o_fr, o_br, o_fc, o_bc, o_fc_lat, o_bc_lat, h, gate1, w_conv, g_norm, w_out, ln_g, ln_b)


def _route_kernel(h_ref, sh_ref, sc_ref, wrt_hi_ref, wrt_lo_ref, rbt_ref, wsg_ref, wsu_ref, wsd_ref, upper_ref,
                  x3_ref, idx_ref, wt_ref, rank_ref, cnt_ref, shared_ref, base_ref):
    tm = TOK_TILE
    n_exp = rbt_ref.shape[0]
    gsz = n_exp // N_GROUPS

    @pl.when(pl.program_id(0) == 0)
    def _():
        base_ref[...] = jnp.zeros_like(base_ref)

    u = h_ref[...] * (1.0 + sc_ref[0]) + sh_ref[0]
    for j in range(ROW_TILES):
        x3_ref[pl.ds(j, tm, stride=ROW_TILES), :] = u[:, j * LANES:(j + 1) * LANES]

    u_hi = u.astype(BF16)
    u_lo = (u - u_hi.astype(F32)).astype(BF16)
    shared_ref[...] = _dot((jax.nn.silu(_dot(u_hi, wsg_ref[...])) * _dot(u_hi, wsu_ref[...])).astype(BF16), wsd_ref[...])

    wrt_hi = wrt_hi_ref[...]
    logits = _dot_nt(wrt_hi, u_hi) + _dot_nt(wrt_hi, u_lo) + _dot_nt(wrt_lo_ref[...], u_hi)
    s = jax.nn.sigmoid(logits)
    biased = s + rbt_ref[...]

    neg = -jnp.inf
    eidx = lax.broadcasted_iota(I32, (n_exp, tm), 0)
    loc = lax.broadcasted_iota(I32, (gsz, tm), 0)
    gscore = []
    for g in range(N_GROUPS):
        vals = biased[g * gsz:(g + 1) * gsz, :]
        m1 = jnp.max(vals, axis=0, keepdims=True)
        i1 = jnp.min(jnp.where(vals == m1, loc, gsz), axis=0, keepdims=True)
        m2 = jnp.max(jnp.where(loc == i1, neg, vals), axis=0, keepdims=True)
        gscore.append(m1 + m2)
    pieces = []
    for g in range(N_GROUPS):
        ahead = jnp.zeros((1, tm), F32)
        for g2 in range(N_GROUPS):
            if g2 == g:
                continue
            wins = (gscore[g2] >= gscore[g]) if g2 < g else (gscore[g2] > gscore[g])
            ahead = ahead + jnp.where(wins, 1.0, 0.0)
        pieces.append(jnp.where(ahead < TOPK_GROUPS, biased[g * gsz:(g + 1) * gsz, :], neg))
    masked = jnp.concatenate(pieces, axis=0)

    onehot = jnp.zeros((n_exp, tm), F32)
    idxs = []
    for _ in range(TOP_K):
        m = jnp.max(masked, axis=0, keepdims=True)
        i = jnp.min(jnp.where(masked == m, eidx, n_exp), axis=0, keepdims=True)
        hit = eidx == i
        onehot = jnp.where(hit, 1.0, onehot)
        masked = jnp.where(hit, neg, masked)
        idxs.append(i)
    sel = onehot * s
    wnorm = sel / jnp.sum(sel, axis=0, keepdims=True) * ROUTED_SCALE

    onehot_b = onehot.astype(BF16)
    upper = upper_ref[...]
    rank_all = _dot(onehot_b, upper) + base_ref[...]
    base_new = base_ref[...] + _dot(onehot_b, jnp.ones((tm, tm), BF16))
    base_ref[...] = base_new
    cnt_ref[...] = base_new[:, :LANES]

    wts, ranks = [], []
    for k in range(TOP_K):
        hit = eidx == idxs[k]
        wts.append(jnp.sum(jnp.where(hit, wnorm, 0.0), axis=0, keepdims=True))
        ranks.append(jnp.sum(jnp.where(hit, rank_all, 0.0), axis=0, keepdims=True))
    idx_ref[...] = jnp.concatenate(idxs, axis=0)
    wt_ref[...] = jnp.concatenate(wts, axis=0)
    rank_ref[...] = jnp.concatenate(ranks, axis=0).astype(I32)


def _route(h1, shift, scale, wr, rbias, wsg, wsu, wsd, group_of_tile):
    t, d = h1.shape
    tm = TOK_TILE
    n_exp = wr.shape[1]
    wrt = wr.T
    wrt_hi = wrt.astype(BF16)
    wrt_lo = (wrt - wrt_hi.astype(F32)).astype(BF16)
    rbt = jnp.broadcast_to(rbias.reshape(n_exp, 1), (n_exp, tm))
    upper = (jnp.arange(tm)[:, None] < jnp.arange(tm)[None, :]).astype(BF16)
    grp = lambda i: (group_of_tile(i), 0, 0)
    full = lambda a: pl.BlockSpec(a.shape, lambda i: (0,) * a.ndim)
    choice = pl.BlockSpec((TOP_K, tm), lambda i: (0, i))
    choice_out = lambda dt: jax.ShapeDtypeStruct((TOP_K, t), dt)
    return pl.pallas_call(
        _route_kernel,
        grid=(t // tm,),
        in_specs=[pl.BlockSpec((tm, d), lambda i: (i, 0)), pl.BlockSpec((1, 1, d), grp), pl.BlockSpec((1, 1, d), grp),
                  full(wrt_hi), full(wrt_lo), full(rbt), full(wsg), full(wsu), full(wsd), full(upper)],
        out_specs=[pl.BlockSpec((tm * ROW_TILES, LANES), lambda i: (i, 0)), choice, choice, choice,
                   pl.BlockSpec((n_exp, LANES), lambda i: (0, 0)),
                   pl.BlockSpec((tm, d), lambda i: (i, 0))],
        out_shape=[jax.ShapeDtypeStruct((t * ROW_TILES, LANES), F32), choice_out(I32), choice_out(F32), choice_out(I32),
                   jax.ShapeDtypeStruct((n_exp, LANES), F32), jax.ShapeDtypeStruct((t, d), F32)],
        scratch_shapes=[pltpu.VMEM((n_exp, tm), F32)],
        compiler_params=_cparams(("arbitrary",)),
        name="route_shared",
    )(h1, shift, scale, wrt_hi, wrt_lo, rbt, wsg, wsu, wsd, upper)


def _dispatch_kernel(idx_ref, rank_ref, pstart_ref, x_ref, xs_hbm, slot_ref, sem):
    def body(i, carry):
        src = x_ref.at[pl.ds(pl.multiple_of(i * ROW_TILES, ROW_TILES), ROW_TILES)]
        for k in range(TOP_K):
            r = i * TOP_K + k
            slot = pstart_ref[idx_ref[r]] + rank_ref[r]
            slot_ref[r] = slot
            pltpu.make_async_copy(src, xs_hbm.at[slot], sem).start(priority=k % DMA_QUEUES)
        return carry

    lax.fori_loop(0, DSP_TILE, body, 0)
    done = xs_hbm.at[pl.ds(0, DSP_TILE * TOP_K)]
    pltpu.make_async_copy(done, done, sem).wait()


def _dispatch(x2, idx, rank, pstart, n_slots):
    t = x2.shape[0] // ROW_TILES
    choice = pl.BlockSpec((DSP_TILE * TOP_K,), lambda i: (i,), memory_space=pltpu.SMEM)
    return pl.pallas_call(
        _dispatch_kernel,
        grid=(t // DSP_TILE,),
        in_specs=[choice, choice, pl.BlockSpec(memory_space=pltpu.SMEM),
                  pl.BlockSpec((DSP_TILE * ROW_TILES, LANES), lambda i: (i, 0))],
        out_specs=[pl.BlockSpec(memory_space=pl.ANY), choice],
        out_shape=[jax.ShapeDtypeStruct((n_slots, ROW_TILES, LANES), F32), jax.ShapeDtypeStruct(idx.shape, I32)],
        scratch_shapes=[pltpu.SemaphoreType.DMA(())],
        compiler_params=_cparams(("arbitrary",)),
        name="dispatch",
    )(idx, rank, pstart, x2)


def _expert_kernel(gstart_ref, nblk_ref, ntot_ref, xs_hbm, wg_ref, wu_ref, wd_ref, ys_hbm,
                   xbuf, ybuf, wg_b, wu_b, wd_b, sem_in, sem_out):
    e = pl.program_id(0)
    nb = nblk_ref[e]
    g0 = gstart_ref[e]
    total = ntot_ref[0]
    blk_rows = MOE_BLOCK * ROW_TILES
    ahead = EXP_BUFS - 1

    def rows_of(g):
        return pl.ds(pl.multiple_of(g * blk_rows, blk_rows), blk_rows)

    def in_copy(g):
        s = g % EXP_BUFS
        return pltpu.make_async_copy(xs_hbm.at[rows_of(g)], xbuf.at[s], sem_in.at[s])

    def out_copy(g):
        s = g % EXP_BUFS
        return pltpu.make_async_copy(ybuf.at[s], ys_hbm.at[rows_of(g)], sem_out.at[s])

    @pl.when(e == 0)
    def _():
        for g in range(ahead):
            @pl.when(g < total)
            def _():
                in_copy(g).start()

    @pl.when(nb > 0)
    def _():
        wg_b[...] = wg_ref[0, 0].astype(BF16)
        wu_b[...] = wu_ref[0, 0].astype(BF16)
        wd_b[...] = wd_ref[0, 0].astype(BF16)

    def body(b, carry):
        g = g0 + b
        s = g % EXP_BUFS

        @pl.when(g + ahead < total)
        def _():
            in_copy(g + ahead).start()

        in_copy(g).wait()

        @pl.when(g >= EXP_BUFS)
        def _():
            out_copy(g - EXP_BUFS).wait()

        x = jnp.concatenate([xbuf[s, pl.ds(j, MOE_BLOCK, stride=ROW_TILES), :] for j in range(ROW_TILES)], axis=-1)
        xb = x.astype(BF16)
        hid = jax.nn.silu(_dot(xb, wg_b[...])) * _dot(xb, wu_b[...])
        y = _dot(hid.astype(BF16), wd_b[...])
        for j in range(ROW_TILES):
            ybuf[s, pl.ds(j, MOE_BLOCK, stride=ROW_TILES), :] = y[:, j * LANES:(j + 1) * LANES]
        out_copy(g).start()
        return carry

    lax.fori_loop(0, nb, body, 0)

    @pl.when(e == pl.num_programs(0) - 1)
    def _():
        for back in range(EXP_BUFS):
            @pl.when(total - 1 - back >= 0)
            def _():
                out_copy(total - 1 - back).wait()


def _experts(xs2, gstart, nblk, ntot, layer, w_gate, w_up, w_down):
    n_rows = xs2.shape[0]
    _, n_exp, d, hdim = w_gate.shape
    blk_rows = MOE_BLOCK * ROW_TILES
    wmap = lambda e, gs, nb, nt: (layer, e, 0, 0)
    return pl.pallas_call(
        _expert_kernel,
        grid_spec=pltpu.PrefetchScalarGridSpec(
            num_scalar_prefetch=3,
            grid=(n_exp,),
            in_specs=[pl.BlockSpec(memory_space=pl.ANY),
                      pl.BlockSpec((1, 1, d, hdim), wmap),
                      pl.BlockSpec((1, 1, d, hdim), wmap),
                      pl.BlockSpec((1, 1, hdim, d), wmap)],
            out_specs=pl.BlockSpec(memory_space=pl.ANY),
            scratch_shapes=[pltpu.VMEM((EXP_BUFS, blk_rows, LANES), F32), pltpu.VMEM((EXP_BUFS, blk_rows, LANES), F32),
                            pltpu.VMEM((d, hdim), BF16), pltpu.VMEM((d, hdim), BF16), pltpu.VMEM((hdim, d), BF16),
                            pltpu.SemaphoreType.DMA((EXP_BUFS,)), pltpu.SemaphoreType.DMA((EXP_BUFS,))],
        ),
        out_shape=jax.ShapeDtypeStruct((n_rows, LANES), F32),
        compiler_params=_cparams(("arbitrary",)),
        name="experts",
    )(gstart, nblk, ntot, xs2, w_gate, w_up, w_down)


def _combine_kernel(slot_cur, slot_nxt, ys_hbm, wt_ref, shared_ref, h_ref, gt_ref, lng_ref, lnb_ref, o_ref,
                    buf, mixed, sem, *, alpha):
    i = pl.program_id(0)
    n = pl.num_programs(0)
    tt = CMB_TILE

    cur = i % 2
    unroll = 4

    def issue_token(slot_ref, b, tok):
        for k in range(TOP_K):
            r = tok * TOP_K + k
            dst = buf.at[b, pl.ds(pl.multiple_of(r * ROW_TILES, ROW_TILES), ROW_TILES)]
            pltpu.make_async_copy(ys_hbm.at[slot_ref[r]], dst, sem.at[b]).start(priority=k % DMA_QUEUES)

    def mix_token(tok):
        first = tok * TOP_K
        acc = None
        for k in range(TOP_K):
            row = buf[cur, pl.ds(pl.multiple_of((first + k) * ROW_TILES, ROW_TILES), ROW_TILES), :]
            term = wt_ref[first + k] * row
            acc = term if acc is None else acc + term
        mixed[pl.ds(pl.multiple_of(tok * ROW_TILES, ROW_TILES), ROW_TILES), :] = acc

    @pl.when(i == 0)
    def _():
        def body(tok, carry):
            issue_token(slot_cur, 0, tok)
            return carry
        lax.fori_loop(0, tt, body, 0)

    pltpu.make_async_copy(buf.at[cur], buf.at[cur], sem.at[cur]).wait()

    @pl.when(i + 1 < n)
    def _():
        def body(step, carry):
            for u in range(unroll):
                issue_token(slot_nxt, (i + 1) % 2, step * unroll + u)
                mix_token(step * unroll + u)
            return carry
        lax.fori_loop(0, tt // unroll, body, 0)

    @pl.when(i + 1 == n)
    def _():
        def body(step, carry):
            for u in range(unroll):
                mix_token(step * unroll + u)
            return carry
        lax.fori_loop(0, tt // unroll, body, 0)

    routed = jnp.concatenate([mixed[pl.ds(j, tt, stride=ROW_TILES), :] for j in range(ROW_TILES)], axis=-1)
    f = shared_ref[...] + routed
    v = alpha * h_ref[...] + gt_ref[0] * f
    o_ref[...] = _layer_norm(v, lng_ref[...], lnb_ref[...])


def _combine(ys3, slots, wts, shared, h1, gate2, ln_g, ln_b, group_of_tile, alpha, latent_only_tiles=None):
    t, d = h1.shape
    tt = CMB_TILE
    n = t // tt
    scale = TOK_TILE // tt
    if latent_only_tiles is None:
        out_rows, out_block = t, lambda i: (i, 0)
    else:
        assert tt == TOK_TILE
        per = latent_only_tiles
        out_rows = t - (n // per) * tt
        out_block = lambda i: ((i // per) * (per - 1) + jnp.maximum(i % per - 1, 0), 0)
    full = lambda a: pl.BlockSpec(a.shape, lambda i: (0,) * a.ndim)
    cur = pl.BlockSpec((tt * TOP_K,), lambda i: (i,), memory_space=pltpu.SMEM)
    nxt = pl.BlockSpec((tt * TOP_K,), lambda i: (jnp.minimum(i + 1, n - 1),), memory_space=pltpu.SMEM)
    return pl.pallas_call(
        functools.partial(_combine_kernel, alpha=alpha),
        grid=(n,),
        in_specs=[cur, nxt, pl.BlockSpec(memory_space=pl.ANY), cur,
                  pl.BlockSpec((tt, d), lambda i: (i, 0)),
                  pl.BlockSpec((tt, d), lambda i: (i, 0)),
                  pl.BlockSpec((1, 1, d), lambda i: (group_of_tile(i // scale), 0, 0)),
                  full(ln_g), full(ln_b)],
        out_specs=pl.BlockSpec((tt, d), out_block),
        out_shape=jax.ShapeDtypeStruct((out_rows, d), F32),
        scratch_shapes=[pltpu.VMEM((2, tt * TOP_K * ROW_TILES, LANES), F32), pltpu.VMEM((tt * ROW_TILES, LANES), F32),
                        pltpu.SemaphoreType.DMA((2,))],
        compiler_params=_cparams(("arbitrary",)),
        name="combine_ln2",
    )(slots, slots, ys3, wts, shared, h1, gate2, ln_g, ln_b)


def _reorder_w_in(w):
    d = w.shape[0]
    o = 0
    k = w[:, o:o + GLA_KEY]; o += GLA_KEY
    v = w[:, o:o + GLA_VAL]; o += GLA_VAL
    dec = w[:, o:o + 2 * GLA_LOWRANK]; o += 2 * GLA_LOWRANK
    q = w[:, o:o + GLA_KEY]; o += GLA_KEY
    g = w[:, o:o + GLA_VAL]; o += GLA_VAL
    conv = w[:, o:o + 3 * CONV_DIM]
    pad = jnp.zeros((d, LANES - 2 * GLA_LOWRANK), w.dtype)
    hk, hv = 2 * GLA_DK, 2 * GLA_DV
    pairs = [jnp.concatenate([v[:, i * hv:(i + 1) * hv], k[:, i * hk:(i + 1) * hk], q[:, i * hk:(i + 1) * hk]], axis=1)
             for i in range(2)]
    return jnp.concatenate([conv, g, pairs[0], pairs[1], dec, pad], axis=1).astype(BF16)


def _decay_weights(w_up, b_dec, pair):
    lo = pair * 2 * GLA_DK
    nr = 2 * GLA_LOWRANK
    outs = []
    for d in range(2):
        w = jnp.zeros((nr, LANES), F32).at[d * GLA_LOWRANK:(d + 1) * GLA_LOWRANK, :].set(w_up[d][:, lo:lo + LANES])
        w_hi = w.astype(BF16)
        w_lo = (w - w_hi.astype(F32)).astype(BF16)
        packed = jnp.concatenate([w_hi, w_hi, w_lo, jnp.zeros((LANES - 3 * nr, LANES), BF16)], axis=0)
        outs.append((packed, b_dec[d][lo:lo + LANES].reshape(1, LANES)))
    return outs[0][0], outs[1][0], outs[0][1], outs[1][1]


def _latent_to_scan_order(a, ctx_len, rows):
    b, _, f = a.shape
    return a[:, ctx_len:].reshape(b, rows, GRID_W, f).swapaxes(1, 2).reshape(b, rows * GRID_W, f)


def _latent_from_scan_order(a, ctx_len, rows):
    b, _, f = a.shape
    return a[:, ctx_len:].reshape(b, GRID_W, rows, f).swapaxes(1, 2).reshape(b * rows * GRID_W, f)


def kernel(x, c, ctx, c_ctx, w_mod, b_mod, w_in, w_conv, w_decay_up, b_decay, g_gla_norm, w_out, ln1_g, ln1_b, w_router, router_bias, w_e_gate, w_e_up, w_e_down, w_s_gate, w_s_up, w_s_down, ln2_g, ln2_b):
    bsz, seq, d = x.shape
    ctx_len = ctx.shape[1]
    depth = w_mod.shape[0]
    n_exp = w_router.shape[2]
    rows = seq // GRID_W
    l = ctx_len + seq
    t = bsz * l
    assert ctx_len == TOK_TILE and seq % TOK_TILE == 0 and d == ROW_TILES * LANES
    assert t % DSP_TILE == 0 and bsz + 1 <= SUBLANES and bsz % GLA_BATCH == 0
    tiles_per_batch = l // TOK_TILE
    alpha = float((2 * depth) ** 0.25)

    def group_of_tile(i):
        return jnp.where(i % tiles_per_batch == 0, bsz, i // tiles_per_batch)

    cond8 = jnp.zeros((SUBLANES, d), F32).at[:bsz].set(c).at[bsz].set(c_ctx)
    mod = _modulation(cond8, w_mod, b_mod).reshape(depth, SUBLANES, 6, 1, d)

    nk = t * TOP_K
    n_blocks = -(-(nk + n_exp * (MOE_BLOCK - 1)) // MOE_BLOCK)
    n_slots = n_blocks * MOE_BLOCK

    h = jnp.concatenate([ctx, x], axis=1).reshape(t, d)
    for i in range(depth):
        m = [mod[i, :, j] for j in range(6)]
        p = _inproj(h, m[0], m[1], _reorder_w_in(w_in[i]), group_of_tile)
        p3 = p.reshape(bsz, l, P_COLS)

        wf, wb, bf, bb = _decay_weights(w_decay_up[i], b_decay[i], 0)
        pair_cols = lambda base: ((base + PAIR_K) // LANES, (base + PAIR_Q) // LANES, (base + PAIR_V) // (2 * GLA_DV),
                                  P_DEC // LANES)
        o_fr, o_br = _gla_pair(p3, pair_cols(P_ROW), p3, pair_cols(P_ROW), wf, wb, bf, bb)
        col_in = _latent_to_scan_order(p3[..., P_COL:], ctx_len, rows)
        wf, wb, bf, bb = _decay_weights(w_decay_up[i], b_decay[i], 1)
        o_fc, o_bc = _gla_pair(p3, pair_cols(P_COL), col_in,
                               (PAIR_K // LANES, PAIR_Q // LANES, PAIR_V // (2 * GLA_DV), (P_DEC - P_COL) // LANES),
                               wf, wb, bf, bb)

        flat = lambda a: a.reshape(t, a.shape[-1])
        h1 = _outproj(p, flat(o_fr), flat(o_br), flat(o_fc), flat(o_bc), _latent_from_scan_order(o_fc, ctx_len, rows),
                      _latent_from_scan_order(o_bc, ctx_len, rows), h, m[2], w_conv[i],
                      g_gla_norm[i].reshape(1, GLA_DV), w_out[i].astype(BF16), ln1_g[i].reshape(1, d), ln1_b[i].reshape(1, d),
                      group_of_tile, alpha, tiles_per_batch)

        x3, idx, wts, rank, counts, shared = _route(
            h1, m[3], m[4], w_router[i], router_bias[i],
            w_s_gate[i].astype(BF16), w_s_up[i].astype(BF16), w_s_down[i].astype(BF16), group_of_tile)

        cnt = counts[:, 0].astype(I32)
        padded = (cnt + MOE_BLOCK - 1) // MOE_BLOCK * MOE_BLOCK
        pend = jnp.cumsum(padded)
        pstart = pend - padded
        idx, rank, wts = idx.T.reshape(nk), rank.T.reshape(nk), wts.T.reshape(nk)

        xs3, slots = _dispatch(x3, idx, rank, pstart, n_slots)
        ys2 = _experts(xs3.reshape(n_slots * ROW_TILES, LANES), pstart // MOE_BLOCK, padded // MOE_BLOCK,
                       (pend[-1] // MOE_BLOCK).reshape(1), i, w_e_gate, w_e_up, w_e_down)
        h = _combine(ys2.reshape(n_slots, ROW_TILES, LANES), slots, wts, shared, h1, m[5],
                     ln2_g[i].reshape(1, d), ln2_b[i].reshape(1, d), group_of_tile, alpha,
                     latent_only_tiles=tiles_per_batch if i == depth - 1 else None)

    return h.reshape(bsz, seq, d)
```

```python
import functools

import jax
import jax.numpy as jnp
from jax import lax
from jax.experimental import pallas as pl
from jax.experimental.pallas import tpu as pltpu

F32 = jnp.float32
BF16 = jnp.bfloat16
I32 = jnp.int32

GRID_W = 64
CONV_DIM = 512
GLA_HEADS = 4
GLA_DK = 64
GLA_DV = 128
GLA_KEY = GLA_HEADS * GLA_DK
GLA_VAL = GLA_HEADS * GLA_DV
GLA_LOWRANK = 16
GLA_GATE_TEMP = 16.0
GLA_CHUNK = 64
TOP_K = 8
N_GROUPS = 8
TOPK_GROUPS = 4
ROUTED_SCALE = 2.5
LN_EPS = 1e-5
RMS_EPS = 1e-6

LANES = 128
SUBLANES = 8
VMEM_LIMIT = 56 * 1024 * 1024

P_CONV = 0
P_G = 1536
P_ROW = 2048
P_COL = 2560
P_DEC = 3072
P_COLS = 3200
PAIR_V, PAIR_K, PAIR_Q = 0, 256, 384

GLA_BATCH = 4
TOK_TILE = 256
MOE_BLOCK = 256
EXP_BUFS = 4
CMB_TILE = 256
DSP_TILE = 256
ROW_TILES = 8
DMA_QUEUES = 2


def _cparams(sem):
    return pltpu.CompilerParams(dimension_semantics=sem, vmem_limit_bytes=VMEM_LIMIT)


def _mod_kernel(c_ref, w_ref, b_ref, o_ref):
    c = c_ref[...]
    a = c * jax.nn.sigmoid(c)
    o_ref[0] = jnp.dot(a, w_ref[0], preferred_element_type=F32, precision=lax.Precision.HIGHEST) + b_ref[0]


def _modulation(cond8, w_mod, b_mod):
    depth, d, n = w_mod.shape
    tn = 1536
    return pl.pallas_call(
        _mod_kernel,
        grid=(depth, n // tn),
        in_specs=[
            pl.BlockSpec((SUBLANES, d), lambda i, j: (0, 0)),
            pl.BlockSpec((1, d, tn), lambda i, j: (i, 0, j)),
            pl.BlockSpec((1, 1, tn), lambda i, j: (i, 0, j)),
        ],
        out_specs=pl.BlockSpec((1, SUBLANES, tn), lambda i, j: (i, 0, j)),
        out_shape=jax.ShapeDtypeStruct((depth, SUBLANES, n), F32),
        compiler_params=_cparams(("arbitrary", "arbitrary")),
        name="adaln_mod",
    )(cond8, w_mod, b_mod.reshape(depth, 1, n))


def _inproj_kernel(x_ref, sh_ref, sc_ref, w_ref, o_ref):
    h = x_ref[...] * (1.0 + sc_ref[0]) + sh_ref[0]
    o_ref[...] = jnp.dot(h.astype(BF16), w_ref[...], preferred_element_type=F32)


def _inproj(h, shift, scale, w, group_of_tile):
    t, d = h.shape
    n = w.shape[1]
    grp = lambda i: (group_of_tile(i), 0, 0)
    return pl.pallas_call(
        _inproj_kernel,
        grid=(t // TOK_TILE,),
        in_specs=[
            pl.BlockSpec((TOK_TILE, d), lambda i: (i, 0)),
            pl.BlockSpec((1, 1, d), grp),
            pl.BlockSpec((1, 1, d), grp),
            pl.BlockSpec((d, n), lambda i: (0, 0)),
        ],
        out_specs=pl.BlockSpec((TOK_TILE, n), lambda i: (i, 0)),
        out_shape=jax.ShapeDtypeStruct((t, n), F32),
        compiler_params=_cparams(("arbitrary",)),
        name="inproj",
    )(h, shift, scale, w)


def _dot(a, b):
    return jnp.dot(a, b, preferred_element_type=F32)


def _dot_nt(a, b):
    return lax.dot_general(a, b, (((1,), (1,)), ((), ())), preferred_element_type=F32)


def _gla_direction(k2, q2, v2, dec, wup, bdec, sums, tri, s_ref, o_ref, bi, reverse):
    t = TOK_TILE
    c = GLA_CHUNK
    d_hi = dec.astype(BF16).astype(F32)
    packed = d_hi + pltpu.roll(dec - d_hi, 2 * GLA_LOWRANK, 1) + pltpu.roll(d_hi, 4 * GLA_LOWRANK, 1)
    z = _dot(packed.astype(BF16), wup) + bdec
    la = (jnp.minimum(z, 0.0) - jnp.log(1.0 + jnp.exp(-jnp.abs(z)))) * (1.0 / GLA_GATE_TEMP)

    la_hi = la.astype(BF16)
    la_lo = (la - la_hi.astype(F32)).astype(BF16)
    acc = _dot(sums, jnp.concatenate([la_hi, la_lo], axis=-1))
    bcum = acc[:t, :LANES] + acc[:t, LANES:]
    btot = acc[t:, :LANES] + acc[t:, LANES:]

    q_dec = q2 * jnp.exp(bcum) * (GLA_DK ** -0.5)
    k_inv = (k2 * jnp.exp(-bcum)).astype(BF16)
    k_tail = k2 * jnp.exp(btot - bcum)
    chunk_decay = jnp.exp(btot)

    lane = lax.broadcasted_iota(I32, (t, 2 * GLA_DK), 1)
    keep = tri > 0.0
    order = range(t // c - 1, -1, -1) if reverse else range(t // c)
    for hh in range(2):
        in_head = (lane // GLA_DK) == hh
        qd = jnp.where(in_head, q_dec, 0.0).astype(BF16)
        kt = jnp.where(in_head, k_tail, 0.0).astype(BF16)
        vh = v2[:, hh * GLA_DV:(hh + 1) * GLA_DV]
        vh_b = vh.astype(BF16)
        att = jnp.where(keep, _dot_nt(qd, k_inv), 0.0).astype(BF16)
        o_intra = _dot(att, vh_b)
        s = s_ref[bi, hh]
        for ci in order:
            rows = slice(ci * c, (ci + 1) * c)
            o_ref[bi, rows, hh * GLA_DV:(hh + 1) * GLA_DV] = o_intra[rows] + _dot_nt(qd[rows], s.astype(BF16))
            kv_t = _dot(vh[rows].T.astype(BF16), kt[rows])
            s = s * chunk_decay[ci * c:ci * c + 1, :] + kv_t
        s_ref[bi, hh] = s


def _gla_kernel(kc, qc, vc, dc, kf, qf, vf, df, kb, qb, vb, db, wupf, wupb, bf, bb, sums_f, sums_b, tri_f, tri_b,
                of_ref, ob_ref, sf_ref, sb_ref):
    first = pl.program_id(1) == 0

    @pl.when(first)
    def _():
        sf_ref[...] = jnp.zeros_like(sf_ref)
        sb_ref[...] = jnp.zeros_like(sb_ref)

    for bi in range(GLA_BATCH):
        pick = lambda c, m: jnp.where(first, c[bi], m[bi])
        _gla_direction(pick(kc, kf), pick(qc, qf), pick(vc, vf), pick(dc, df), wupf[...], bf[...], sums_f[...], tri_f[...],
                       sf_ref, of_ref, bi, False)
        _gla_direction(pick(kc, kb), pick(qc, qb), pick(vc, vb), pick(dc, db), wupb[...], bb[...], sums_b[...], tri_b[...],
                       sb_ref, ob_ref, bi, True)


def _chunk_matrices():
    t = TOK_TILE
    r = jnp.arange(t)[:, None]
    q = jnp.arange(t)[None, :]
    same = (r // GLA_CHUNK) == (q // GLA_CHUNK)
    out = []
    for tri in (same & (q <= r), same & (q >= r)):
        out.append((jnp.concatenate([tri, same], axis=0).astype(BF16), tri.astype(F32)))
    return out[0][0], out[1][0], out[0][1], out[1][1]


def _gla_pair(ctx_src, ctx_cols, lat_src, lat_cols, wupf, wupb, bf, bb):
    bsz, l, _ = ctx_src.shape
    nb = l // TOK_TILE
    nl = nb - 1
    skip = lat_src.shape[1] // TOK_TILE - nl
    assert skip in (0, 1)
    fwd = lambda j: j
    bwd = lambda j: jnp.where(j == 0, 0, nb - j)
    lat_fwd = lambda j: jnp.maximum(j - 1, 0) + skip
    lat_bwd = lambda j: jnp.where(j == 0, nl - 1, nl - j) + skip

    def spec(width, colidx, order):
        return pl.BlockSpec((GLA_BATCH, TOK_TILE, width), lambda b, j: (b, order(j), colidx))

    widths = (LANES, LANES, 2 * GLA_DV, LANES)
    ctx_specs = [spec(w, c, lambda j: 0) for w, c in zip(widths, ctx_cols)]
    fwd_specs = [spec(w, c, lat_fwd) for w, c in zip(widths, lat_cols)]
    bwd_specs = [spec(w, c, lat_bwd) for w, c in zip(widths, lat_cols)]

    full = lambda a: pl.BlockSpec(a.shape, lambda b, j: (0,) * a.ndim)
    consts = _chunk_matrices()
    out = jax.ShapeDtypeStruct((bsz, l, 2 * GLA_DV), F32)
    return pl.pallas_call(
        _gla_kernel,
        grid=(bsz // GLA_BATCH, nb),
        in_specs=ctx_specs + fwd_specs + bwd_specs + [full(wupf), full(wupb), full(bf), full(bb)] + [full(a) for a in consts],
        out_specs=[pl.BlockSpec((GLA_BATCH, TOK_TILE, 2 * GLA_DV), lambda b, j: (b, fwd(j), 0)),
                   pl.BlockSpec((GLA_BATCH, TOK_TILE, 2 * GLA_DV), lambda b, j: (b, bwd(j), 0))],
        out_shape=[out, out],
        scratch_shapes=[pltpu.VMEM((GLA_BATCH, 2, GLA_DV, LANES), F32), pltpu.VMEM((GLA_BATCH, 2, GLA_DV, LANES), F32)],
        compiler_params=_cparams(("arbitrary", "arbitrary")),
        name="gla_pair",
    )(*([ctx_src] * 4 + [lat_src] * 8), wupf, wupb, bf, bb, *consts)


def _layer_norm(v, g, b):
    mu = jnp.mean(v, axis=-1, keepdims=True)
    dv = v - mu
    var = jnp.mean(dv * dv, axis=-1, keepdims=True)
    return dv * lax.rsqrt(var + LN_EPS) * g + b


def _outproj_kernel(conv_ref, gate_ref, ofr, obr, ofc_ctx, obc_ctx, ofc_lat, obc_lat, h_ref, gt_ref, wconv_ref, gn_ref,
                    wout_ref, lng_ref, lnb_ref, o_ref, *, alpha, tiles_per_batch):
    tm = TOK_TILE
    is_ctx = (pl.program_id(0) % tiles_per_batch) == 0
    seg = jnp.where(is_ctx, tm, GRID_W)

    cv = conv_ref[...]
    b_gate = cv[:, :CONV_DIM]
    u = cv[:, CONV_DIM:2 * CONV_DIM] * cv[:, 2 * CONV_DIM:]
    pos = lax.broadcasted_iota(I32, (tm, CONV_DIM), 0) & (seg - 1)
    prev = jnp.where(pos == 0, 0.0, pltpu.roll(u, 1, 0))
    nxt = jnp.where(pos == seg - 1, 0.0, pltpu.roll(u, tm - 1, 0))
    wc = wconv_ref[...]
    mix_conv = b_gate * (wc[0:1, :] * prev + wc[1:2, :] * u + wc[2:3, :] * nxt)

    o_r = ofr[...] + obr[...]
    o_c = jnp.where(is_ctx, ofc_ctx[...] + obc_ctx[...], ofc_lat[...] + obc_lat[...])
    gate = gate_ref[...]
    gn = gn_ref[...]
    parts = [mix_conv.astype(BF16)]
    for hd in range(GLA_HEADS):
        src = o_r if hd < 2 else o_c
        o = src[:, (hd % 2) * GLA_DV:(hd % 2 + 1) * GLA_DV]
        o = o * lax.rsqrt(jnp.mean(o * o, axis=-1, keepdims=True) + RMS_EPS) * gn
        g = gate[:, hd * GLA_DV:(hd + 1) * GLA_DV]
        parts.append((o * (g * jax.nn.sigmoid(g))).astype(BF16))
    mix = jnp.concatenate(parts, axis=-1)
    y = _dot(mix, wout_ref[...])
    v = alpha * h_ref[...] + gt_ref[0] * y
    o_ref[...] = _layer_norm(v, lng_ref[...], lnb_ref[...])


def _outproj(p, o_fr, o_br, o_fc, o_bc, o_fc_lat, o_bc_lat, h, gate1, w_conv, g_norm, w_out, ln_g, ln_b, group_of_tile,
             alpha, tiles_per_batch):
    t, d = h.shape
    tm = TOK_TILE
    per = tiles_per_batch
    tok = lambda width, colidx: pl.BlockSpec((tm, width), lambda i: (i, colidx))
    ctx_rows = pl.BlockSpec((tm, 2 * GLA_DV), lambda i: (i // per * per, 0))
    lat_rows = pl.BlockSpec((tm, 2 * GLA_DV), lambda i: (i // per * (per - 1) + jnp.maximum(i % per - 1, 0), 0))
    full = lambda a: pl.BlockSpec(a.shape, lambda i: (0,) * a.ndim)
    return pl.pallas_call(
        functools.partial(_outproj_kernel, alpha=alpha, tiles_per_batch=tiles_per_batch),
        grid=(t // tm,),
        in_specs=[tok(3 * CONV_DIM, P_CONV // (3 * CONV_DIM)), tok(GLA_VAL, P_G // GLA_VAL),
                  tok(2 * GLA_DV, 0), tok(2 * GLA_DV, 0), ctx_rows, ctx_rows, lat_rows, lat_rows,
                  tok(d, 0), pl.BlockSpec((1, 1, d), lambda i: (group_of_tile(i), 0, 0)),
                  full(w_conv), full(g_norm), full(w_out), full(ln_g), full(ln_b)],
        out_specs=pl.BlockSpec((tm, d), lambda i: (i, 0)),
        out_shape=jax.ShapeDtypeStruct((t, d), F32),
        compiler_params=_cparams(("arbitrary",)),
        name="outproj_ln1",
    )(p, p, o_fr, o_br, o_fc, o_bc, o_fc_lat, o_bc_lat, h, gate1, w_conv, g_norm, w_out, ln_g, ln_b)


def _route_kernel(h_ref, sh_ref, sc_ref, wrt_hi_ref, wrt_lo_ref, rbt_ref, upper_ref,
                  idx_ref, wt_ref, rank_ref, cnt_ref, base_ref):
    tm = TOK_TILE
    n_exp = rbt_ref.shape[0]
    gsz = n_exp // N_GROUPS

    @pl.when(pl.program_id(0) == 0)
    def _():
        base_ref[...] = jnp.zeros_like(base_ref)

    u = h_ref[...] * (1.0 + sc_ref[0]) + sh_ref[0]
    u_hi = u.astype(BF16)
    u_lo = (u - u_hi.astype(F32)).astype(BF16)

    wrt_hi = wrt_hi_ref[...]
    logits = _dot_nt(wrt_hi, u_hi) + _dot_nt(wrt_hi, u_lo) + _dot_nt(wrt_lo_ref[...], u_hi)
    s = jax.nn.sigmoid(logits)
    biased = s + rbt_ref[...]

    neg = -jnp.inf
    eidx = lax.broadcasted_iota(I32, (n_exp, tm), 0)
    loc = lax.broadcasted_iota(I32, (gsz, tm), 0)
    gscore = []
    for g in range(N_GROUPS):
        vals = biased[g * gsz:(g + 1) * gsz, :]
        m1 = jnp.max(vals, axis=0, keepdims=True)
        i1 = jnp.min(jnp.where(vals == m1, loc, gsz), axis=0, keepdims=True)
        m2 = jnp.max(jnp.where(loc == i1, neg, vals), axis=0, keepdims=True)
        gscore.append(m1 + m2)
    pieces = []
    for g in range(N_GROUPS):
        ahead = jnp.zeros((1, tm), F32)
        for g2 in range(N_GROUPS):
            if g2 == g:
                continue
            wins = (gscore[g2] >= gscore[g]) if g2 < g else (gscore[g2] > gscore[g])
            ahead = ahead + jnp.where(wins, 1.0, 0.0)
        pieces.append(jnp.where(ahead < TOPK_GROUPS, biased[g * gsz:(g + 1) * gsz, :], neg))
    masked = jnp.concatenate(pieces, axis=0)

    onehot = jnp.zeros((n_exp, tm), F32)
    idxs = []
    for _ in range(TOP_K):
        m = jnp.max(masked, axis=0, keepdims=True)
        i = jnp.min(jnp.where(masked == m, eidx, n_exp), axis=0, keepdims=True)
        hit = eidx == i
        onehot = jnp.where(hit, 1.0, onehot)
        masked = jnp.where(hit, neg, masked)
        idxs.append(i)
    sel = onehot * s
    wnorm = sel / jnp.sum(sel, axis=0, keepdims=True) * ROUTED_SCALE

    onehot_b = onehot.astype(BF16)
    upper = upper_ref[...]
    rank_all = _dot(onehot_b, upper) + base_ref[...]
    base_new = base_ref[...] + _dot(onehot_b, jnp.ones((tm, tm), BF16))
    base_ref[...] = base_new
    cnt_ref[...] = base_new[:, :LANES]

    wts, ranks = [], []
    for k in range(TOP_K):
        hit = eidx == idxs[k]
        wts.append(jnp.sum(jnp.where(hit, wnorm, 0.0), axis=0, keepdims=True))
        ranks.append(jnp.sum(jnp.where(hit, rank_all, 0.0), axis=0, keepdims=True))
    idx_ref[...] = jnp.concatenate(idxs, axis=0)
    wt_ref[...] = jnp.concatenate(wts, axis=0)
    rank_ref[...] = jnp.concatenate(ranks, axis=0).astype(I32)


def _route(h1, shift, scale, wr, rbias, group_of_tile):
    t, d = h1.shape
    tm = TOK_TILE
    n_exp = wr.shape[1]
    wrt = wr.T
    wrt_hi = wrt.astype(BF16)
    wrt_lo = (wrt - wrt_hi.astype(F32)).astype(BF16)
    rbt = jnp.broadcast_to(rbias.reshape(n_exp, 1), (n_exp, tm))
    upper = (jnp.arange(tm)[:, None] < jnp.arange(tm)[None, :]).astype(BF16)
    grp = lambda i: (group_of_tile(i), 0, 0)
    full = lambda a: pl.BlockSpec(a.shape, lambda i: (0,) * a.ndim)
    choice = pl.BlockSpec((TOP_K, tm), lambda i: (0, i))
    choice_out = lambda dt: jax.ShapeDtypeStruct((TOP_K, t), dt)
    return pl.pallas_call(
        _route_kernel,
        grid=(t // tm,),
        in_specs=[pl.BlockSpec((tm, d), lambda i: (i, 0)), pl.BlockSpec((1, 1, d), grp), pl.BlockSpec((1, 1, d), grp),
                  full(wrt_hi), full(wrt_lo), full(rbt), full(upper)],
        out_specs=[choice, choice, choice, pl.BlockSpec((n_exp, LANES), lambda i: (0, 0))],
        out_shape=[choice_out(I32), choice_out(F32), choice_out(I32), jax.ShapeDtypeStruct((n_exp, LANES), F32)],
        scratch_shapes=[pltpu.VMEM((n_exp, tm), F32)],
        compiler_params=_cparams(("arbitrary",)),
        name="route_shared",
    )(h1, shift, scale, wrt_hi, wrt_lo, rbt, upper)


def _dispatch_kernel(idx_ref, rank_ref, pstart_ref, h_ref, sh_ref, sc_ref, wsg_ref, wsu_ref, wsd_ref,
                     xs_hbm, slot_ref, shared_ref, rows, sem):
    u = h_ref[...] * (1.0 + sc_ref[0]) + sh_ref[0]
    for j in range(ROW_TILES):
        rows[pl.ds(j, DSP_TILE, stride=ROW_TILES), :] = u[:, j * LANES:(j + 1) * LANES]

    def body(i, carry):
        src = rows.at[pl.ds(pl.multiple_of(i * ROW_TILES, ROW_TILES), ROW_TILES)]
        for k in range(TOP_K):
            r = i * TOP_K + k
            slot = pstart_ref[idx_ref[r]] + rank_ref[r]
            slot_ref[r] = slot
            pltpu.make_async_copy(src, xs_hbm.at[slot], sem).start(priority=k % DMA_QUEUES)
        return carry

    lax.fori_loop(0, DSP_TILE, body, 0)

    ub = u.astype(BF16)
    shared_ref[...] = _dot((jax.nn.silu(_dot(ub, wsg_ref[...])) * _dot(ub, wsu_ref[...])).astype(BF16), wsd_ref[...])

    done = xs_hbm.at[pl.ds(0, DSP_TILE * TOP_K)]
    pltpu.make_async_copy(done, done, sem).wait()


def _dispatch(h1, shift, scale, wsg, wsu, wsd, idx, rank, pstart, n_slots, group_of_tile):
    t, d = h1.shape
    assert DSP_TILE == TOK_TILE
    choice = pl.BlockSpec((DSP_TILE * TOP_K,), lambda i: (i,), memory_space=pltpu.SMEM)
    grp = lambda i: (group_of_tile(i), 0, 0)
    full = lambda a: pl.BlockSpec(a.shape, lambda i: (0,) * a.ndim)
    tok = pl.BlockSpec((DSP_TILE, d), lambda i: (i, 0))
    return pl.pallas_call(
        _dispatch_kernel,
        grid=(t // DSP_TILE,),
        in_specs=[choice, choice, pl.BlockSpec(memory_space=pltpu.SMEM), tok,
                  pl.BlockSpec((1, 1, d), grp), pl.BlockSpec((1, 1, d), grp), full(wsg), full(wsu), full(wsd)],
        out_specs=[pl.BlockSpec(memory_space=pl.ANY), choice, tok],
        out_shape=[jax.ShapeDtypeStruct((n_slots, ROW_TILES, LANES), F32), jax.ShapeDtypeStruct(idx.shape, I32),
                   jax.ShapeDtypeStruct((t, d), F32)],
        scratch_shapes=[pltpu.VMEM((DSP_TILE * ROW_TILES, LANES), F32), pltpu.SemaphoreType.DMA(())],
        compiler_params=_cparams(("arbitrary",)),
        name="dispatch",
    )(idx, rank, pstart, h1, shift, scale, wsg, wsu, wsd)


def _expert_kernel(gstart_ref, nblk_ref, ntot_ref, xs_hbm, wg_ref, wu_ref, wd_ref, ys_hbm,
                   xbuf, ybuf, wg_b, wu_b, wd_b, sem_in, sem_out):
    e = pl.program_id(0)
    nb = nblk_ref[e]
    g0 = gstart_ref[e]
    total = ntot_ref[0]
    blk_rows = MOE_BLOCK * ROW_TILES
    ahead = EXP_BUFS - 1

    def rows_of(g):
        return pl.ds(pl.multiple_of(g * blk_rows, blk_rows), blk_rows)

    def in_copy(g):
        s = g % EXP_BUFS
        return pltpu.make_async_copy(xs_hbm.at[rows_of(g)], xbuf.at[s], sem_in.at[s])

    def out_copy(g):
        s = g % EXP_BUFS
        return pltpu.make_async_copy(ybuf.at[s], ys_hbm.at[rows_of(g)], sem_out.at[s])

    @pl.when(e == 0)
    def _():
        for g in range(ahead):
            @pl.when(g < total)
            def _():
                in_copy(g).start()

    @pl.when(nb > 0)
    def _():
        wg_b[...] = wg_ref[0, 0].astype(BF16)
        wu_b[...] = wu_ref[0, 0].astype(BF16)
        wd_b[...] = wd_ref[0, 0].astype(BF16)

    def body(b, carry):
        g = g0 + b
        s = g % EXP_BUFS

        @pl.when(g + ahead < total)
        def _():
            in_copy(g + ahead).start()

        in_copy(g).wait()

        @pl.when(g >= EXP_BUFS)
        def _():
            out_copy(g - EXP_BUFS).wait()

        x = jnp.concatenate([xbuf[s, pl.ds(j, MOE_BLOCK, stride=ROW_TILES), :] for j in range(ROW_TILES)], axis=-1)
        xb = x.astype(BF16)
        hid = jax.nn.silu(_dot(xb, wg_b[...])) * _dot(xb, wu_b[...])
        y = _dot(hid.astype(BF16), wd_b[...])
        for j in range(ROW_TILES):
            ybuf[s, pl.ds(j, MOE_BLOCK, stride=ROW_TILES), :] = y[:, j * LANES:(j + 1) * LANES]
        out_copy(g).start()
        return carry

    lax.fori_loop(0, nb, body, 0)

    @pl.when(e == pl.num_programs(0) - 1)
    def _():
        for back in range(EXP_BUFS):
            @pl.when(total - 1 - back >= 0)
            def _():
                out_copy(total - 1 - back).wait()


def _experts(xs2, gstart, nblk, ntot, layer, w_gate, w_up, w_down):
    n_rows = xs2.shape[0]
    _, n_exp, d, hdim = w_gate.shape
    blk_rows = MOE_BLOCK * ROW_TILES
    wmap = lambda e, gs, nb, nt: (layer, e, 0, 0)
    return pl.pallas_call(
        _expert_kernel,
        grid_spec=pltpu.PrefetchScalarGridSpec(
            num_scalar_prefetch=3,
            grid=(n_exp,),
            in_specs=[pl.BlockSpec(memory_space=pl.ANY),
                      pl.BlockSpec((1, 1, d, hdim), wmap),
                      pl.BlockSpec((1, 1, d, hdim), wmap),
                      pl.BlockSpec((1, 1, hdim, d), wmap)],
            out_specs=pl.BlockSpec(memory_space=pl.ANY),
            scratch_shapes=[pltpu.VMEM((EXP_BUFS, blk_rows, LANES), F32), pltpu.VMEM((EXP_BUFS, blk_rows, LANES), F32),
                            pltpu.VMEM((d, hdim), BF16), pltpu.VMEM((d, hdim), BF16), pltpu.VMEM((hdim, d), BF16),
                            pltpu.SemaphoreType.DMA((EXP_BUFS,)), pltpu.SemaphoreType.DMA((EXP_BUFS,))],
        ),
        out_shape=jax.ShapeDtypeStruct((n_rows, LANES), F32),
        compiler_params=_cparams(("arbitrary",)),
        name="experts",
    )(gstart, nblk, ntot, xs2, w_gate, w_up, w_down)


def _combine_kernel(slot_cur, slot_nxt, ys_hbm, wt_ref, shared_ref, h_ref, gt_ref, lng_ref, lnb_ref, o_ref,
                    buf, mixed, sem, *, alpha):
    i = pl.program_id(0)
    n = pl.num_programs(0)
    tt = CMB_TILE

    cur = i % 2
    unroll = 4

    def issue_token(slot_ref, b, tok):
        for k in range(TOP_K):
            r = tok * TOP_K + k
            dst = buf.at[b, pl.ds(pl.multiple_of(r * ROW_TILES, ROW_TILES), ROW_TILES)]
            pltpu.make_async_copy(ys_hbm.at[slot_ref[r]], dst, sem.at[b]).start(priority=k % DMA_QUEUES)

    def mix_token(tok):
        first = tok * TOP_K
        acc = None
        for k in range(TOP_K):
            row = buf[cur, pl.ds(pl.multiple_of((first + k) * ROW_TILES, ROW_TILES), ROW_TILES), :]
            term = wt_ref[first + k] * row
            acc = term if acc is None else acc + term
        mixed[pl.ds(pl.multiple_of(tok * ROW_TILES, ROW_TILES), ROW_TILES), :] = acc

    @pl.when(i == 0)
    def _():
        def body(tok, carry):
            issue_token(slot_cur, 0, tok)
            return carry
        lax.fori_loop(0, tt, body, 0)

    pltpu.make_async_copy(buf.at[cur], buf.at[cur], sem.at[cur]).wait()

    @pl.when(i + 1 < n)
    def _():
        def body(step, carry):
            for u in range(unroll):
                issue_token(slot_nxt, (i + 1) % 2, step * unroll + u)
                mix_token(step * unroll + u)
            return carry
        lax.fori_loop(0, tt // unroll, body, 0)

    @pl.when(i + 1 == n)
    def _():
        def body(step, carry):
            for u in range(unroll):
                mix_token(step * unroll + u)
            return carry
        lax.fori_loop(0, tt // unroll, body, 0)

    routed = jnp.concatenate([mixed[pl.ds(j, tt, stride=ROW_TILES), :] for j in range(ROW_TILES)], axis=-1)
    f = shared_ref[...] + routed
    v = alpha * h_ref[...] + gt_ref[0] * f
    o_ref[...] = _layer_norm(v, lng_ref[...], lnb_ref[...])


def _combine(ys3, slots, wts, shared, h1, gate2, ln_g, ln_b, group_of_tile, alpha, latent_only_tiles=None):
    t, d = h1.shape
    tt = CMB_TILE
    n = t // tt
    scale = TOK_TILE // tt
    if latent_only_tiles is None:
        out_rows, out_block = t, lambda i: (i, 0)
    else:
        assert tt == TOK_TILE
        per = latent_only_tiles
        out_rows = t - (n // per) * tt
        out_block = lambda i: ((i // per) * (per - 1) + jnp.maximum(i % per - 1, 0), 0)
    full = lambda a: pl.BlockSpec(a.shape, lambda i: (0,) * a.ndim)
    cur = pl.BlockSpec((tt * TOP_K,), lambda i: (i,), memory_space=pltpu.SMEM)
    nxt = pl.BlockSpec((tt * TOP_K,), lambda i: (jnp.minimum(i + 1, n - 1),), memory_space=pltpu.SMEM)
    return pl.pallas_call(
        functools.partial(_combine_kernel, alpha=alpha),
        grid=(n,),
        in_specs=[cur, nxt, pl.BlockSpec(memory_space=pl.ANY), cur,
                  pl.BlockSpec((tt, d), lambda i: (i, 0)),
                  pl.BlockSpec((tt, d), lambda i: (i, 0)),
                  pl.BlockSpec((1, 1, d), lambda i: (group_of_tile(i // scale), 0, 0)),
                  full(ln_g), full(ln_b)],
        out_specs=pl.BlockSpec((tt, d), out_block),
        out_shape=jax.ShapeDtypeStruct((out_rows, d), F32),
        scratch_shapes=[pltpu.VMEM((2, tt * TOP_K * ROW_TILES, LANES), F32), pltpu.VMEM((tt * ROW_TILES, LANES), F32),
                        pltpu.SemaphoreType.DMA((2,))],
        compiler_params=_cparams(("arbitrary",)),
        name="combine_ln2",
    )(slots, slots, ys3, wts, shared, h1, gate2, ln_g, ln_b)


def _reorder_w_in(w):
    d = w.shape[0]
    o = 0
    k = w[:, o:o + GLA_KEY]; o += GLA_KEY
    v = w[:, o:o + GLA_VAL]; o += GLA_VAL
    dec = w[:, o:o + 2 * GLA_LOWRANK]; o += 2 * GLA_LOWRANK
    q = w[:, o:o + GLA_KEY]; o += GLA_KEY
    g = w[:, o:o + GLA_VAL]; o += GLA_VAL
    conv = w[:, o:o + 3 * CONV_DIM]
    pad = jnp.zeros((d, LANES - 2 * GLA_LOWRANK), w.dtype)
    hk, hv = 2 * GLA_DK, 2 * GLA_DV
    pairs = [jnp.concatenate([v[:, i * hv:(i + 1) * hv], k[:, i * hk:(i + 1) * hk], q[:, i * hk:(i + 1) * hk]], axis=1)
             for i in range(2)]
    return jnp.concatenate([conv, g, pairs[0], pairs[1], dec, pad], axis=1).astype(BF16)


def _decay_weights(w_up, b_dec, pair):
    lo = pair * 2 * GLA_DK
    nr = 2 * GLA_LOWRANK
    outs = []
    for d in range(2):
        w = jnp.zeros((nr, LANES), F32).at[d * GLA_LOWRANK:(d + 1) * GLA_LOWRANK, :].set(w_up[d][:, lo:lo + LANES])
        w_hi = w.astype(BF16)
        w_lo = (w - w_hi.astype(F32)).astype(BF16)
        packed = jnp.concatenate([w_hi, w_hi, w_lo, jnp.zeros((LANES - 3 * nr, LANES), BF16)], axis=0)
        outs.append((packed, b_dec[d][lo:lo + LANES].reshape(1, LANES)))
    return outs[0][0], outs[1][0], outs[0][1], outs[1][1]


def _latent_to_scan_order(a, ctx_len, rows):
    b, _, f = a.shape
    return a[:, ctx_len:].reshape(b, rows, GRID_W, f).swapaxes(1, 2).reshape(b, rows * GRID_W, f)


def _latent_from_scan_order(a, ctx_len, rows):
    b, _, f = a.shape
    return a[:, ctx_len:].reshape(b, GRID_W, rows, f).swapaxes(1, 2).reshape(b * rows * GRID_W, f)


def kernel(x, c, ctx, c_ctx, w_mod, b_mod, w_in, w_conv, w_decay_up, b_decay, g_gla_norm, w_out, ln1_g, ln1_b, w_router, router_bias, w_e_gate, w_e_up, w_e_down, w_s_gate, w_s_up, w_s_down, ln2_g, ln2_b):
    bsz, seq, d = x.shape
    ctx_len = ctx.shape[1]
    depth = w_mod.shape[0]
    n_exp = w_router.shape[2]
    rows = seq // GRID_W
    l = ctx_len + seq
    t = bsz * l
    assert ctx_len == TOK_TILE and seq % TOK_TILE == 0 and d == ROW_TILES * LANES
    assert t % DSP_TILE == 0 and bsz + 1 <= SUBLANES and bsz % GLA_BATCH == 0
    tiles_per_batch = l // TOK_TILE
    alpha = float((2 * depth) ** 0.25)

    def group_of_tile(i):
        return jnp.where(i % tiles_per_batch == 0, bsz, i // tiles_per_batch)

    cond8 = jnp.zeros((SUBLANES, d), F32).at[:bsz].set(c).at[bsz].set(c_ctx)
    mod = _modulation(cond8, w_mod, b_mod).reshape(depth, SUBLANES, 6, 1, d)

    nk = t * TOP_K
    n_blocks = -(-(nk + n_exp * (MOE_BLOCK - 1)) // MOE_BLOCK)
    n_slots = n_blocks * MOE_BLOCK

    h = jnp.concatenate([ctx, x], axis=1).reshape(t, d)
    for i in range(depth):
        m = [mod[i, :, j] for j in range(6)]
        p = _inproj(h, m[0], m[1], _reorder_w_in(w_in[i]), group_of_tile)
        p3 = p.reshape(bsz, l, P_COLS)

        wf, wb, bf, bb = _decay_weights(w_decay_up[i], b_decay[i], 0)
        pair_cols = lambda base: ((base + PAIR_K) // LANES, (base + PAIR_Q) // LANES, (base + PAIR_V) // (2 * GLA_DV),
                                  P_DEC // LANES)
        o_fr, o_br = _gla_pair(p3, pair_cols(P_ROW), p3, pair_cols(P_ROW), wf, wb, bf, bb)
        col_in = _latent_to_scan_order(p3[..., P_COL:], ctx_len, rows)
        wf, wb, bf, bb = _decay_weights(w_decay_up[i], b_decay[i], 1)
        o_fc, o_bc = _gla_pair(p3, pair_cols(P_COL), col_in,
                               (PAIR_K // LANES, PAIR_Q // LANES, PAIR_V // (2 * GLA_DV), (P_DEC - P_COL) // LANES),
                               wf, wb, bf, bb)

        flat = lambda a: a.reshape(t, a.shape[-1])
        h1 = _outproj(p, flat(o_fr), flat(o_br), flat(o_fc), flat(o_bc), _latent_from_scan_order(o_fc, ctx_len, rows),
                      _latent_from_scan_order(o_bc, ctx_len, rows), h, m[2], w_conv[i],
                      g_gla_norm[i].reshape(1, GLA_DV), w_out[i].astype(BF16), ln1_g[i].reshape(1, d), ln1_b[i].reshape(1, d),
                      group_of_tile, alpha, tiles_per_batch)

        idx, wts, rank, counts = _route(h1, m[3], m[4], w_router[i], router_bias[i], group_of_tile)

        cnt = counts[:, 0].astype(I32)
        padded = (cnt + MOE_BLOCK - 1) // MOE_BLOCK * MOE_BLOCK
        pend = jnp.cumsum(padded)
        pstart = pend - padded
        idx, rank, wts = idx.T.reshape(nk), rank.T.reshape(nk), wts.T.reshape(nk)

        xs3, slots, shared = _dispatch(h1, m[3], m[4], w_s_gate[i].astype(BF16), w_s_up[i].astype(BF16),
                                       w_s_down[i].astype(BF16), idx, rank, pstart, n_slots, group_of_tile)
        ys2 = _experts(xs3.reshape(n_slots * ROW_TILES, LANES), pstart // MOE_BLOCK, padded // MOE_BLOCK,
                       (pend[-1] // MOE_BLOCK).reshape(1), i, w_e_gate, w_e_up, w_e_down)
        h = _combine(ys2.reshape(n_slots, ROW_TILES, LANES), slots, wts, shared, h1, m[5],
                     ln2_g[i].reshape(1, d), ln2_b[i].reshape(1, d), group_of_tile, alpha,
                     latent_only_tiles=tiles_per_batch if i == depth - 1 else None)

    return h.reshape(bsz, seq, d)
```

```python
import functools

import jax
import jax.numpy as jnp
from jax import lax
from jax.experimental import pallas as pl
from jax.experimental.pallas import tpu as pltpu

F32 = jnp.float32
BF16 = jnp.bfloat16
I32 = jnp.int32

GRID_W = 64
CONV_DIM = 512
GLA_HEADS = 4
GLA_DK = 64
GLA_DV = 128
GLA_KEY = GLA_HEADS * GLA_DK
GLA_VAL = GLA_HEADS * GLA_DV
GLA_LOWRANK = 16
GLA_GATE_TEMP = 16.0
GLA_CHUNK = 64
TOP_K = 8
N_GROUPS = 8
TOPK_GROUPS = 4
ROUTED_SCALE = 2.5
LN_EPS = 1e-5
RMS_EPS = 1e-6

LANES = 128
SUBLANES = 8
VMEM_LIMIT = 56 * 1024 * 1024

P_CONV = 0
P_G = 1536
P_ROW = 2048
P_COL = 2560
P_DEC = 3072
P_COLS = 3200
PAIR_V, PAIR_K, PAIR_Q = 0, 256, 384

GLA_BATCH = 4
TOK_TILE = 256
MOE_BLOCK = 256
EXP_BUFS = 6
CMB_TILE = 256
DSP_TILE = 256
ROW_TILES = 8
DMA_QUEUES = 2


def _cparams(sem):
    return pltpu.CompilerParams(dimension_semantics=sem, vmem_limit_bytes=VMEM_LIMIT)


def _mod_kernel(c_ref, w_ref, b_ref, o_ref):
    c = c_ref[...]
    a = c * jax.nn.sigmoid(c)
    o_ref[0] = jnp.dot(a, w_ref[0], preferred_element_type=F32, precision=lax.Precision.HIGHEST) + b_ref[0]


def _modulation(cond8, w_mod, b_mod):
    depth, d, n = w_mod.shape
    tn = 1536
    return pl.pallas_call(
        _mod_kernel,
        grid=(depth, n // tn),
        in_specs=[
            pl.BlockSpec((SUBLANES, d), lambda i, j: (0, 0)),
            pl.BlockSpec((1, d, tn), lambda i, j: (i, 0, j)),
            pl.BlockSpec((1, 1, tn), lambda i, j: (i, 0, j)),
        ],
        out_specs=pl.BlockSpec((1, SUBLANES, tn), lambda i, j: (i, 0, j)),
        out_shape=jax.ShapeDtypeStruct((depth, SUBLANES, n), F32),
        compiler_params=_cparams(("arbitrary", "arbitrary")),
        name="adaln_mod",
    )(cond8, w_mod, b_mod.reshape(depth, 1, n))


def _inproj_kernel(x_ref, sh_ref, sc_ref, w_ref, o_ref):
    h = x_ref[...] * (1.0 + sc_ref[0]) + sh_ref[0]
    o_ref[...] = jnp.dot(h.astype(BF16), w_ref[...], preferred_element_type=F32)


def _inproj(h, shift, scale, w, group_of_tile):
    t, d = h.shape
    n = w.shape[1]
    grp = lambda i: (group_of_tile(i), 0, 0)
    return pl.pallas_call(
        _inproj_kernel,
        grid=(t // TOK_TILE,),
        in_specs=[
            pl.BlockSpec((TOK_TILE, d), lambda i: (i, 0)),
            pl.BlockSpec((1, 1, d), grp),
            pl.BlockSpec((1, 1, d), grp),
            pl.BlockSpec((d, n), lambda i: (0, 0)),
        ],
        out_specs=pl.BlockSpec((TOK_TILE, n), lambda i: (i, 0)),
        out_shape=jax.ShapeDtypeStruct((t, n), F32),
        compiler_params=_cparams(("arbitrary",)),
        name="inproj",
    )(h, shift, scale, w)


def _dot(a, b):
    return jnp.dot(a, b, preferred_element_type=F32)


def _dot_nt(a, b):
    return lax.dot_general(a, b, (((1,), (1,)), ((), ())), preferred_element_type=F32)


def _gla_direction(k2, q2, v2, dec, wup, bdec, sums, tri, s_ref, o_ref, bi, reverse):
    t = TOK_TILE
    c = GLA_CHUNK
    d_hi = dec.astype(BF16).astype(F32)
    packed = d_hi + pltpu.roll(dec - d_hi, 2 * GLA_LOWRANK, 1) + pltpu.roll(d_hi, 4 * GLA_LOWRANK, 1)
    z = _dot(packed.astype(BF16), wup) + bdec
    la = (jnp.minimum(z, 0.0) - jnp.log(1.0 + jnp.exp(-jnp.abs(z)))) * (1.0 / GLA_GATE_TEMP)

    la_hi = la.astype(BF16)
    la_lo = (la - la_hi.astype(F32)).astype(BF16)
    acc = _dot(sums, jnp.concatenate([la_hi, la_lo], axis=-1))
    bcum = acc[:t, :LANES] + acc[:t, LANES:]
    btot = acc[t:, :LANES] + acc[t:, LANES:]

    q_dec = q2 * jnp.exp(bcum) * (GLA_DK ** -0.5)
    k_inv = (k2 * jnp.exp(-bcum)).astype(BF16)
    k_tail = k2 * jnp.exp(btot - bcum)
    chunk_decay = jnp.exp(btot)

    lane = lax.broadcasted_iota(I32, (t, 2 * GLA_DK), 1)
    keep = tri > 0.0
    order = range(t // c - 1, -1, -1) if reverse else range(t // c)
    for hh in range(2):
        in_head = (lane // GLA_DK) == hh
        qd = jnp.where(in_head, q_dec, 0.0).astype(BF16)
        kt = jnp.where(in_head, k_tail, 0.0).astype(BF16)
        vh = v2[:, hh * GLA_DV:(hh + 1) * GLA_DV]
        vh_b = vh.astype(BF16)
        att = jnp.where(keep, _dot_nt(qd, k_inv), 0.0).astype(BF16)
        o_intra = _dot(att, vh_b)
        s = s_ref[bi, hh]
        for ci in order:
            rows = slice(ci * c, (ci + 1) * c)
            o_ref[bi, rows, hh * GLA_DV:(hh + 1) * GLA_DV] = o_intra[rows] + _dot_nt(qd[rows], s.astype(BF16))
            kv_t = _dot(vh[rows].T.astype(BF16), kt[rows])
            s = s * chunk_decay[ci * c:ci * c + 1, :] + kv_t
        s_ref[bi, hh] = s


def _gla_kernel(kc, qc, vc, dc, kf, qf, vf, df, kb, qb, vb, db, wupf, wupb, bf, bb, sums_f, sums_b, tri_f, tri_b,
                of_ref, ob_ref, sf_ref, sb_ref):
    first = pl.program_id(1) == 0

    @pl.when(first)
    def _():
        sf_ref[...] = jnp.zeros_like(sf_ref)
        sb_ref[...] = jnp.zeros_like(sb_ref)

    for bi in range(GLA_BATCH):
        pick = lambda c, m: jnp.where(first, c[bi], m[bi])
        _gla_direction(pick(kc, kf), pick(qc, qf), pick(vc, vf), pick(dc, df), wupf[...], bf[...], sums_f[...], tri_f[...],
                       sf_ref, of_ref, bi, False)
        _gla_direction(pick(kc, kb), pick(qc, qb), pick(vc, vb), pick(dc, db), wupb[...], bb[...], sums_b[...], tri_b[...],
                       sb_ref, ob_ref, bi, True)


def _chunk_matrices():
    t = TOK_TILE
    r = jnp.arange(t)[:, None]
    q = jnp.arange(t)[None, :]
    same = (r // GLA_CHUNK) == (q // GLA_CHUNK)
    out = []
    for tri in (same & (q <= r), same & (q >= r)):
        out.append((jnp.concatenate([tri, same], axis=0).astype(BF16), tri.astype(F32)))
    return out[0][0], out[1][0], out[0][1], out[1][1]


def _gla_pair(ctx_src, ctx_cols, lat_src, lat_cols, wupf, wupb, bf, bb):
    bsz, l, _ = ctx_src.shape
    nb = l // TOK_TILE
    nl = nb - 1
    skip = lat_src.shape[1] // TOK_TILE - nl
    assert skip in (0, 1)
    fwd = lambda j: j
    bwd = lambda j: jnp.where(j == 0, 0, nb - j)
    lat_fwd = lambda j: jnp.maximum(j - 1, 0) + skip
    lat_bwd = lambda j: jnp.where(j == 0, nl - 1, nl - j) + skip

    def spec(width, colidx, order):
        return pl.BlockSpec((GLA_BATCH, TOK_TILE, width), lambda b, j: (b, order(j), colidx))

    widths = (LANES, LANES, 2 * GLA_DV, LANES)
    ctx_specs = [spec(w, c, lambda j: 0) for w, c in zip(widths, ctx_cols)]
    fwd_specs = [spec(w, c, lat_fwd) for w, c in zip(widths, lat_cols)]
    bwd_specs = [spec(w, c, lat_bwd) for w, c in zip(widths, lat_cols)]

    full = lambda a: pl.BlockSpec(a.shape, lambda b, j: (0,) * a.ndim)
    consts = _chunk_matrices()
    out = jax.ShapeDtypeStruct((bsz, l, 2 * GLA_DV), F32)
    return pl.pallas_call(
        _gla_kernel,
        grid=(bsz // GLA_BATCH, nb),
        in_specs=ctx_specs + fwd_specs + bwd_specs + [full(wupf), full(wupb), full(bf), full(bb)] + [full(a) for a in consts],
        out_specs=[pl.BlockSpec((GLA_BATCH, TOK_TILE, 2 * GLA_DV), lambda b, j: (b, fwd(j), 0)),
                   pl.BlockSpec((GLA_BATCH, TOK_TILE, 2 * GLA_DV), lambda b, j: (b, bwd(j), 0))],
        out_shape=[out, out],
        scratch_shapes=[pltpu.VMEM((GLA_BATCH, 2, GLA_DV, LANES), F32), pltpu.VMEM((GLA_BATCH, 2, GLA_DV, LANES), F32)],
        compiler_params=_cparams(("arbitrary", "arbitrary")),
        name="gla_pair",
    )(*([ctx_src] * 4 + [lat_src] * 8), wupf, wupb, bf, bb, *consts)


def _layer_norm(v, g, b):
    mu = jnp.mean(v, axis=-1, keepdims=True)
    dv = v - mu
    var = jnp.mean(dv * dv, axis=-1, keepdims=True)
    return dv * lax.rsqrt(var + LN_EPS) * g + b


def _outproj_kernel(conv_ref, gate_ref, ofr, obr, ofc_ctx, obc_ctx, ofc_lat, obc_lat, h_ref, gt_ref, wconv_ref, gn_ref,
                    wout_ref, lng_ref, lnb_ref, o_ref, *, alpha, tiles_per_batch):
    tm = TOK_TILE
    is_ctx = (pl.program_id(0) % tiles_per_batch) == 0
    seg = jnp.where(is_ctx, tm, GRID_W)

    cv = conv_ref[...]
    b_gate = cv[:, :CONV_DIM]
    u = cv[:, CONV_DIM:2 * CONV_DIM] * cv[:, 2 * CONV_DIM:]
    pos = lax.broadcasted_iota(I32, (tm, CONV_DIM), 0) & (seg - 1)
    prev = jnp.where(pos == 0, 0.0, pltpu.roll(u, 1, 0))
    nxt = jnp.where(pos == seg - 1, 0.0, pltpu.roll(u, tm - 1, 0))
    wc = wconv_ref[...]
    mix_conv = b_gate * (wc[0:1, :] * prev + wc[1:2, :] * u + wc[2:3, :] * nxt)

    o_r = ofr[...] + obr[...]
    o_c = jnp.where(is_ctx, ofc_ctx[...] + obc_ctx[...], ofc_lat[...] + obc_lat[...])
    gate = gate_ref[...]
    gn = gn_ref[...]
    parts = [mix_conv.astype(BF16)]
    for hd in range(GLA_HEADS):
        src = o_r if hd < 2 else o_c
        o = src[:, (hd % 2) * GLA_DV:(hd % 2 + 1) * GLA_DV]
        o = o * lax.rsqrt(jnp.mean(o * o, axis=-1, keepdims=True) + RMS_EPS) * gn
        g = gate[:, hd * GLA_DV:(hd + 1) * GLA_DV]
        parts.append((o * (g * jax.nn.sigmoid(g))).astype(BF16))
    mix = jnp.concatenate(parts, axis=-1)
    y = _dot(mix, wout_ref[...])
    v = alpha * h_ref[...] + gt_ref[0] * y
    o_ref[...] = _layer_norm(v, lng_ref[...], lnb_ref[...])


def _outproj(p, o_fr, o_br, o_fc, o_bc, o_fc_lat, o_bc_lat, h, gate1, w_conv, g_norm, w_out, ln_g, ln_b, group_of_tile,
             alpha, tiles_per_batch):
    t, d = h.shape
    tm = TOK_TILE
    per = tiles_per_batch
    tok = lambda width, colidx: pl.BlockSpec((tm, width), lambda i: (i, colidx))
    ctx_rows = pl.BlockSpec((tm, 2 * GLA_DV), lambda i: (i // per * per, 0))
    lat_rows = pl.BlockSpec((tm, 2 * GLA_DV), lambda i: (i // per * (per - 1) + jnp.maximum(i % per - 1, 0), 0))
    full = lambda a: pl.BlockSpec(a.shape, lambda i: (0,) * a.ndim)
    return pl.pallas_call(
        functools.partial(_outproj_kernel, alpha=alpha, tiles_per_batch=tiles_per_batch),
        grid=(t // tm,),
        in_specs=[tok(3 * CONV_DIM, P_CONV // (3 * CONV_DIM)), tok(GLA_VAL, P_G // GLA_VAL),
                  tok(2 * GLA_DV, 0), tok(2 * GLA_DV, 0), ctx_rows, ctx_rows, lat_rows, lat_rows,
                  tok(d, 0), pl.BlockSpec((1, 1, d), lambda i: (group_of_tile(i), 0, 0)),
                  full(w_conv), full(g_norm), full(w_out), full(ln_g), full(ln_b)],
        out_specs=pl.BlockSpec((tm, d), lambda i: (i, 0)),
        out_shape=jax.ShapeDtypeStruct((t, d), F32),
        compiler_params=_cparams(("arbitrary",)),
        name="outproj_ln1",
    )(p, p, o_fr, o_br, o_fc, o_bc, o_fc_lat, o_bc_lat, h, gate1, w_conv, g_norm, w_out, ln_g, ln_b)


def _route_kernel(h_ref, sh_ref, sc_ref, wrt_hi_ref, wrt_lo_ref, rbt_ref, upper_ref,
                  idx_ref, wt_ref, rank_ref, cnt_ref, base_ref):
    tm = TOK_TILE
    n_exp = rbt_ref.shape[0]
    gsz = n_exp // N_GROUPS

    @pl.when(pl.program_id(0) == 0)
    def _():
        base_ref[...] = jnp.zeros_like(base_ref)

    u = h_ref[...] * (1.0 + sc_ref[0]) + sh_ref[0]
    u_hi = u.astype(BF16)
    u_lo = (u - u_hi.astype(F32)).astype(BF16)

    wrt_hi = wrt_hi_ref[...]
    logits = _dot_nt(wrt_hi, u_hi) + _dot_nt(wrt_hi, u_lo) + _dot_nt(wrt_lo_ref[...], u_hi)
    s = jax.nn.sigmoid(logits)
    biased = s + rbt_ref[...]

    neg = -jnp.inf
    eidx = lax.broadcasted_iota(I32, (n_exp, tm), 0)
    loc = lax.broadcasted_iota(I32, (gsz, tm), 0)
    gscore = []
    for g in range(N_GROUPS):
        vals = biased[g * gsz:(g + 1) * gsz, :]
        m1 = jnp.max(vals, axis=0, keepdims=True)
        i1 = jnp.min(jnp.where(vals == m1, loc, gsz), axis=0, keepdims=True)
        m2 = jnp.max(jnp.where(loc == i1, neg, vals), axis=0, keepdims=True)
        gscore.append(m1 + m2)
    pieces = []
    for g in range(N_GROUPS):
        ahead = jnp.zeros((1, tm), F32)
        for g2 in range(N_GROUPS):
            if g2 == g:
                continue
            wins = (gscore[g2] >= gscore[g]) if g2 < g else (gscore[g2] > gscore[g])
            ahead = ahead + jnp.where(wins, 1.0, 0.0)
        pieces.append(jnp.where(ahead < TOPK_GROUPS, biased[g * gsz:(g + 1) * gsz, :], neg))
    masked = jnp.concatenate(pieces, axis=0)

    onehot = jnp.zeros((n_exp, tm), F32)
    idxs = []
    for _ in range(TOP_K):
        m = jnp.max(masked, axis=0, keepdims=True)
        i = jnp.min(jnp.where(masked == m, eidx, n_exp), axis=0, keepdims=True)
        hit = eidx == i
        onehot = jnp.where(hit, 1.0, onehot)
        masked = jnp.where(hit, neg, masked)
        idxs.append(i)
    sel = onehot * s
    wnorm = sel / jnp.sum(sel, axis=0, keepdims=True) * ROUTED_SCALE

    onehot_b = onehot.astype(BF16)
    upper = upper_ref[...]
    rank_all = _dot(onehot_b, upper) + base_ref[...]
    base_new = base_ref[...] + _dot(onehot_b, jnp.ones((tm, tm), BF16))
    base_ref[...] = base_new
    cnt_ref[...] = base_new[:, :LANES]

    wts, ranks = [], []
    for k in range(TOP_K):
        hit = eidx == idxs[k]
        wts.append(jnp.sum(jnp.where(hit, wnorm, 0.0), axis=0, keepdims=True))
        ranks.append(jnp.sum(jnp.where(hit, rank_all, 0.0), axis=0, keepdims=True))
    idx_ref[...] = jnp.concatenate(idxs, axis=0)
    wt_ref[...] = jnp.concatenate(wts, axis=0)
    rank_ref[...] = jnp.concatenate(ranks, axis=0).astype(I32)


def _route(h1, shift, scale, wr, rbias, group_of_tile):
    t, d = h1.shape
    tm = TOK_TILE
    n_exp = wr.shape[1]
    wrt = wr.T
    wrt_hi = wrt.astype(BF16)
    wrt_lo = (wrt - wrt_hi.astype(F32)).astype(BF16)
    rbt = jnp.broadcast_to(rbias.reshape(n_exp, 1), (n_exp, tm))
    upper = (jnp.arange(tm)[:, None] < jnp.arange(tm)[None, :]).astype(BF16)
    grp = lambda i: (group_of_tile(i), 0, 0)
    full = lambda a: pl.BlockSpec(a.shape, lambda i: (0,) * a.ndim)
    choice = pl.BlockSpec((TOP_K, tm), lambda i: (0, i))
    choice_out = lambda dt: jax.ShapeDtypeStruct((TOP_K, t), dt)
    return pl.pallas_call(
        _route_kernel,
        grid=(t // tm,),
        in_specs=[pl.BlockSpec((tm, d), lambda i: (i, 0)), pl.BlockSpec((1, 1, d), grp), pl.BlockSpec((1, 1, d), grp),
                  full(wrt_hi), full(wrt_lo), full(rbt), full(upper)],
        out_specs=[choice, choice, choice, pl.BlockSpec((n_exp, LANES), lambda i: (0, 0))],
        out_shape=[choice_out(I32), choice_out(F32), choice_out(I32), jax.ShapeDtypeStruct((n_exp, LANES), F32)],
        scratch_shapes=[pltpu.VMEM((n_exp, tm), F32)],
        compiler_params=_cparams(("arbitrary",)),
        name="route_shared",
    )(h1, shift, scale, wrt_hi, wrt_lo, rbt, upper)


def _dispatch_kernel(idx_ref, rank_ref, pstart_ref, h_ref, sh_ref, sc_ref, hn_ref, shn_ref, scn_ref,
                     wsg_ref, wsu_ref, wsd_ref, xs_hbm, slot_ref, shared_ref, rows, sem):
    step = pl.program_id(0)
    cur = step % 2

    def prepare(h, shift, scale, half):
        u = h[...] * (1.0 + scale[0]) + shift[0]
        dst = rows.at[half]
        for j in range(ROW_TILES):
            dst[pl.ds(j, DSP_TILE, stride=ROW_TILES), :] = u[:, j * LANES:(j + 1) * LANES]

    @pl.when(step == 0)
    def _():
        prepare(h_ref, sh_ref, sc_ref, 0)

    def body(i, carry):
        src = rows.at[cur, pl.ds(pl.multiple_of(i * ROW_TILES, ROW_TILES), ROW_TILES)]
        for k in range(TOP_K):
            r = i * TOP_K + k
            slot = pstart_ref[idx_ref[r]] + rank_ref[r]
            slot_ref[r] = slot
            pltpu.make_async_copy(src, xs_hbm.at[slot], sem).start(priority=k % DMA_QUEUES)
        return carry

    lax.fori_loop(0, DSP_TILE, body, 0)

    @pl.when(step + 1 < pl.num_programs(0))
    def _():
        prepare(hn_ref, shn_ref, scn_ref, 1 - cur)

    ub = (h_ref[...] * (1.0 + sc_ref[0]) + sh_ref[0]).astype(BF16)
    shared_ref[...] = _dot((jax.nn.silu(_dot(ub, wsg_ref[...])) * _dot(ub, wsu_ref[...])).astype(BF16), wsd_ref[...])

    done = xs_hbm.at[pl.ds(0, DSP_TILE * TOP_K)]
    pltpu.make_async_copy(done, done, sem).wait()


def _dispatch(h1, shift, scale, wsg, wsu, wsd, idx, rank, pstart, n_slots, group_of_tile):
    t, d = h1.shape
    assert DSP_TILE == TOK_TILE
    n = t // DSP_TILE
    choice = pl.BlockSpec((DSP_TILE * TOP_K,), lambda i: (i,), memory_space=pltpu.SMEM)
    nxt = lambda i: jnp.minimum(i + 1, n - 1)
    grp = lambda i: (group_of_tile(i), 0, 0)
    grp_nxt = lambda i: (group_of_tile(nxt(i)), 0, 0)
    full = lambda a: pl.BlockSpec(a.shape, lambda i: (0,) * a.ndim)
    tok = pl.BlockSpec((DSP_TILE, d), lambda i: (i, 0))
    return pl.pallas_call(
        _dispatch_kernel,
        grid=(n,),
        in_specs=[choice, choice, pl.BlockSpec(memory_space=pltpu.SMEM), tok,
                  pl.BlockSpec((1, 1, d), grp), pl.BlockSpec((1, 1, d), grp),
                  pl.BlockSpec((DSP_TILE, d), lambda i: (nxt(i), 0)),
                  pl.BlockSpec((1, 1, d), grp_nxt), pl.BlockSpec((1, 1, d), grp_nxt),
                  full(wsg), full(wsu), full(wsd)],
        out_specs=[pl.BlockSpec(memory_space=pl.ANY), choice, tok],
        out_shape=[jax.ShapeDtypeStruct((n_slots, ROW_TILES, LANES), F32), jax.ShapeDtypeStruct(idx.shape, I32),
                   jax.ShapeDtypeStruct((t, d), F32)],
        scratch_shapes=[pltpu.VMEM((2, DSP_TILE * ROW_TILES, LANES), F32), pltpu.SemaphoreType.DMA(())],
        compiler_params=_cparams(("arbitrary",)),
        name="dispatch",
    )(idx, rank, pstart, h1, shift, scale, h1, shift, scale, wsg, wsu, wsd)


def _expert_kernel(gstart_ref, nblk_ref, ntot_ref, xs_hbm, wg_ref, wu_ref, wd_ref, ys_hbm,
                   xbuf, ybuf, wg_b, wu_b, wd_b, sem_in, sem_out):
    e = pl.program_id(0)
    nb = nblk_ref[e]
    g0 = gstart_ref[e]
    total = ntot_ref[0]
    blk_rows = MOE_BLOCK * ROW_TILES
    ahead = EXP_BUFS - 1

    def rows_of(g):
        return pl.ds(pl.multiple_of(g * blk_rows, blk_rows), blk_rows)

    def in_copy(g):
        s = g % EXP_BUFS
        return pltpu.make_async_copy(xs_hbm.at[rows_of(g)], xbuf.at[s], sem_in.at[s])

    def out_copy(g):
        s = g % EXP_BUFS
        return pltpu.make_async_copy(ybuf.at[s], ys_hbm.at[rows_of(g)], sem_out.at[s])

    @pl.when(e == 0)
    def _():
        for g in range(ahead):
            @pl.when(g < total)
            def _():
                in_copy(g).start()

    @pl.when(nb > 0)
    def _():
        wg_b[...] = wg_ref[0, 0].astype(BF16)
        wu_b[...] = wu_ref[0, 0].astype(BF16)
        wd_b[...] = wd_ref[0, 0].astype(BF16)

    def body(b, carry):
        g = g0 + b
        s = g % EXP_BUFS

        @pl.when(g + ahead < total)
        def _():
            in_copy(g + ahead).start()

        in_copy(g).wait()

        @pl.when(g >= EXP_BUFS)
        def _():
            out_copy(g - EXP_BUFS).wait()

        x = jnp.concatenate([xbuf[s, pl.ds(j, MOE_BLOCK, stride=ROW_TILES), :] for j in range(ROW_TILES)], axis=-1)
        xb = x.astype(BF16)
        hid = jax.nn.silu(_dot(xb, wg_b[...])) * _dot(xb, wu_b[...])
        y = _dot(hid.astype(BF16), wd_b[...])
        for j in range(ROW_TILES):
            ybuf[s, pl.ds(j, MOE_BLOCK, stride=ROW_TILES), :] = y[:, j * LANES:(j + 1) * LANES]
        out_copy(g).start()
        return carry

    lax.fori_loop(0, nb, body, 0)

    @pl.when(e == pl.num_programs(0) - 1)
    def _():
        for back in range(EXP_BUFS):
            @pl.when(total - 1 - back >= 0)
            def _():
                out_copy(total - 1 - back).wait()


def _experts(xs2, gstart, nblk, ntot, layer, w_gate, w_up, w_down):
    n_rows = xs2.shape[0]
    _, n_exp, d, hdim = w_gate.shape
    blk_rows = MOE_BLOCK * ROW_TILES
    wmap = lambda e, gs, nb, nt: (layer, e, 0, 0)
    return pl.pallas_call(
        _expert_kernel,
        grid_spec=pltpu.PrefetchScalarGridSpec(
            num_scalar_prefetch=3,
            grid=(n_exp,),
            in_specs=[pl.BlockSpec(memory_space=pl.ANY),
                      pl.BlockSpec((1, 1, d, hdim), wmap),
                      pl.BlockSpec((1, 1, d, hdim), wmap),
                      pl.BlockSpec((1, 1, hdim, d), wmap)],
            out_specs=pl.BlockSpec(memory_space=pl.ANY),
            scratch_shapes=[pltpu.VMEM((EXP_BUFS, blk_rows, LANES), F32), pltpu.VMEM((EXP_BUFS, blk_rows, LANES), F32),
                            pltpu.VMEM((d, hdim), BF16), pltpu.VMEM((d, hdim), BF16), pltpu.VMEM((hdim, d), BF16),
                            pltpu.SemaphoreType.DMA((EXP_BUFS,)), pltpu.SemaphoreType.DMA((EXP_BUFS,))],
        ),
        out_shape=jax.ShapeDtypeStruct((n_rows, LANES), F32),
        compiler_params=_cparams(("arbitrary",)),
        name="experts",
    )(gstart, nblk, ntot, xs2, w_gate, w_up, w_down)


def _combine_kernel(slot_cur, slot_nxt, ys_hbm, wt_ref, shared_ref, h_ref, gt_ref, lng_ref, lnb_ref, o_ref,
                    buf, mixed, sem, *, alpha):
    i = pl.program_id(0)
    n = pl.num_programs(0)
    tt = CMB_TILE

    cur = i % 2
    unroll = 4

    def issue_token(slot_ref, b, tok):
        for k in range(TOP_K):
            r = tok * TOP_K + k
            dst = buf.at[b, pl.ds(pl.multiple_of(r * ROW_TILES, ROW_TILES), ROW_TILES)]
            pltpu.make_async_copy(ys_hbm.at[slot_ref[r]], dst, sem.at[b]).start(priority=k % DMA_QUEUES)

    def mix_token(tok):
        first = tok * TOP_K
        acc = None
        for k in range(TOP_K):
            row = buf[cur, pl.ds(pl.multiple_of((first + k) * ROW_TILES, ROW_TILES), ROW_TILES), :]
            term = wt_ref[first + k] * row
            acc = term if acc is None else acc + term
        mixed[pl.ds(pl.multiple_of(tok * ROW_TILES, ROW_TILES), ROW_TILES), :] = acc

    @pl.when(i == 0)
    def _():
        def body(tok, carry):
            issue_token(slot_cur, 0, tok)
            return carry
        lax.fori_loop(0, tt, body, 0)

    pltpu.make_async_copy(buf.at[cur], buf.at[cur], sem.at[cur]).wait()

    @pl.when(i + 1 < n)
    def _():
        def body(step, carry):
            for u in range(unroll):
                issue_token(slot_nxt, (i + 1) % 2, step * unroll + u)
                mix_token(step * unroll + u)
            return carry
        lax.fori_loop(0, tt // unroll, body, 0)

    @pl.when(i + 1 == n)
    def _():
        def body(step, carry):
            for u in range(unroll):
                mix_token(step * unroll + u)
            return carry
        lax.fori_loop(0, tt // unroll, body, 0)

    routed = jnp.concatenate([mixed[pl.ds(j, tt, stride=ROW_TILES), :] for j in range(ROW_TILES)], axis=-1)
    f = shared_ref[...] + routed
    v = alpha * h_ref[...] + gt_ref[0] * f
    o_ref[...] = _layer_norm(v, lng_ref[...], lnb_ref[...])


def _combine(ys3, slots, wts, shared, h1, gate2, ln_g, ln_b, group_of_tile, alpha, latent_only_tiles=None):
    t, d = h1.shape
    tt = CMB_TILE
    n = t // tt
    scale = TOK_TILE // tt
    if latent_only_tiles is None:
        out_rows, out_block = t, lambda i: (i, 0)
    else:
        assert tt == TOK_TILE
        per = latent_only_tiles
        out_rows = t - (n // per) * tt
        out_block = lambda i: ((i // per) * (per - 1) + jnp.maximum(i % per - 1, 0), 0)
    full = lambda a: pl.BlockSpec(a.shape, lambda i: (0,) * a.ndim)
    cur = pl.BlockSpec((tt * TOP_K,), lambda i: (i,), memory_space=pltpu.SMEM)
    nxt = pl.BlockSpec((tt * TOP_K,), lambda i: (jnp.minimum(i + 1, n - 1),), memory_space=pltpu.SMEM)
    return pl.pallas_call(
        functools.partial(_combine_kernel, alpha=alpha),
        grid=(n,),
        in_specs=[cur, nxt, pl.BlockSpec(memory_space=pl.ANY), cur,
                  pl.BlockSpec((tt, d), lambda i: (i, 0)),
                  pl.BlockSpec((tt, d), lambda i: (i, 0)),
                  pl.BlockSpec((1, 1, d), lambda i: (group_of_tile(i // scale), 0, 0)),
                  full(ln_g), full(ln_b)],
        out_specs=pl.BlockSpec((tt, d), out_block),
        out_shape=jax.ShapeDtypeStruct((out_rows, d), F32),
        scratch_shapes=[pltpu.VMEM((2, tt * TOP_K * ROW_TILES, LANES), F32), pltpu.VMEM((tt * ROW_TILES, LANES), F32),
                        pltpu.SemaphoreType.DMA((2,))],
        compiler_params=_cparams(("arbitrary",)),
        name="combine_ln2",
    )(slots, slots, ys3, wts, shared, h1, gate2, ln_g, ln_b)


def _reorder_w_in(w):
    d = w.shape[0]
    o = 0
    k = w[:, o:o + GLA_KEY]; o += GLA_KEY
    v = w[:, o:o + GLA_VAL]; o += GLA_VAL
    dec = w[:, o:o + 2 * GLA_LOWRANK]; o += 2 * GLA_LOWRANK
    q = w[:, o:o + GLA_KEY]; o += GLA_KEY
    g = w[:, o:o + GLA_VAL]; o += GLA_VAL
    conv = w[:, o:o + 3 * CONV_DIM]
    pad = jnp.zeros((d, LANES - 2 * GLA_LOWRANK), w.dtype)
    hk, hv = 2 * GLA_DK, 2 * GLA_DV
    pairs = [jnp.concatenate([v[:, i * hv:(i + 1) * hv], k[:, i * hk:(i + 1) * hk], q[:, i * hk:(i + 1) * hk]], axis=1)
             for i in range(2)]
    return jnp.concatenate([conv, g, pairs[0], pairs[1], dec, pad], axis=1).astype(BF16)


def _decay_weights(w_up, b_dec, pair):
    lo = pair * 2 * GLA_DK
    nr = 2 * GLA_LOWRANK
    outs = []
    for d in range(2):
        w = jnp.zeros((nr, LANES), F32).at[d * GLA_LOWRANK:(d + 1) * GLA_LOWRANK, :].set(w_up[d][:, lo:lo + LANES])
        w_hi = w.astype(BF16)
        w_lo = (w - w_hi.astype(F32)).astype(BF16)
        packed = jnp.concatenate([w_hi, w_hi, w_lo, jnp.zeros((LANES - 3 * nr, LANES), BF16)], axis=0)
        outs.append((packed, b_dec[d][lo:lo + LANES].reshape(1, LANES)))
    return outs[0][0], outs[1][0], outs[0][1], outs[1][1]


def _latent_to_scan_order(a, ctx_len, rows):
    b, _, f = a.shape
    return a[:, ctx_len:].reshape(b, rows, GRID_W, f).swapaxes(1, 2).reshape(b, rows * GRID_W, f)


def _latent_from_scan_order(a, ctx_len, rows):
    b, _, f = a.shape
    return a[:, ctx_len:].reshape(b, GRID_W, rows, f).swapaxes(1, 2).reshape(b * rows * GRID_W, f)


def kernel(x, c, ctx, c_ctx, w_mod, b_mod, w_in, w_conv, w_decay_up, b_decay, g_gla_norm, w_out, ln1_g, ln1_b, w_router, router_bias, w_e_gate, w_e_up, w_e_down, w_s_gate, w_s_up, w_s_down, ln2_g, ln2_b):
    bsz, seq, d = x.shape
    ctx_len = ctx.shape[1]
    depth = w_mod.shape[0]
    n_exp = w_router.shape[2]
    rows = seq // GRID_W
    l = ctx_len + seq
    t = bsz * l
    assert ctx_len == TOK_TILE and seq % TOK_TILE == 0 and d == ROW_TILES * LANES
    assert t % DSP_TILE == 0 and bsz + 1 <= SUBLANES and bsz % GLA_BATCH == 0
    tiles_per_batch = l // TOK_TILE
    alpha = float((2 * depth) ** 0.25)

    def group_of_tile(i):
        return jnp.where(i % tiles_per_batch == 0, bsz, i // tiles_per_batch)

    cond8 = jnp.zeros((SUBLANES, d), F32).at[:bsz].set(c).at[bsz].set(c_ctx)
    mod = _modulation(cond8, w_mod, b_mod).reshape(depth, SUBLANES, 6, 1, d)

    nk = t * TOP_K
    n_blocks = -(-(nk + n_exp * (MOE_BLOCK - 1)) // MOE_BLOCK)
    n_slots = n_blocks * MOE_BLOCK

    h = jnp.concatenate([ctx, x], axis=1).reshape(t, d)
    for i in range(depth):
        m = [mod[i, :, j] for j in range(6)]
        p = _inproj(h, m[0], m[1], _reorder_w_in(w_in[i]), group_of_tile)
        p3 = p.reshape(bsz, l, P_COLS)

        wf, wb, bf, bb = _decay_weights(w_decay_up[i], b_decay[i], 0)
        pair_cols = lambda base: ((base + PAIR_K) // LANES, (base + PAIR_Q) // LANES, (base + PAIR_V) // (2 * GLA_DV),
                                  P_DEC // LANES)
        o_fr, o_br = _gla_pair(p3, pair_cols(P_ROW), p3, pair_cols(P_ROW), wf, wb, bf, bb)
        col_in = _latent_to_scan_order(p3[..., P_COL:], ctx_len, rows)
        wf, wb, bf, bb = _decay_weights(w_decay_up[i], b_decay[i], 1)
        o_fc, o_bc = _gla_pair(p3, pair_cols(P_COL), col_in,
                               (PAIR_K // LANES, PAIR_Q // LANES, PAIR_V // (2 * GLA_DV), (P_DEC - P_COL) // LANES),
                               wf, wb, bf, bb)

        flat = lambda a: a.reshape(t, a.shape[-1])
        h1 = _outproj(p, flat(o_fr), flat(o_br), flat(o_fc), flat(o_bc), _latent_from_scan_order(o_fc, ctx_len, rows),
                      _latent_from_scan_order(o_bc, ctx_len, rows), h, m[2], w_conv[i],
                      g_gla_norm[i].reshape(1, GLA_DV), w_out[i].astype(BF16), ln1_g[i].reshape(1, d), ln1_b[i].reshape(1, d),
                      group_of_tile, alpha, tiles_per_batch)

        idx, wts, rank, counts = _route(h1, m[3], m[4], w_router[i], router_bias[i], group_of_tile)

        cnt = counts[:, 0].astype(I32)
        padded = (cnt + MOE_BLOCK - 1) // MOE_BLOCK * MOE_BLOCK
        pend = jnp.cumsum(padded)
        pstart = pend - padded
        idx, rank, wts = idx.T.reshape(nk), rank.T.reshape(nk), wts.T.reshape(nk)

        xs3, slots, shared = _dispatch(h1, m[3], m[4], w_s_gate[i].astype(BF16), w_s_up[i].astype(BF16),
                                       w_s_down[i].astype(BF16), idx, rank, pstart, n_slots, group_of_tile)
        ys2 = _experts(xs3.reshape(n_slots * ROW_TILES, LANES), pstart // MOE_BLOCK, padded // MOE_BLOCK,
                       (pend[-1] // MOE_BLOCK).reshape(1), i, w_e_gate, w_e_up, w_e_down)
        h = _combine(ys2.reshape(n_slots, ROW_TILES, LANES), slots, wts, shared, h1, m[5],
                     ln2_g[i].reshape(1, d), ln2_b[i].reshape(1, d), group_of_tile, alpha,
                     latent_only_tiles=tiles_per_batch if i == depth - 1 else None)

    return h.reshape(bsz, seq, d)
```

```python
import functools

import jax
import jax.numpy as jnp
from jax import lax
from jax.experimental import pallas as pl
from jax.experimental.pallas import tpu as pltpu

F32 = jnp.float32
BF16 = jnp.bfloat16
I32 = jnp.int32

GRID_W = 64
CONV_DIM = 512
GLA_HEADS = 4
GLA_DK = 64
GLA_DV = 128
GLA_KEY = GLA_HEADS * GLA_DK
GLA_VAL = GLA_HEADS * GLA_DV
GLA_LOWRANK = 16
GLA_GATE_TEMP = 16.0
GLA_CHUNK = 64
TOP_K = 8
N_GROUPS = 8
TOPK_GROUPS = 4
ROUTED_SCALE = 2.5
LN_EPS = 1e-5
RMS_EPS = 1e-6

LANES = 128
SUBLANES = 8
VMEM_LIMIT = 56 * 1024 * 1024

P_CONV = 0
P_G = 1536
P_ROW = 2048
P_COL = 2560
P_DEC = 3072
P_COLS = 3200
PAIR_V, PAIR_K, PAIR_Q = 0, 256, 384

GLA_BATCH = 4
TOK_TILE = 256
MOE_BLOCK = 256
EXP_BUFS = 6
CMB_TILE = 256
DSP_TILE = 256
ROW_TILES = 8
DMA_QUEUES = 2


def _cparams(sem):
    return pltpu.CompilerParams(dimension_semantics=sem, vmem_limit_bytes=VMEM_LIMIT)


def _mod_kernel(c_ref, w_ref, b_ref, o_ref):
    c = c_ref[...]
    a = c * jax.nn.sigmoid(c)
    o_ref[0] = jnp.dot(a, w_ref[0], preferred_element_type=F32, precision=lax.Precision.HIGHEST) + b_ref[0]


def _modulation(cond8, w_mod, b_mod):
    depth, d, n = w_mod.shape
    tn = 1536
    return pl.pallas_call(
        _mod_kernel,
        grid=(depth, n // tn),
        in_specs=[
            pl.BlockSpec((SUBLANES, d), lambda i, j: (0, 0)),
            pl.BlockSpec((1, d, tn), lambda i, j: (i, 0, j)),
            pl.BlockSpec((1, 1, tn), lambda i, j: (i, 0, j)),
        ],
        out_specs=pl.BlockSpec((1, SUBLANES, tn), lambda i, j: (i, 0, j)),
        out_shape=jax.ShapeDtypeStruct((depth, SUBLANES, n), F32),
        compiler_params=_cparams(("arbitrary", "arbitrary")),
        name="adaln_mod",
    )(cond8, w_mod, b_mod.reshape(depth, 1, n))


def _inproj_kernel(x_ref, sh_ref, sc_ref, w_ref, o_ref):
    h = x_ref[...] * (1.0 + sc_ref[0]) + sh_ref[0]
    o_ref[...] = jnp.dot(h.astype(BF16), w_ref[...], preferred_element_type=F32)


def _inproj(h, shift, scale, w, group_of_tile):
    t, d = h.shape
    n = w.shape[1]
    grp = lambda i: (group_of_tile(i), 0, 0)
    return pl.pallas_call(
        _inproj_kernel,
        grid=(t // TOK_TILE,),
        in_specs=[
            pl.BlockSpec((TOK_TILE, d), lambda i: (i, 0)),
            pl.BlockSpec((1, 1, d), grp),
            pl.BlockSpec((1, 1, d), grp),
            pl.BlockSpec((d, n), lambda i: (0, 0)),
        ],
        out_specs=pl.BlockSpec((TOK_TILE, n), lambda i: (i, 0)),
        out_shape=jax.ShapeDtypeStruct((t, n), F32),
        compiler_params=_cparams(("arbitrary",)),
        name="inproj",
    )(h, shift, scale, w)


def _dot(a, b):
    return jnp.dot(a, b, preferred_element_type=F32)


def _dot_nt(a, b):
    return lax.dot_general(a, b, (((1,), (1,)), ((), ())), preferred_element_type=F32)


def _gla_direction(k2, q2, v2, dec, wup, bdec, sums, tri, s_ref, o_ref, bi, reverse):
    t = TOK_TILE
    c = GLA_CHUNK
    d_hi = dec.astype(BF16).astype(F32)
    packed = d_hi + pltpu.roll(dec - d_hi, 2 * GLA_LOWRANK, 1) + pltpu.roll(d_hi, 4 * GLA_LOWRANK, 1)
    z = _dot(packed.astype(BF16), wup) + bdec
    la = (jnp.minimum(z, 0.0) - jnp.log(1.0 + jnp.exp(-jnp.abs(z)))) * (1.0 / GLA_GATE_TEMP)

    la_hi = la.astype(BF16)
    la_lo = (la - la_hi.astype(F32)).astype(BF16)
    acc = _dot(sums, jnp.concatenate([la_hi, la_lo], axis=-1))
    bcum = acc[:t, :LANES] + acc[:t, LANES:]
    btot = acc[t:, :LANES] + acc[t:, LANES:]

    q_dec = q2 * jnp.exp(bcum) * (GLA_DK ** -0.5)
    k_inv = (k2 * jnp.exp(-bcum)).astype(BF16)
    k_tail = k2 * jnp.exp(btot - bcum)
    chunk_decay = jnp.exp(btot)

    lane = lax.broadcasted_iota(I32, (t, 2 * GLA_DK), 1)
    keep = tri > 0.0
    order = range(t // c - 1, -1, -1) if reverse else range(t // c)
    for hh in range(2):
        in_head = (lane // GLA_DK) == hh
        qd = jnp.where(in_head, q_dec, 0.0).astype(BF16)
        kt = jnp.where(in_head, k_tail, 0.0).astype(BF16)
        vh = v2[:, hh * GLA_DV:(hh + 1) * GLA_DV]
        vh_b = vh.astype(BF16)
        att = jnp.where(keep, _dot_nt(qd, k_inv), 0.0).astype(BF16)
        o_intra = _dot(att, vh_b)
        s = s_ref[bi, hh]
        for ci in order:
            rows = slice(ci * c, (ci + 1) * c)
            o_ref[bi, rows, hh * GLA_DV:(hh + 1) * GLA_DV] = o_intra[rows] + _dot_nt(qd[rows], s.astype(BF16))
            kv_t = _dot(vh[rows].T.astype(BF16), kt[rows])
            s = s * chunk_decay[ci * c:ci * c + 1, :] + kv_t
        s_ref[bi, hh] = s


def _gla_kernel(kc, qc, vc, dc, kf, qf, vf, df, kb, qb, vb, db, wupf, wupb, bf, bb, sums_f, sums_b, tri_f, tri_b,
                of_ref, ob_ref, sf_ref, sb_ref):
    first = pl.program_id(1) == 0

    @pl.when(first)
    def _():
        sf_ref[...] = jnp.zeros_like(sf_ref)
        sb_ref[...] = jnp.zeros_like(sb_ref)

    for bi in range(GLA_BATCH):
        pick = lambda c, m: jnp.where(first, c[bi], m[bi])
        _gla_direction(pick(kc, kf), pick(qc, qf), pick(vc, vf), pick(dc, df), wupf[...], bf[...], sums_f[...], tri_f[...],
                       sf_ref, of_ref, bi, False)
        _gla_direction(pick(kc, kb), pick(qc, qb), pick(vc, vb), pick(dc, db), wupb[...], bb[...], sums_b[...], tri_b[...],
                       sb_ref, ob_ref, bi, True)


def _chunk_matrices():
    t = TOK_TILE
    r = jnp.arange(t)[:, None]
    q = jnp.arange(t)[None, :]
    same = (r // GLA_CHUNK) == (q // GLA_CHUNK)
    out = []
    for tri in (same & (q <= r), same & (q >= r)):
        out.append((jnp.concatenate([tri, same], axis=0).astype(BF16), tri.astype(F32)))
    return out[0][0], out[1][0], out[0][1], out[1][1]


def _gla_pair(ctx_src, ctx_cols, lat_src, lat_cols, wupf, wupb, bf, bb):
    bsz, l, _ = ctx_src.shape
    nb = l // TOK_TILE
    nl = nb - 1
    skip = lat_src.shape[1] // TOK_TILE - nl
    assert skip in (0, 1)
    fwd = lambda j: j
    bwd = lambda j: jnp.where(j == 0, 0, nb - j)
    lat_fwd = lambda j: jnp.maximum(j - 1, 0) + skip
    lat_bwd = lambda j: jnp.where(j == 0, nl - 1, nl - j) + skip

    def spec(width, colidx, order):
        return pl.BlockSpec((GLA_BATCH, TOK_TILE, width), lambda b, j: (b, order(j), colidx))

    widths = (LANES, LANES, 2 * GLA_DV, LANES)
    ctx_specs = [spec(w, c, lambda j: 0) for w, c in zip(widths, ctx_cols)]
    fwd_specs = [spec(w, c, lat_fwd) for w, c in zip(widths, lat_cols)]
    bwd_specs = [spec(w, c, lat_bwd) for w, c in zip(widths, lat_cols)]

    full = lambda a: pl.BlockSpec(a.shape, lambda b, j: (0,) * a.ndim)
    consts = _chunk_matrices()
    out = jax.ShapeDtypeStruct((bsz, l, 2 * GLA_DV), F32)
    return pl.pallas_call(
        _gla_kernel,
        grid=(bsz // GLA_BATCH, nb),
        in_specs=ctx_specs + fwd_specs + bwd_specs + [full(wupf), full(wupb), full(bf), full(bb)] + [full(a) for a in consts],
        out_specs=[pl.BlockSpec((GLA_BATCH, TOK_TILE, 2 * GLA_DV), lambda b, j: (b, fwd(j), 0)),
                   pl.BlockSpec((GLA_BATCH, TOK_TILE, 2 * GLA_DV), lambda b, j: (b, bwd(j), 0))],
        out_shape=[out, out],
        scratch_shapes=[pltpu.VMEM((GLA_BATCH, 2, GLA_DV, LANES), F32), pltpu.VMEM((GLA_BATCH, 2, GLA_DV, LANES), F32)],
        compiler_params=_cparams(("arbitrary", "arbitrary")),
        name="gla_pair",
    )(*([ctx_src] * 4 + [lat_src] * 8), wupf, wupb, bf, bb, *consts)


def _layer_norm(v, g, b):
    mu = jnp.mean(v, axis=-1, keepdims=True)
    dv = v - mu
    var = jnp.mean(dv * dv, axis=-1, keepdims=True)
    return dv * lax.rsqrt(var + LN_EPS) * g + b


def _outproj_kernel(conv_ref, gate_ref, ofr, obr, ofc_ctx, obc_ctx, ofc_lat, obc_lat, h_ref, gt_ref, wconv_ref, gn_ref,
                    wout_ref, lng_ref, lnb_ref, o_ref, *, alpha, tiles_per_batch):
    tm = TOK_TILE
    is_ctx = (pl.program_id(0) % tiles_per_batch) == 0
    seg = jnp.where(is_ctx, tm, GRID_W)

    cv = conv_ref[...]
    b_gate = cv[:, :CONV_DIM]
    u = cv[:, CONV_DIM:2 * CONV_DIM] * cv[:, 2 * CONV_DIM:]
    pos = lax.broadcasted_iota(I32, (tm, CONV_DIM), 0) & (seg - 1)
    prev = jnp.where(pos == 0, 0.0, pltpu.roll(u, 1, 0))
    nxt = jnp.where(pos == seg - 1, 0.0, pltpu.roll(u, tm - 1, 0))
    wc = wconv_ref[...]
    mix_conv = b_gate * (wc[0:1, :] * prev + wc[1:2, :] * u + wc[2:3, :] * nxt)

    o_r = ofr[...] + obr[...]
    o_c = jnp.where(is_ctx, ofc_ctx[...] + obc_ctx[...], ofc_lat[...] + obc_lat[...])
    gate = gate_ref[...]
    gn = gn_ref[...]
    parts = [mix_conv.astype(BF16)]
    for hd in range(GLA_HEADS):
        src = o_r if hd < 2 else o_c
        o = src[:, (hd % 2) * GLA_DV:(hd % 2 + 1) * GLA_DV]
        o = o * lax.rsqrt(jnp.mean(o * o, axis=-1, keepdims=True) + RMS_EPS) * gn
        g = gate[:, hd * GLA_DV:(hd + 1) * GLA_DV]
        parts.append((o * (g * jax.nn.sigmoid(g))).astype(BF16))
    mix = jnp.concatenate(parts, axis=-1)
    y = _dot(mix, wout_ref[...])
    v = alpha * h_ref[...] + gt_ref[0] * y
    o_ref[...] = _layer_norm(v, lng_ref[...], lnb_ref[...])


def _outproj(p, o_fr, o_br, o_fc, o_bc, o_fc_lat, o_bc_lat, h, gate1, w_conv, g_norm, w_out, ln_g, ln_b, group_of_tile,
             alpha, tiles_per_batch):
    t, d = h.shape
    tm = TOK_TILE
    per = tiles_per_batch
    tok = lambda width, colidx: pl.BlockSpec((tm, width), lambda i: (i, colidx))
    ctx_rows = pl.BlockSpec((tm, 2 * GLA_DV), lambda i: (i // per * per, 0))
    lat_rows = pl.BlockSpec((tm, 2 * GLA_DV), lambda i: (i // per * (per - 1) + jnp.maximum(i % per - 1, 0), 0))
    full = lambda a: pl.BlockSpec(a.shape, lambda i: (0,) * a.ndim)
    return pl.pallas_call(
        functools.partial(_outproj_kernel, alpha=alpha, tiles_per_batch=tiles_per_batch),
        grid=(t // tm,),
        in_specs=[tok(3 * CONV_DIM, P_CONV // (3 * CONV_DIM)), tok(GLA_VAL, P_G // GLA_VAL),
                  tok(2 * GLA_DV, 0), tok(2 * GLA_DV, 0), ctx_rows, ctx_rows, lat_rows, lat_rows,
                  tok(d, 0), pl.BlockSpec((1, 1, d), lambda i: (group_of_tile(i), 0, 0)),
                  full(w_conv), full(g_norm), full(w_out), full(ln_g), full(ln_b)],
        out_specs=pl.BlockSpec((tm, d), lambda i: (i, 0)),
        out_shape=jax.ShapeDtypeStruct((t, d), F32),
        compiler_params=_cparams(("arbitrary",)),
        name="outproj_ln1",
    )(p, p, o_fr, o_br, o_fc, o_bc, o_fc_lat, o_bc_lat, h, gate1, w_conv, g_norm, w_out, ln_g, ln_b)


def _route_kernel(h_ref, sh_ref, sc_ref, wrt_hi_ref, wrt_lo_ref, rbt_ref, upper_ref,
                  idx_ref, wt_ref, rank_ref, cnt_ref, base_ref):
    tm = TOK_TILE
    n_exp = rbt_ref.shape[0]
    gsz = n_exp // N_GROUPS

    @pl.when(pl.program_id(0) == 0)
    def _():
        base_ref[...] = jnp.zeros_like(base_ref)

    u = h_ref[...] * (1.0 + sc_ref[0]) + sh_ref[0]
    u_hi = u.astype(BF16)
    u_lo = (u - u_hi.astype(F32)).astype(BF16)

    wrt_hi = wrt_hi_ref[...]
    logits = _dot_nt(wrt_hi, u_hi) + _dot_nt(wrt_hi, u_lo) + _dot_nt(wrt_lo_ref[...], u_hi)
    s = jax.nn.sigmoid(logits)
    biased = s + rbt_ref[...]

    neg = -jnp.inf
    eidx = lax.broadcasted_iota(I32, (n_exp, tm), 0)
    loc = lax.broadcasted_iota(I32, (gsz, tm), 0)
    gscore = []
    for g in range(N_GROUPS):
        vals = biased[g * gsz:(g + 1) * gsz, :]
        m1 = jnp.max(vals, axis=0, keepdims=True)
        i1 = jnp.min(jnp.where(vals == m1, loc, gsz), axis=0, keepdims=True)
        m2 = jnp.max(jnp.where(loc == i1, neg, vals), axis=0, keepdims=True)
        gscore.append(m1 + m2)
    pieces = []
    for g in range(N_GROUPS):
        ahead = jnp.zeros((1, tm), F32)
        for g2 in range(N_GROUPS):
            if g2 == g:
                continue
            wins = (gscore[g2] >= gscore[g]) if g2 < g else (gscore[g2] > gscore[g])
            ahead = ahead + jnp.where(wins, 1.0, 0.0)
        pieces.append(jnp.where(ahead < TOPK_GROUPS, biased[g * gsz:(g + 1) * gsz, :], neg))
    masked = jnp.concatenate(pieces, axis=0)

    onehot = jnp.zeros((n_exp, tm), F32)
    idxs = []
    for _ in range(TOP_K):
        m = jnp.max(masked, axis=0, keepdims=True)
        i = jnp.min(jnp.where(masked == m, eidx, n_exp), axis=0, keepdims=True)
        hit = eidx == i
        onehot = jnp.where(hit, 1.0, onehot)
        masked = jnp.where(hit, neg, masked)
        idxs.append(i)
    sel = onehot * s
    wnorm = sel / jnp.sum(sel, axis=0, keepdims=True) * ROUTED_SCALE

    onehot_b = onehot.astype(BF16)
    upper = upper_ref[...]
    rank_all = _dot(onehot_b, upper) + base_ref[...]
    base_new = base_ref[...] + _dot(onehot_b, jnp.ones((tm, tm), BF16))
    base_ref[...] = base_new
    cnt_ref[...] = base_new[:, :LANES]

    wts, ranks = [], []
    for k in range(TOP_K):
        hit = eidx == idxs[k]
        wts.append(jnp.sum(jnp.where(hit, wnorm, 0.0), axis=0, keepdims=True))
        ranks.append(jnp.sum(jnp.where(hit, rank_all, 0.0), axis=0, keepdims=True))
    idx_ref[...] = jnp.concatenate(idxs, axis=0)
    wt_ref[...] = jnp.concatenate(wts, axis=0)
    rank_ref[...] = jnp.concatenate(ranks, axis=0).astype(I32)


def _route(h1, shift, scale, wr, rbias, group_of_tile):
    t, d = h1.shape
    tm = TOK_TILE
    n_exp = wr.shape[1]
    wrt = wr.T
    wrt_hi = wrt.astype(BF16)
    wrt_lo = (wrt - wrt_hi.astype(F32)).astype(BF16)
    rbt = jnp.broadcast_to(rbias.reshape(n_exp, 1), (n_exp, tm))
    upper = (jnp.arange(tm)[:, None] < jnp.arange(tm)[None, :]).astype(BF16)
    grp = lambda i: (group_of_tile(i), 0, 0)
    full = lambda a: pl.BlockSpec(a.shape, lambda i: (0,) * a.ndim)
    choice = pl.BlockSpec((TOP_K, tm), lambda i: (0, i))
    choice_out = lambda dt: jax.ShapeDtypeStruct((TOP_K, t), dt)
    return pl.pallas_call(
        _route_kernel,
        grid=(t // tm,),
        in_specs=[pl.BlockSpec((tm, d), lambda i: (i, 0)), pl.BlockSpec((1, 1, d), grp), pl.BlockSpec((1, 1, d), grp),
                  full(wrt_hi), full(wrt_lo), full(rbt), full(upper)],
        out_specs=[choice, choice, choice, pl.BlockSpec((n_exp, LANES), lambda i: (0, 0))],
        out_shape=[choice_out(I32), choice_out(F32), choice_out(I32), jax.ShapeDtypeStruct((n_exp, LANES), F32)],
        scratch_shapes=[pltpu.VMEM((n_exp, tm), F32)],
        compiler_params=_cparams(("arbitrary",)),
        name="route_shared",
    )(h1, shift, scale, wrt_hi, wrt_lo, rbt, upper)


def _dispatch_kernel(idx_ref, rank_ref, pstart_ref, h_ref, sh_ref, sc_ref, wsg_ref, wsu_ref, wsd_ref,
                     xs_hbm, slot_ref, shared_ref, rows, sem):
    u = h_ref[...] * (1.0 + sc_ref[0]) + sh_ref[0]
    for j in range(ROW_TILES):
        rows[pl.ds(j, DSP_TILE, stride=ROW_TILES), :] = u[:, j * LANES:(j + 1) * LANES]

    def body(i, carry):
        src = rows.at[pl.ds(pl.multiple_of(i * ROW_TILES, ROW_TILES), ROW_TILES)]
        for k in range(TOP_K):
            r = i * TOP_K + k
            slot = pstart_ref[idx_ref[r]] + rank_ref[r]
            slot_ref[r] = slot
            pltpu.make_async_copy(src, xs_hbm.at[slot], sem).start(priority=k % DMA_QUEUES)
        return carry

    lax.fori_loop(0, DSP_TILE, body, 0)

    ub = u.astype(BF16)
    shared_ref[...] = _dot((jax.nn.silu(_dot(ub, wsg_ref[...])) * _dot(ub, wsu_ref[...])).astype(BF16), wsd_ref[...])

    done = xs_hbm.at[pl.ds(0, DSP_TILE * TOP_K)]
    pltpu.make_async_copy(done, done, sem).wait()


def _dispatch(h1, shift, scale, wsg, wsu, wsd, idx, rank, pstart, n_slots, group_of_tile):
    t, d = h1.shape
    assert DSP_TILE == TOK_TILE
    choice = pl.BlockSpec((DSP_TILE * TOP_K,), lambda i: (i,), memory_space=pltpu.SMEM)
    grp = lambda i: (group_of_tile(i), 0, 0)
    full = lambda a: pl.BlockSpec(a.shape, lambda i: (0,) * a.ndim)
    tok = pl.BlockSpec((DSP_TILE, d), lambda i: (i, 0))
    return pl.pallas_call(
        _dispatch_kernel,
        grid=(t // DSP_TILE,),
        in_specs=[choice, choice, pl.BlockSpec(memory_space=pltpu.SMEM), tok,
                  pl.BlockSpec((1, 1, d), grp), pl.BlockSpec((1, 1, d), grp), full(wsg), full(wsu), full(wsd)],
        out_specs=[pl.BlockSpec(memory_space=pl.ANY), choice, tok],
        out_shape=[jax.ShapeDtypeStruct((n_slots, ROW_TILES, LANES), F32), jax.ShapeDtypeStruct(idx.shape, I32),
                   jax.ShapeDtypeStruct((t, d), F32)],
        scratch_shapes=[pltpu.VMEM((DSP_TILE * ROW_TILES, LANES), F32), pltpu.SemaphoreType.DMA(())],
        compiler_params=_cparams(("arbitrary",)),
        name="dispatch",
    )(idx, rank, pstart, h1, shift, scale, wsg, wsu, wsd)


def _expert_kernel(gstart_ref, nblk_ref, ntot_ref, xs_hbm, wg_ref, wu_ref, wd_ref, ys_hbm,
                   xbuf, ybuf, wg_b, wu_b, wd_b, sem_in, sem_out):
    e = pl.program_id(0)
    nb = nblk_ref[e]
    g0 = gstart_ref[e]
    total = ntot_ref[0]
    blk_rows = MOE_BLOCK * ROW_TILES
    ahead = EXP_BUFS - 1

    def rows_of(g):
        return pl.ds(pl.multiple_of(g * blk_rows, blk_rows), blk_rows)

    def in_copy(g):
        s = g % EXP_BUFS
        return pltpu.make_async_copy(xs_hbm.at[rows_of(g)], xbuf.at[s], sem_in.at[s])

    def out_copy(g):
        s = g % EXP_BUFS
        return pltpu.make_async_copy(ybuf.at[s], ys_hbm.at[rows_of(g)], sem_out.at[s])

    @pl.when(e == 0)
    def _():
        for g in range(ahead):
            @pl.when(g < total)
            def _():
                in_copy(g).start()

    @pl.when(nb > 0)
    def _():
        wg_b[...] = wg_ref[0, 0].astype(BF16)
        wu_b[...] = wu_ref[0, 0].astype(BF16)
        wd_b[...] = wd_ref[0, 0].astype(BF16)

    def body(b, carry):
        g = g0 + b
        s = g % EXP_BUFS

        @pl.when(g + ahead < total)
        def _():
            in_copy(g + ahead).start()

        in_copy(g).wait()

        @pl.when(g >= EXP_BUFS)
        def _():
            out_copy(g - EXP_BUFS).wait()

        x = jnp.concatenate([xbuf[s, pl.ds(j, MOE_BLOCK, stride=ROW_TILES), :] for j in range(ROW_TILES)], axis=-1)
        xb = x.astype(BF16)
        hid = jax.nn.silu(_dot(xb, wg_b[...])) * _dot(xb, wu_b[...])
        y = _dot(hid.astype(BF16), wd_b[...])
        for j in range(ROW_TILES):
            ybuf[s, pl.ds(j, MOE_BLOCK, stride=ROW_TILES), :] = y[:, j * LANES:(j + 1) * LANES]
        out_copy(g).start()
        return carry

    lax.fori_loop(0, nb, body, 0)

    @pl.when(e == pl.num_programs(0) - 1)
    def _():
        for back in range(EXP_BUFS):
            @pl.when(total - 1 - back >= 0)
            def _():
                out_copy(total - 1 - back).wait()


def _experts(xs2, gstart, nblk, ntot, layer, w_gate, w_up, w_down):
    n_rows = xs2.shape[0]
    _, n_exp, d, hdim = w_gate.shape
    blk_rows = MOE_BLOCK * ROW_TILES
    wmap = lambda e, gs, nb, nt: (layer, e, 0, 0)
    return pl.pallas_call(
        _expert_kernel,
        grid_spec=pltpu.PrefetchScalarGridSpec(
            num_scalar_prefetch=3,
            grid=(n_exp,),
            in_specs=[pl.BlockSpec(memory_space=pl.ANY),
                      pl.BlockSpec((1, 1, d, hdim), wmap),
                      pl.BlockSpec((1, 1, d, hdim), wmap),
                      pl.BlockSpec((1, 1, hdim, d), wmap)],
            out_specs=pl.BlockSpec(memory_space=pl.ANY),
            scratch_shapes=[pltpu.VMEM((EXP_BUFS, blk_rows, LANES), F32), pltpu.VMEM((EXP_BUFS, blk_rows, LANES), F32),
                            pltpu.VMEM((d, hdim), BF16), pltpu.VMEM((d, hdim), BF16), pltpu.VMEM((hdim, d), BF16),
                            pltpu.SemaphoreType.DMA((EXP_BUFS,)), pltpu.SemaphoreType.DMA((EXP_BUFS,))],
        ),
        out_shape=jax.ShapeDtypeStruct((n_rows, LANES), F32),
        compiler_params=_cparams(("arbitrary",)),
        name="experts",
    )(gstart, nblk, ntot, xs2, w_gate, w_up, w_down)


def _combine_kernel(slot_cur, slot_nxt, ys_hbm, wt_ref, shared_ref, h_ref, gt_ref, lng_ref, lnb_ref, o_ref,
                    buf, mixed, sem, *, alpha):
    i = pl.program_id(0)
    n = pl.num_programs(0)
    tt = CMB_TILE

    cur = i % 2
    unroll = 4

    def issue_token(slot_ref, b, tok):
        for k in range(TOP_K):
            r = tok * TOP_K + k
            dst = buf.at[b, pl.ds(pl.multiple_of(r * ROW_TILES, ROW_TILES), ROW_TILES)]
            pltpu.make_async_copy(ys_hbm.at[slot_ref[r]], dst, sem.at[b]).start(priority=k % DMA_QUEUES)

    def mix_token(tok):
        first = tok * TOP_K
        acc = None
        for k in range(TOP_K):
            row = buf[cur, pl.ds(pl.multiple_of((first + k) * ROW_TILES, ROW_TILES), ROW_TILES), :]
            term = wt_ref[first + k] * row
            acc = term if acc is None else acc + term
        mixed[pl.ds(pl.multiple_of(tok * ROW_TILES, ROW_TILES), ROW_TILES), :] = acc

    @pl.when(i == 0)
    def _():
        def body(tok, carry):
            issue_token(slot_cur, 0, tok)
            return carry
        lax.fori_loop(0, tt, body, 0)

    pltpu.make_async_copy(buf.at[cur], buf.at[cur], sem.at[cur]).wait()

    @pl.when(i + 1 < n)
    def _():
        def body(step, carry):
            for u in range(unroll):
                issue_token(slot_nxt, (i + 1) % 2, step * unroll + u)
                mix_token(step * unroll + u)
            return carry
        lax.fori_loop(0, tt // unroll, body, 0)

    @pl.when(i + 1 == n)
    def _():
        def body(step, carry):
            for u in range(unroll):
                mix_token(step * unroll + u)
            return carry
        lax.fori_loop(0, tt // unroll, body, 0)

    routed = jnp.concatenate([mixed[pl.ds(j, tt, stride=ROW_TILES), :] for j in range(ROW_TILES)], axis=-1)
    f = shared_ref[...] + routed
    v = alpha * h_ref[...] + gt_ref[0] * f
    o_ref[...] = _layer_norm(v, lng_ref[...], lnb_ref[...])


def _combine(ys3, slots, wts, shared, h1, gate2, ln_g, ln_b, group_of_tile, alpha, latent_only_tiles=None):
    t, d = h1.shape
    tt = CMB_TILE
    n = t // tt
    scale = TOK_TILE // tt
    if latent_only_tiles is None:
        out_rows, out_block = t, lambda i: (i, 0)
    else:
        assert tt == TOK_TILE
        per = latent_only_tiles
        out_rows = t - (n // per) * tt
        out_block = lambda i: ((i // per) * (per - 1) + jnp.maximum(i % per - 1, 0), 0)
    full = lambda a: pl.BlockSpec(a.shape, lambda i: (0,) * a.ndim)
    cur = pl.BlockSpec((tt * TOP_K,), lambda i: (i,), memory_space=pltpu.SMEM)
    nxt = pl.BlockSpec((tt * TOP_K,), lambda i: (jnp.minimum(i + 1, n - 1),), memory_space=pltpu.SMEM)
    return pl.pallas_call(
        functools.partial(_combine_kernel, alpha=alpha),
        grid=(n,),
        in_specs=[cur, nxt, pl.BlockSpec(memory_space=pl.ANY), cur,
                  pl.BlockSpec((tt, d), lambda i: (i, 0)),
                  pl.BlockSpec((tt, d), lambda i: (i, 0)),
                  pl.BlockSpec((1, 1, d), lambda i: (group_of_tile(i // scale), 0, 0)),
                  full(ln_g), full(ln_b)],
        out_specs=pl.BlockSpec((tt, d), out_block),
        out_shape=jax.ShapeDtypeStruct((out_rows, d), F32),
        scratch_shapes=[pltpu.VMEM((2, tt * TOP_K * ROW_TILES, LANES), F32), pltpu.VMEM((tt * ROW_TILES, LANES), F32),
                        pltpu.SemaphoreType.DMA((2,))],
        compiler_params=_cparams(("arbitrary",)),
        name="combine_ln2",
    )(slots, slots, ys3, wts, shared, h1, gate2, ln_g, ln_b)


def _reorder_w_in(w):
    d = w.shape[0]
    o = 0
    k = w[:, o:o + GLA_KEY]; o += GLA_KEY
    v = w[:, o:o + GLA_VAL]; o += GLA_VAL
    dec = w[:, o:o + 2 * GLA_LOWRANK]; o += 2 * GLA_LOWRANK
    q = w[:, o:o + GLA_KEY]; o += GLA_KEY
    g = w[:, o:o + GLA_VAL]; o += GLA_VAL
    conv = w[:, o:o + 3 * CONV_DIM]
    pad = jnp.zeros((d, LANES - 2 * GLA_LOWRANK), w.dtype)
    hk, hv = 2 * GLA_DK, 2 * GLA_DV
    pairs = [jnp.concatenate([v[:, i * hv:(i + 1) * hv], k[:, i * hk:(i + 1) * hk], q[:, i * hk:(i + 1) * hk]], axis=1)
             for i in range(2)]
    return jnp.concatenate([conv, g, pairs[0], pairs[1], dec, pad], axis=1).astype(BF16)


def _decay_weights(w_up, b_dec, pair):
    lo = pair * 2 * GLA_DK
    nr = 2 * GLA_LOWRANK
    outs = []
    for d in range(2):
        w = jnp.zeros((nr, LANES), F32).at[d * GLA_LOWRANK:(d + 1) * GLA_LOWRANK, :].set(w_up[d][:, lo:lo + LANES])
        w_hi = w.astype(BF16)
        w_lo = (w - w_hi.astype(F32)).astype(BF16)
        packed = jnp.concatenate([w_hi, w_hi, w_lo, jnp.zeros((LANES - 3 * nr, LANES), BF16)], axis=0)
        outs.append((packed, b_dec[d][lo:lo + LANES].reshape(1, LANES)))
    return outs[0][0], outs[1][0], outs[0][1], outs[1][1]


def _latent_to_scan_order(a, ctx_len, rows):
    b, _, f = a.shape
    return a[:, ctx_len:].reshape(b, rows, GRID_W, f).swapaxes(1, 2).reshape(b, rows * GRID_W, f)


def _latent_from_scan_order(a, ctx_len, rows):
    b, _, f = a.shape
    return a[:, ctx_len:].reshape(b, GRID_W, rows, f).swapaxes(1, 2).reshape(b * rows * GRID_W, f)


def kernel(x, c, ctx, c_ctx, w_mod, b_mod, w_in, w_conv, w_decay_up, b_decay, g_gla_norm, w_out, ln1_g, ln1_b, w_router, router_bias, w_e_gate, w_e_up, w_e_down, w_s_gate, w_s_up, w_s_down, ln2_g, ln2_b):
    bsz, seq, d = x.shape
    ctx_len = ctx.shape[1]
    depth = w_mod.shape[0]
    n_exp = w_router.shape[2]
    rows = seq // GRID_W
    l = ctx_len + seq
    t = bsz * l
    assert ctx_len == TOK_TILE and seq % TOK_TILE == 0 and d == ROW_TILES * LANES
    assert t % DSP_TILE == 0 and bsz + 1 <= SUBLANES and bsz % GLA_BATCH == 0
    tiles_per_batch = l // TOK_TILE
    alpha = float((2 * depth) ** 0.25)

    def group_of_tile(i):
        return jnp.where(i % tiles_per_batch == 0, bsz, i // tiles_per_batch)

    cond8 = jnp.zeros((SUBLANES, d), F32).at[:bsz].set(c).at[bsz].set(c_ctx)
    mod = _modulation(cond8, w_mod, b_mod).reshape(depth, SUBLANES, 6, 1, d)

    nk = t * TOP_K
    n_blocks = -(-(nk + n_exp * (MOE_BLOCK - 1)) // MOE_BLOCK)
    n_slots = n_blocks * MOE_BLOCK

    h = jnp.concatenate([ctx, x], axis=1).reshape(t, d)
    for i in range(depth):
        m = [mod[i, :, j] for j in range(6)]
        p = _inproj(h, m[0], m[1], _reorder_w_in(w_in[i]), group_of_tile)
        p3 = p.reshape(bsz, l, P_COLS)

        wf, wb, bf, bb = _decay_weights(w_decay_up[i], b_decay[i], 0)
        pair_cols = lambda base: ((base + PAIR_K) // LANES, (base + PAIR_Q) // LANES, (base + PAIR_V) // (2 * GLA_DV),
                                  P_DEC // LANES)
        o_fr, o_br = _gla_pair(p3, pair_cols(P_ROW), p3, pair_cols(P_ROW), wf, wb, bf, bb)
        col_in = _latent_to_scan_order(p3[..., P_COL:], ctx_len, rows)
        wf, wb, bf, bb = _decay_weights(w_decay_up[i], b_decay[i], 1)
        o_fc, o_bc = _gla_pair(p3, pair_cols(P_COL), col_in,
                               (PAIR_K // LANES, PAIR_Q // LANES, PAIR_V // (2 * GLA_DV), (P_DEC - P_COL) // LANES),
                               wf, wb, bf, bb)

        flat = lambda a: a.reshape(t, a.shape[-1])
        h1 = _outproj(p, flat(o_fr), flat(o_br), flat(o_fc), flat(o_bc), _latent_from_scan_order(o_fc, ctx_len, rows),
                      _latent_from_scan_order(o_bc, ctx_len, rows), h, m[2], w_conv[i],
                      g_gla_norm[i].reshape(1, GLA_DV), w_out[i].astype(BF16), ln1_g[i].reshape(1, d), ln1_b[i].reshape(1, d),
                      group_of_tile, alpha, tiles_per_batch)

        idx, wts, rank, counts = _route(h1, m[3], m[4], w_router[i], router_bias[i], group_of_tile)

        cnt = counts[:, 0].astype(I32)
        padded = (cnt + MOE_BLOCK - 1) // MOE_BLOCK * MOE_BLOCK
        pend = jnp.cumsum(padded)
        pstart = pend - padded
        idx, rank, wts = idx.T.reshape(nk), rank.T.reshape(nk), wts.T.reshape(nk)

        xs3, slots, shared = _dispatch(h1, m[3], m[4], w_s_gate[i].astype(BF16), w_s_up[i].astype(BF16),
                                       w_s_down[i].astype(BF16), idx, rank, pstart, n_slots, group_of_tile)
        ys2 = _experts(xs3.reshape(n_slots * ROW_TILES, LANES), pstart // MOE_BLOCK, padded // MOE_BLOCK,
                       (pend[-1] // MOE_BLOCK).reshape(1), i, w_e_gate, w_e_up, w_e_down)
        h = _combine(ys2.reshape(n_slots, ROW_TILES, LANES), slots, wts, shared, h1, m[5],
                     ln2_g[i].reshape(1, d), ln2_b[i].reshape(1, d), group_of_tile, alpha,
                     latent_only_tiles=tiles_per_batch if i == depth - 1 else None)

    return h.reshape(bsz, seq, d)
```
